```python
import jax
import jax.numpy as jnp
from jax import lax
import numpy as np

D_MODEL = 1024
BATCH = 32
SEQ = 256
DEPTH = 4
DEC_BATCH = 2
DEC_SEQ = 4096
PAST_LEN = 512

GRID_W = 64
EPS = 1e-6
NEG_BIG = -1e30
MAX_EXP_ARG = 80.0
CHUNK = 32
MIX_W = D_MODEL // 2
H_A = 4
DH_A = MIX_W // H_A
H_B = 4
DK_B = MIX_W // H_B
DV_B = MIX_W // H_B
H_C = 4
DK_C = MIX_W // (2 * H_C)
DV_C = MIX_W // H_C
R_C = 16
GLA_TAU = 16.0
H_D = 4
DH_D = MIX_W // H_D
CONV_W = 4
LRU_C = 8.0
N_BRANCH = 4
MLSTM_F_BIAS = 3.0
N_KEYS = 128
N_EXPERTS = N_KEYS * N_KEYS
PEER_HEADS = 8
PEER_DK = 256
PEER_TOPK = 16
PEER_BLOCK = 128
POS_BASE = 10000.0

SEGMENTS = (
    ('a_q', H_A * DH_A), ('a_k', H_A * DH_A), ('a_v', H_A * DH_A), ('a_o', H_A * DH_A),
    ('a_i', 2 * H_A), ('a_f', 2 * H_A),
    ('b_q', H_B * DK_B), ('b_f', 2 * H_B * DK_B), ('b_i', H_B * DV_B), ('b_g', H_B * DV_B),
    ('c_q', H_C * DK_C), ('c_k', H_C * DK_C), ('c_v', H_C * DV_C), ('c_g', H_C * DV_C),
    ('c_lr', 2 * R_C),
    ('d_x', H_D * DH_D), ('d_g', H_D * DH_D),
    ('gates', N_BRANCH * D_MODEL),
)
N_IN = sum(width for _, width in SEGMENTS)

kernel_name = 'hybrid_bidir_recurrent_peer_diffusion_step'


def segment_bounds(name):
    start = 0
    for seg, width in SEGMENTS:
        if seg == name:
            return start, start + width
        start += width
    raise KeyError(name)


def split_columns(z):
    cols = {}
    start = 0
    for name, width in SEGMENTS:
        cols[name] = z[..., start:start + width]
        start += width
    return cols


def rmsnorm(x, g):
    xf = x.astype(jnp.float32)
    return xf * lax.rsqrt(jnp.mean(xf * xf, axis=-1, keepdims=True) + EPS) * g.astype(jnp.float32)


def head_rms_norm(h):
    return h * lax.rsqrt(jnp.mean(h * h, axis=-1, keepdims=True) + EPS)


def flip(a):
    return jnp.flip(a, axis=1)


def to_chunks(a):
    bsz, t = a.shape[0], a.shape[1]
    a = a.reshape(bsz, t // CHUNK, CHUNK, *a.shape[2:])
    return jnp.moveaxis(a, (1, 3), (0, 2))


def from_chunks(a):
    a = jnp.moveaxis(a, (0, 2), (1, 3))
    return a.reshape(a.shape[0], a.shape[1] * a.shape[2], *a.shape[3:])


def mlstm_scan(q, k, v, log_i, log_f, C0, n0, m0):
    causal = jnp.tril(jnp.ones((CHUNK, CHUNK), dtype=bool))

    def step(carry, xs):
        C, n, m = carry
        qc, kc, vc, li, lf = xs
        b = jnp.cumsum(lf, axis=-1)
        logw = jnp.where(causal, b[..., :, None] - b[..., None, :] + li[..., None, :], NEG_BIG)
        from_state = b + m[..., None]
        m_t = jnp.maximum(from_state, jnp.max(logw, axis=-1))
        w_state = jnp.exp(from_state - m_t)
        scores = jnp.einsum('bhtd,bhsd->bhts', qc, kc) * jnp.exp(logw - m_t[..., None])
        num = (w_state[..., None] * jnp.einsum('bhtd,bhde->bhte', qc, C)
               + jnp.einsum('bhts,bhse->bhte', scores, vc))
        den = w_state * jnp.einsum('bhtd,bhd->bht', qc, n) + jnp.sum(scores, axis=-1)
        floor = jnp.exp(jnp.minimum(-m_t, MAX_EXP_ARG))
        h = num / jnp.maximum(jnp.abs(den), floor)[..., None]
        m_new = m_t[..., -1]
        w_new = jnp.exp(b[..., -1:] - b + li - m_new[..., None])
        decay = jnp.exp(b[..., -1] + m - m_new)
        C_new = decay[..., None, None] * C + jnp.einsum('bhs,bhsd,bhse->bhde', w_new, kc, vc)
        n_new = decay[..., None] * n + jnp.einsum('bhs,bhsd->bhd', w_new, kc)
        return (C_new, n_new, m_new), h

    xs = tuple(to_chunks(a) for a in (q, k, v, log_i, log_f))
    (C, n, m), h = lax.scan(step, (C0, n0, m0), xs)
    return from_chunks(h), (C, n, m)


def gla_scan(q, k, v, log_a, S0):
    causal = jnp.tril(jnp.ones((CHUNK, CHUNK), dtype=bool))[:, :, None]

    def step(S, xs):
        qc, kc, vc, la = xs
        b = jnp.cumsum(la, axis=2)
        rel = jnp.where(causal, b[:, :, :, None, :] - b[:, :, None, :, :], 0.0)
        decay_ts = jnp.where(causal, jnp.exp(rel), 0.0)
        scores = jnp.einsum('bhtsd,bhsd->bhts', qc[:, :, :, None, :] * decay_ts, kc)
        o = (jnp.einsum('bhtd,bhde->bhte', qc * jnp.exp(b), S)
             + jnp.einsum('bhts,bhse->bhte', scores, vc))
        b_last = b[:, :, -1]
        S_new = (jnp.exp(b_last)[..., None] * S
                 + jnp.einsum('bhsd,bhse->bhde', kc * jnp.exp(b_last[:, :, None] - b), vc))
        return S_new, o

    xs = tuple(to_chunks(a) for a in (q, k, v, log_a))
    S, o = lax.scan(step, S0, xs)
    return from_chunks(o), (S,)


def hgrn_scan(q, v, k, log_f, S0):
    return gla_scan(q, k, v, log_f, S0)


def lru_scan(a, u, h0):
    def combine(left, right):
        a_l, u_l = left
        a_r, u_r = right
        return a_l * a_r, a_r * u_l + u_r
    a_cum, u_cum = lax.associative_scan(combine, (a, u), axis=1)
    h = a_cum * h0[:, None, :] + u_cum
    return h, (h[:, -1],)


def run_bidirectional(scan_fn, shared, per_dir, state):
    out_f, st_f = scan_fn(*shared, *(a[:, :, 0] for a in per_dir), *(s[:, 0] for s in state))
    out_b, st_b = scan_fn(*(flip(a) for a in shared), *(flip(a[:, :, 1]) for a in per_dir),
                          *(s[:, 1] for s in state))
    new_state = tuple(jnp.stack([f, b], axis=1) for f, b in zip(st_f, st_b))
    return out_f + flip(out_b), new_state


def depthwise_conv(x, w, b):
    left = (CONV_W - 1) // 2
    y = lax.conv_general_dilated(x, w.astype(x.dtype)[:, None, :], window_strides=(1,),
                                 padding=[(left, CONV_W - 1 - left)],
                                 dimension_numbers=('NWC', 'WIO', 'NWC'),
                                 feature_group_count=x.shape[-1])
    return y + b.astype(x.dtype)


def grid_position_embedding(rows):
    quarter = D_MODEL // 4
    omega = 1.0 / (POS_BASE ** (jnp.arange(quarter, dtype=jnp.float32) / quarter))
    r, col = jnp.meshgrid(jnp.arange(rows, dtype=jnp.float32),
                          jnp.arange(GRID_W, dtype=jnp.float32), indexing='ij')

    def enc(pos):
        ang = pos.reshape(-1, 1) * omega
        return jnp.concatenate([jnp.sin(ang), jnp.cos(ang)], axis=-1)
    return jnp.concatenate([enc(r), enc(col)], axis=-1)


def token_mixers(h, lp, state):
    bsz, t, _ = h.shape
    p = split_columns((h @ lp['w_in'] + lp['b_in']).astype(jnp.float32))
    C0, n0, m0, Sb0, Sc0, hd0 = state

    qa = p['a_q'].reshape(bsz, t, H_A, DH_A)
    ka = p['a_k'].reshape(bsz, t, H_A, DH_A) * DH_A ** -0.5
    va = p['a_v'].reshape(bsz, t, H_A, DH_A)
    li = p['a_i'].reshape(bsz, t, 2, H_A)
    lf = jax.nn.log_sigmoid(p['a_f'].reshape(bsz, t, 2, H_A))
    ha, st_a = run_bidirectional(mlstm_scan, (qa, ka, va), (li, lf), (C0, n0, m0))
    ya = jax.nn.sigmoid(p['a_o']) * head_rms_norm(ha).reshape(bsz, t, H_A * DH_A)

    lb = lp['hgrn_lb']
    zf = p['b_f'].reshape(bsz, t, 2, H_B * DK_B)
    log_fb = jax.nn.log_sigmoid(zf) + jnp.log1p(lb * jnp.exp(jnp.minimum(-zf, MAX_EXP_ARG)))
    kb = (1.0 - lb) * jax.nn.sigmoid(-zf)
    log_fb = log_fb.reshape(bsz, t, 2, H_B, DK_B)
    kb = kb.reshape(bsz, t, 2, H_B, DK_B)
    qb = p['b_q'].reshape(bsz, t, H_B, DK_B) * DK_B ** -0.5
    vb = p['b_i'].reshape(bsz, t, H_B, DV_B)
    hb, st_b = run_bidirectional(hgrn_scan, (qb, vb), (kb, log_fb), (Sb0,))
    yb = jax.nn.silu(p['b_g']) * head_rms_norm(hb).reshape(bsz, t, H_B * DV_B)

    qc = p['c_q'].reshape(bsz, t, H_C, DK_C) * DK_C ** -0.5
    kc = p['c_k'].reshape(bsz, t, H_C, DK_C)
    vc = p['c_v'].reshape(bsz, t, H_C, DV_C)
    lr = p['c_lr'].reshape(bsz, t, 2, R_C)
    zg = jnp.einsum('btnr,nrk->btnk', lr, lp['gla_up'].astype(jnp.float32)) + lp['gla_up_b']
    gc = (jax.nn.log_sigmoid(zg) / GLA_TAU).reshape(bsz, t, 2, H_C, DK_C)
    hc, st_c = run_bidirectional(gla_scan, (qc, kc, vc), (gc,), (Sc0,))
    yc = jax.nn.silu(p['c_g']) * head_rms_norm(hc).reshape(bsz, t, H_C * DV_C)

    xd = depthwise_conv(p['d_x'], lp['conv_w'], lp['conv_b'])
    xdh = xd.reshape(bsz, t, H_D, DH_D)
    r = jax.nn.sigmoid(jnp.einsum('bthi,nhij->btnhj', xdh, lp['lru_w_a'].astype(jnp.float32))
                       .reshape(bsz, t, 2, H_D * DH_D) + lp['lru_b_a'])
    ig = jax.nn.sigmoid(jnp.einsum('bthi,nhij->btnhj', xdh, lp['lru_w_x'].astype(jnp.float32))
                        .reshape(bsz, t, 2, H_D * DH_D) + lp['lru_b_x'])
    log_a = -LRU_C * r * jax.nn.softplus(-lp['lru_lambda'].astype(jnp.float32))
    a = jnp.exp(log_a)
    u = jnp.sqrt(jnp.maximum(-jnp.expm1(2.0 * log_a), 0.0)) * (ig * xd[:, :, None, :])
    hd, st_d = run_bidirectional(lru_scan, (), (a, u), (hd0,))
    yd = hd * jax.nn.gelu(p['d_g'])

    branches = jnp.stack([ya, yb, yc, yd], axis=0)
    proj = jnp.einsum('nbtw,nwd->nbtd', branches, lp['w_branch'])
    gates = jax.nn.sigmoid(p['gates'].reshape(bsz, t, N_BRANCH, D_MODEL))
    merged = jnp.einsum('btnd,nbtd->btd', gates, proj)
    out = merged @ lp['w_out']
    return out, (*st_a, *st_b, *st_c, *st_d)


def peer_ffn(h, w_q, sub_keys, u, v):
    bsz, t, d = h.shape

    def block(xb):
        q = (xb @ w_q).astype(jnp.float32).reshape(-1, PEER_HEADS, 2, PEER_DK // 2)
        s = jnp.einsum('nhpk,hpek->nhpe', q, sub_keys.astype(jnp.float32))
        s1, i1 = lax.top_k(s[:, :, 0], PEER_TOPK)
        s2, i2 = lax.top_k(s[:, :, 1], PEER_TOPK)
        cand_s = (s1[..., :, None] + s2[..., None, :]).reshape(-1, PEER_HEADS, PEER_TOPK * PEER_TOPK)
        cand_i = (i1[..., :, None] * N_KEYS + i2[..., None, :]).reshape(-1, PEER_HEADS, PEER_TOPK * PEER_TOPK)
        top_s, pos = lax.top_k(cand_s, PEER_TOPK)
        idx = jnp.take_along_axis(cand_i, pos, axis=-1)
        g = jax.nn.softmax(top_s, axis=-1)
        act = jax.nn.gelu(jnp.einsum('nd,nhkd->nhk', xb, u[idx]).astype(jnp.float32)) * g
        return jnp.einsum('nhk,nhkd->nd', act, v[idx])

    y = lax.map(block, h.reshape(-1, PEER_BLOCK, d))
    return y.reshape(bsz, t, d)


def trunk_layer(x, cond, lp, state):
    mod = (jax.nn.silu(cond.astype(jnp.float32)) @ lp['w_mod'] + lp['b_mod'])[:, None, :]
    sh1, sc1, g1, sh2, sc2, g2 = jnp.split(mod, 6, axis=-1)
    h = rmsnorm(x, lp['norm1_g']) * (1.0 + sc1) + sh1
    mix, new_state = token_mixers(h, lp, state)
    x = x + (g1 * mix).astype(x.dtype)
    h = rmsnorm(x, lp['norm2_g']) * (1.0 + sc2) + sh2
    ffn = peer_ffn(h, lp['peer_w_q'], lp['peer_sub_keys'], lp['peer_u'], lp['peer_v'])
    x = x + (g2 * ffn).astype(x.dtype)
    return x, new_state


def setup_inputs(seed: int = 0) -> dict:
    key = jax.random.key(seed)
    ks = jax.random.split(key, 40)
    f32 = jnp.float32
    d = D_MODEL
    w_mix = H_D * DH_D

    def nrm(i, shape, scale):
        return jax.random.normal(ks[i], shape, f32) * scale

    f0, f1 = segment_bounds('a_f')
    b_in = nrm(16, (DEPTH, N_IN), 0.02).at[:, f0:f1].add(MLSTM_F_BIAS)
    a0 = jax.random.uniform(ks[26], (DEPTH, 2, w_mix), f32, 0.9, 0.999) ** (1.0 / LRU_C)
    lru_lambda = jnp.log(a0) - jnp.log1p(-a0)
    return {
        'x_prompt': nrm(0, (BATCH, SEQ, d), 1.0),
        'x_sample': nrm(1, (DEC_BATCH, DEC_SEQ, d), 1.0),
        'state_mlstm_C': nrm(2, (DEC_BATCH, DEPTH, 2, H_A, DH_A, DH_A), 0.5),
        'state_mlstm_n': nrm(3, (DEC_BATCH, DEPTH, 2, H_A, DH_A), 0.5),
        'state_mlstm_m': nrm(4, (DEC_BATCH, DEPTH, 2, H_A), 1.0),
        'state_hgrn_S': nrm(5, (DEC_BATCH, DEPTH, 2, H_B, DK_B, DV_B), 0.5),
        'state_gla_S': nrm(6, (DEC_BATCH, DEPTH, 2, H_C, DK_C, DV_C), 1.0),
        'state_lru_h': nrm(7, (DEC_BATCH, DEPTH, 2, w_mix), 0.5),
        'c': nrm(8, (DEC_BATCH, d), 1.0),
        'c_ctx': nrm(9, (d,), 1.0),
        'norm1_g': 1.0 + nrm(10, (DEPTH, d), 0.02),
        'norm2_g': 1.0 + nrm(11, (DEPTH, d), 0.02),
        'final_norm_g': 1.0 + nrm(12, (d,), 0.02),
        'w_mod': nrm(13, (DEPTH, d, 6 * d), 0.5 * d ** -0.5),
        'b_mod': nrm(14, (DEPTH, 6 * d), 0.02),
        'w_in': nrm(15, (DEPTH, d, N_IN), d ** -0.5),
        'b_in': b_in,
        'w_gla_up': nrm(17, (DEPTH, 2, R_C, H_C * DK_C), R_C ** -0.5),
        'b_gla_up': nrm(18, (DEPTH, 2, H_C * DK_C), 0.02),
        'hgrn_lower_bounds': nrm(19, (DEPTH, 2, H_B * DK_B), 0.1),
        'conv_w': nrm(20, (DEPTH, CONV_W, w_mix), CONV_W ** -0.5),
        'conv_b': nrm(21, (DEPTH, w_mix), 0.02),
        'lru_w_a': nrm(22, (DEPTH, 2, H_D, DH_D, DH_D), DH_D ** -0.5),
        'lru_b_a': nrm(23, (DEPTH, 2, w_mix), 0.02),
        'lru_w_x': nrm(24, (DEPTH, 2, H_D, DH_D, DH_D), DH_D ** -0.5),
        'lru_b_x': nrm(25, (DEPTH, 2, w_mix), 0.02),
        'lru_lambda': lru_lambda,
        'w_branch': nrm(27, (DEPTH, N_BRANCH, MIX_W, d), MIX_W ** -0.5),
        'w_out': nrm(28, (DEPTH, d, d), d ** -0.5),
        'peer_w_q': nrm(29, (DEPTH, d, PEER_HEADS * PEER_DK), d ** -0.5),
        'peer_sub_keys': nrm(30, (DEPTH, PEER_HEADS, 2, N_KEYS, PEER_DK // 2), (PEER_DK // 2) ** -0.5),
        'peer_u': nrm(31, (DEPTH, N_EXPERTS, d), d ** -0.5),
        'peer_v': nrm(32, (DEPTH, N_EXPERTS, d), PEER_HEADS ** -0.5),
    }


def reference(x_prompt, x_sample, state_mlstm_C, state_mlstm_n, state_mlstm_m, state_hgrn_S,
              state_gla_S, state_lru_h, c, c_ctx, norm1_g, norm2_g, final_norm_g, w_mod, b_mod,
              w_in, b_in, w_gla_up, b_gla_up, hgrn_lower_bounds, conv_w, conv_b, lru_w_a, lru_b_a,
              lru_w_x, lru_b_x, lru_lambda, w_branch, w_out, peer_w_q, peer_sub_keys, peer_u, peer_v):
    f32 = jnp.float32
    lb_soft = jax.nn.softmax(hgrn_lower_bounds.astype(f32), axis=0)
    lb_all = jnp.cumsum(lb_soft, axis=0) - lb_soft[0:1]

    def layer_params(l):
        return {
            'norm1_g': norm1_g[l], 'norm2_g': norm2_g[l], 'w_mod': w_mod[l], 'b_mod': b_mod[l],
            'w_in': w_in[l], 'b_in': b_in[l], 'gla_up': w_gla_up[l], 'gla_up_b': b_gla_up[l],
            'hgrn_lb': lb_all[l], 'conv_w': conv_w[l], 'conv_b': conv_b[l],
            'lru_w_a': lru_w_a[l], 'lru_b_a': lru_b_a[l], 'lru_w_x': lru_w_x[l],
            'lru_b_x': lru_b_x[l], 'lru_lambda': lru_lambda[l], 'w_branch': w_branch[l],
            'w_out': w_out[l], 'peer_w_q': peer_w_q[l], 'peer_sub_keys': peer_sub_keys[l],
            'peer_u': peer_u[l], 'peer_v': peer_v[l],
        }

    bp = x_prompt.shape[0]
    w_mix = H_D * DH_D
    zero_state = (
        jnp.zeros((bp, 2, H_A, DH_A, DH_A), f32), jnp.zeros((bp, 2, H_A, DH_A), f32),
        jnp.zeros((bp, 2, H_A), f32), jnp.zeros((bp, 2, H_B, DK_B, DV_B), f32),
        jnp.zeros((bp, 2, H_C, DK_C, DV_C), f32), jnp.zeros((bp, 2, w_mix), f32),
    )
    x = x_prompt
    ctx_states = []
    for l in range(DEPTH):
        x, st = trunk_layer(x, c_ctx[None, :], layer_params(l), zero_state)
        ctx_states.append(st)
    y_prompt = rmsnorm(x, final_norm_g)
    new_mlstm_C = jnp.stack([s[0] for s in ctx_states], axis=1)
    new_mlstm_n = jnp.stack([s[1] for s in ctx_states], axis=1)
    new_mlstm_m = jnp.stack([s[2] for s in ctx_states], axis=1)
    new_hgrn_S = jnp.stack([s[3] for s in ctx_states], axis=1)
    new_gla_S = jnp.stack([s[4] for s in ctx_states], axis=1)
    new_lru_h = jnp.stack([s[5] for s in ctx_states], axis=1)

    rows = x_sample.shape[1] // GRID_W
    x = x_sample + grid_position_embedding(rows).astype(x_sample.dtype)[None]
    for l in range(DEPTH):
        cached = (state_mlstm_C[:, l].astype(f32), state_mlstm_n[:, l].astype(f32),
                  state_mlstm_m[:, l].astype(f32), state_hgrn_S[:, l].astype(f32),
                  state_gla_S[:, l].astype(f32), state_lru_h[:, l].astype(f32))
        x, _ = trunk_layer(x, c, layer_params(l), cached)
    y_sample = rmsnorm(x, final_norm_g)
    return (y_prompt, y_sample, new_mlstm_C, new_mlstm_n, new_mlstm_m, new_hgrn_S, new_gla_S, new_lru_h)
```

```python
import functools

import jax
import jax.numpy as jnp
from jax import lax
from jax.experimental import pallas as pl
from jax.experimental.pallas import tpu as pltpu

F32 = jnp.float32
BF16 = jnp.bfloat16
HIGHEST = lax.Precision.HIGHEST

D_MODEL = 1024
DEPTH = 4
GRID_W = 64
EPS = 1e-6
NEG_BIG = -1e30
MAX_EXP_ARG = 80.0
MIX_W = 512
N_HEAD = 4
DH = 128
DK_C = 64
R_C = 16
GLA_TAU = 16.0
CONV_W = 4
LRU_C = 8.0
N_BRANCH = 4
N_KEYS = 128
N_EXPERTS = N_KEYS * N_KEYS
PEER_HEADS = 8
PEER_TOPK = 16
POS_BASE = 10000.0

LANE = 128
VMEM_LIMIT = 56 * 1024 * 1024

COL_AQ, COL_AK, COL_AV, COL_AO = 0, 512, 1024, 1536
COL_BQ, COL_BF, COL_BI, COL_BG = 2048, 2560, 3584, 4096
COL_CQ, COL_CK, COL_CV, COL_CG = 4608, 4864, 5120, 5632
COL_DX, COL_DG = 6144, 6656
COL_GATES = 7168
COL_SMALL = 11264
SMALL_W = 256
N_PACK = COL_SMALL + SMALL_W

_ORIG_AI = 2048
_ORIG_BQ = 2064
_ORIG_CLR = 6160
_ORIG_DX = 6192
_ORIG_END = 11312

CHUNK_A = 128
CHUNK_G = 64
SUB_G = 16
CHUNK_D = 128


def _cparams(sem):
    return pltpu.CompilerParams(dimension_semantics=sem, vmem_limit_bytes=VMEM_LIMIT)


def _bdot(a, b):
    return jnp.dot(a.astype(BF16), b.astype(BF16), preferred_element_type=F32)


def _bdot_nt(a, b):
    return lax.dot_general(a.astype(BF16), b.astype(BF16), (((1,), (1,)), ((), ())),
                           preferred_element_type=F32)


def _bdot_tn(a, b):
    return lax.dot_general(a.astype(BF16), b.astype(BF16), (((0,), (0,)), ((), ())),
                           preferred_element_type=F32)


def _split(a):
    hi = a.astype(BF16)
    lo = (a - hi.astype(F32)).astype(BF16)
    return hi, lo


def _dot3(a, b_hi, b_lo):
    a_hi, a_lo = _split(a)
    return (jnp.dot(a_hi, b_hi, preferred_element_type=F32)
            + jnp.dot(a_lo, b_hi, preferred_element_type=F32)
            + jnp.dot(a_hi, b_lo, preferred_element_type=F32))


def _log_sigmoid(z):
    return jnp.minimum(z, 0.0) - jnp.log1p(jnp.exp(-jnp.abs(z)))


def _sigmoid(z):
    return 1.0 / (1.0 + jnp.exp(-z))


def _gelu(x):
    return 0.5 * x * (1.0 + jnp.tanh(0.7978845608028654 * (x + 0.044715 * x * x * x)))


def _silu(x):
    return x * _sigmoid(x)


def _expm1(y):
    u = jnp.exp(y)
    near = (u - 1.0) * y / jnp.log(u)
    return jnp.where(u == 1.0, y, jnp.where(u < 0.5, u - 1.0, near))


def _mod_kernel(c_ref, w_ref, b_ref, o_ref):
    a = _silu(c_ref[...])
    o_ref[0] = jnp.dot(a, w_ref[0], precision=HIGHEST, preferred_element_type=F32) + b_ref[0]


def _modulation(cond, w_mod, b_mod):
    r = cond.shape[0]
    tc = 1536
    return pl.pallas_call(
        _mod_kernel,
        grid=(DEPTH, 6 * D_MODEL // tc),
        in_specs=[pl.BlockSpec((r, D_MODEL), lambda l, j: (0, 0)),
                  pl.BlockSpec((1, D_MODEL, tc), lambda l, j: (l, 0, j)),
                  pl.BlockSpec((1, 1, tc), lambda l, j: (l, 0, j))],
        out_specs=pl.BlockSpec((1, r, tc), lambda l, j: (l, 0, j)),
        out_shape=jax.ShapeDtypeStruct((DEPTH, r, 6 * D_MODEL), F32),
        compiler_params=_cparams(("parallel", "parallel")),
        name="modulation",
    )(cond, w_mod, b_mod.reshape(DEPTH, 1, 6 * D_MODEL))


def _addpos_kernel(x_ref, p_ref, o_ref):
    o_ref[0] = x_ref[0] + p_ref[...]


def _add_position(x, pos):
    b, t, d = x.shape
    tt = 512
    return pl.pallas_call(
        _addpos_kernel,
        grid=(b, t // tt),
        in_specs=[pl.BlockSpec((1, tt, d), lambda i, j: (i, j, 0)),
                  pl.BlockSpec((tt, d), lambda i, j: (j, 0))],
        out_specs=pl.BlockSpec((1, tt, d), lambda i, j: (i, j, 0)),
        out_shape=jax.ShapeDtypeStruct(x.shape, F32),
        compiler_params=_cparams(("parallel", "parallel")),
        name="add_position",
    )(x, pos)


def _inproj_kernel(x_ref, mod_ref, g_ref, w_ref, b_ref, o_ref, h_ref):
    @pl.when(pl.program_id(1) == 0)
    def _():
        x = x_ref[...]
        inv = lax.rsqrt(jnp.mean(x * x, axis=-1, keepdims=True) + EPS)
        sh = mod_ref[0, :, 0:D_MODEL]
        sc = mod_ref[0, :, D_MODEL:2 * D_MODEL]
        h_ref[...] = (x * inv * g_ref[...] * (1.0 + sc) + sh).astype(BF16)

    o_ref[...] = jnp.dot(h_ref[...], w_ref[...], preferred_element_type=F32) + b_ref[...]


def _inproj(x, mod_l, norm_g, w_pack, b_pack, seq_len):
    n = x.shape[0]
    tn, tc = 512, 1152
    rows = mod_l.shape[0]
    if rows == 1:
        mod_map = lambda i, j: (0, 0, 0)
    else:
        mod_map = lambda i, j: ((i * tn) // seq_len, 0, 0)
    return pl.pallas_call(
        _inproj_kernel,
        grid=(n // tn, N_PACK // tc),
        in_specs=[pl.BlockSpec((tn, D_MODEL), lambda i, j: (i, 0)),
                  pl.BlockSpec((1, 1, 6 * D_MODEL), mod_map),
                  pl.BlockSpec((1, D_MODEL), lambda i, j: (0, 0)),
                  pl.BlockSpec((D_MODEL, tc), lambda i, j: (0, j)),
                  pl.BlockSpec((1, tc), lambda i, j: (0, j))],
        out_specs=pl.BlockSpec((tn, tc), lambda i, j: (i, j)),
        out_shape=jax.ShapeDtypeStruct((n, N_PACK), F32),
        scratch_shapes=[pltpu.VMEM((tn, D_MODEL), BF16)],
        compiler_params=_cparams(("parallel", "arbitrary")),
        name="in_projection",
    )(x, mod_l, norm_g, w_pack, b_pack)


def _causal_mask(length, rev):
    row = lax.broadcasted_iota(jnp.int32, (length, length), 0)
    col = lax.broadcasted_iota(jnp.int32, (length, length), 1)
    return (col >= row) if rev else (col <= row)


def _mlstm_kernel(*refs, has_init, emit_state):
    L = CHUNK_A
    ins = list(refs)
    dirs = [ins[0:4], ins[4:8]]
    pos = 8
    if has_init:
        c0_ref, n0_ref, m0_ref = ins[pos:pos + 3]
        pos += 3
    outs = ins[pos:pos + 2]
    pos += 2
    if emit_state:
        cout_ref, nout_ref, mout_ref = ins[pos:pos + 3]
        pos += 3
    cs_ref, ns_ref, ms_ref = ins[pos:pos + 3]
    c = pl.program_id(1)

    @pl.when(c == 0)
    def _():
        if has_init:
            cs_ref[...] = c0_ref[0]
            ns_ref[...] = n0_ref[0]
            ms_ref[...] = m0_ref[0]
        else:
            cs_ref[...] = jnp.zeros_like(cs_ref)
            ns_ref[...] = jnp.zeros_like(ns_ref)
            ms_ref[...] = jnp.zeros_like(ms_ref)

    for d in range(2):
        rev = d == 1
        q_ref, k_ref, v_ref, s_ref = dirs[d]
        o_ref = outs[d]
        mask = _causal_mask(L, rev)
        tri = mask.astype(F32)
        sm = s_ref[:, 0:LANE]
        lf = _log_sigmoid(sm)
        bc = jnp.dot(tri, lf, precision=HIGHEST, preferred_element_type=F32)
        sm_t = sm.T
        bc_t = bc.T
        last = 0 if rev else L - 1
        for h in range(N_HEAD):
            ci, cf = d * N_HEAD + h, 2 * N_HEAD + d * N_HEAD + h
            li_c, b_c = sm[:, ci:ci + 1], bc[:, cf:cf + 1]
            li_r, b_r = sm_t[ci:ci + 1, :], bc_t[cf:cf + 1, :]
            m_prev = ms_ref[d, h][:, 0:1]
            logw = jnp.where(mask, b_c - b_r + li_r, NEG_BIG)
            from_state = b_c + m_prev
            m_t = jnp.maximum(from_state, jnp.max(logw, axis=-1, keepdims=True))
            w_state = jnp.exp(from_state - m_t)
            hs = slice(h * DH, (h + 1) * DH)
            q = q_ref[:, hs]
            k = k_ref[:, hs] * (DH ** -0.5)
            v = v_ref[:, hs]
            scores = _bdot_nt(q, k) * jnp.exp(logw - m_t)
            c_st = cs_ref[d, h]
            n_st = ns_ref[d, h]
            num = w_state * _bdot(q, c_st) + _bdot(scores, v)
            den = (w_state * jnp.sum(q * n_st, axis=-1, keepdims=True)
                   + jnp.sum(scores, axis=-1, keepdims=True))
            floor = jnp.exp(jnp.minimum(-m_t, MAX_EXP_ARG))
            o_ref[:, hs] = num / jnp.maximum(jnp.abs(den), floor)
            m_new = m_t[last:last + 1, :]
            b_last = b_c[last:last + 1, :]
            kw = k * jnp.exp(b_last - b_c + li_c - m_new)
            decay = jnp.exp(b_last + m_prev - m_new)
            cs_ref[d, h] = decay * c_st + _bdot_tn(kw, v)
            ns_ref[d, h] = decay * n_st + jnp.sum(kw, axis=0, keepdims=True)
            ms_ref[d, h] = jnp.broadcast_to(m_new, (1, LANE))

    if emit_state:
        @pl.when(c == pl.num_programs(1) - 1)
        def _():
            cout_ref[0] = cs_ref[...]
            nout_ref[0] = ns_ref[...]
            mout_ref[0] = ms_ref[...]


def _mlstm(p, bsz, seq, state, emit_state):
    L = CHUNK_A
    nc = seq // L
    n = bsz * seq
    has_init = state is not None

    def fwd(cb):
        return lambda b, c: (b * nc + c, cb)

    def bwd(cb):
        return lambda b, c: (b * nc + (nc - 1 - c), cb)

    in_specs, args = [], []
    for mk in (fwd, bwd):
        for col in (COL_AQ, COL_AK, COL_AV):
            in_specs.append(pl.BlockSpec((L, MIX_W), mk(col // MIX_W)))
            args.append(p)
        in_specs.append(pl.BlockSpec((L, SMALL_W), mk(COL_SMALL // SMALL_W)))
        args.append(p)
    st_specs = [pl.BlockSpec((1, 2, N_HEAD, DH, DH), lambda b, c: (b, 0, 0, 0, 0)),
                pl.BlockSpec((1, 2, N_HEAD, 1, DH), lambda b, c: (b, 0, 0, 0, 0)),
                pl.BlockSpec((1, 2, N_HEAD, 1, LANE), lambda b, c: (b, 0, 0, 0, 0))]
    st_shapes = [jax.ShapeDtypeStruct((bsz, 2, N_HEAD, DH, DH), F32),
                 jax.ShapeDtypeStruct((bsz, 2, N_HEAD, 1, DH), F32),
                 jax.ShapeDtypeStruct((bsz, 2, N_HEAD, 1, LANE), F32)]
    if has_init:
        c0, n0, m0 = state
        in_specs += st_specs
        args += [c0, n0.reshape(bsz, 2, N_HEAD, 1, DH),
                 jnp.broadcast_to(m0[..., None, None], (bsz, 2, N_HEAD, 1, LANE))]
    out_specs = [pl.BlockSpec((L, MIX_W), fwd(0)), pl.BlockSpec((L, MIX_W), bwd(0))]
    out_shape = [jax.ShapeDtypeStruct((n, MIX_W), F32)] * 2
    if emit_state:
        out_specs += st_specs
        out_shape += st_shapes
    res = pl.pallas_call(
        functools.partial(_mlstm_kernel, has_init=has_init, emit_state=emit_state),
        grid=(bsz, nc),
        in_specs=in_specs,
        out_specs=out_specs,
        out_shape=out_shape,
        scratch_shapes=[pltpu.VMEM((2, N_HEAD, DH, DH), F32),
                        pltpu.VMEM((2, N_HEAD, 1, DH), F32),
                        pltpu.VMEM((2, N_HEAD, 1, LANE), F32)],
        compiler_params=_cparams(("parallel", "arbitrary")),
        name="mlstm_scan",
    )(*args)
    o_f, o_b = res[0], res[1]
    new_state = None
    if emit_state:
        new_state = (res[2], res[3].reshape(bsz, 2, N_HEAD, DH), res[4][:, :, :, 0, 0])
    return o_f, o_b, new_state


def _gla_direction(q, k, la, v, s_ref, d, o_ref, *, rev, dk):
    L, S = CHUNK_G, SUB_G
    nsub = L // S
    mask = _causal_mask(L, rev)
    tri = mask.astype(F32)
    b = jnp.dot(tri, la, precision=HIGHEST, preferred_element_type=F32)
    bx = b - la
    ref_rows = [bx[i * S + (S - 1 if rev else 0):i * S + (S - 1 if rev else 0) + 1, :]
                for i in range(nsub)]
    bref = jnp.concatenate([jnp.broadcast_to(r, (S, r.shape[1])) for r in ref_rows], axis=0)
    q_hat = q * jnp.exp(b - bref)
    rowid = lax.broadcasted_iota(jnp.int32, (L, 1), 0)
    k_hat = []
    for i in range(nsub):
        seen = (rowid >= i * S) if rev else (rowid < (i + 1) * S)
        k_hat.append(k * jnp.exp(jnp.where(seen, ref_rows[i] - b, NEG_BIG)))
    last = 0 if rev else L - 1
    b_last = b[last:last + 1, :]
    q_state = q * jnp.exp(b)
    k_state = k * jnp.exp(b_last - b)
    e_last = jnp.exp(b_last)
    for h in range(N_HEAD):
        ks = slice(h * dk, (h + 1) * dk)
        vs = slice(h * DH, (h + 1) * DH)
        vh = v[:, vs]
        blocks = [_bdot_nt(q_hat[i * S:(i + 1) * S, ks], k_hat[i][:, ks]) for i in range(nsub)]
        scores = jnp.where(mask, jnp.concatenate(blocks, axis=0), 0.0)
        s_t = s_ref[d, h]
        o_ref[:, vs] = _bdot_nt(q_state[:, ks], s_t) + _bdot(scores, vh)
        s_ref[d, h] = s_t * e_last[:, ks] + _bdot_tn(vh, k_state[:, ks])


def _scan_state_io(ins, pos, has_init, emit_state, n_out):
    s0_ref = sout_ref = None
    if has_init:
        s0_ref = ins[pos]
        pos += 1
    outs = ins[pos:pos + n_out]
    pos += n_out
    if emit_state:
        sout_ref = ins[pos]
        pos += 1
    return s0_ref, outs, sout_ref, ins[pos]


def _scan_state_init(s_ref, s0_ref):
    @pl.when(pl.program_id(1) == 0)
    def _():
        if s0_ref is not None:
            s_ref[...] = s0_ref[0]
        else:
            s_ref[...] = jnp.zeros_like(s_ref)


def _scan_state_emit(s_ref, sout_ref):
    if sout_ref is not None:
        @pl.when(pl.program_id(1) == pl.num_programs(1) - 1)
        def _():
            sout_ref[0] = s_ref[...]


def _hgrn_kernel(*refs, has_init, emit_state):
    ins = list(refs)
    lb_ref = ins[6]
    s0_ref, outs, sout_ref, s_ref = _scan_state_io(ins, 7, has_init, emit_state, 2)
    _scan_state_init(s_ref, s0_ref)
    for d in range(2):
        q_ref, z_ref, v_ref = ins[3 * d:3 * d + 3]
        z = z_ref[...]
        lb = lb_ref[d:d + 1, :]
        la = _log_sigmoid(z) + jnp.log1p(lb * jnp.exp(jnp.minimum(-z, MAX_EXP_ARG)))
        k = (1.0 - lb) * _sigmoid(-z)
        q = q_ref[...] * (DH ** -0.5)
        _gla_direction(q, k, la, v_ref[...], s_ref, d, outs[d], rev=d == 1, dk=DH)
    _scan_state_emit(s_ref, sout_ref)


def _gla_kernel(*refs, has_init, emit_state):
    ins = list(refs)
    up_ref, upb_ref = ins[8], ins[9]
    s0_ref, outs, sout_ref, s_ref = _scan_state_io(ins, 10, has_init, emit_state, 2)
    _scan_state_init(s_ref, s0_ref)
    for d in range(2):
        q_ref, k_ref, v_ref, sm_ref = ins[4 * d:4 * d + 4]
        zg = jnp.dot(sm_ref[...], up_ref[d], precision=HIGHEST,
                     preferred_element_type=F32) + upb_ref[d:d + 1, :]
        la = _log_sigmoid(zg) * (1.0 / GLA_TAU)
        q = q_ref[...] * (DK_C ** -0.5)
        _gla_direction(q, k_ref[...], la, v_ref[...], s_ref, d, outs[d], rev=d == 1, dk=DK_C)
    _scan_state_emit(s_ref, sout_ref)


def _gated_scan(kernel, p, bsz, seq, cols, extra, extra_specs, dk, state_t, emit_state, name):
    L = CHUNK_G
    nc = seq // L
    n = bsz * seq
    has_init = state_t is not None

    def fwd(cb):
        return lambda b, c: (b * nc + c, cb)

    def bwd(cb):
        return lambda b, c: (b * nc + (nc - 1 - c), cb)

    in_specs, args = [], []
    for d, mk in enumerate((fwd, bwd)):
        for off, width in cols[d]:
            in_specs.append(pl.BlockSpec((L, width), mk(off // width)))
            args.append(p)
    in_specs += extra_specs
    args += extra
    st_spec = pl.BlockSpec((1, 2, N_HEAD, DH, dk), lambda b, c: (b, 0, 0, 0, 0))
    if has_init:
        in_specs.append(st_spec)
        args.append(state_t)
    out_specs = [pl.BlockSpec((L, MIX_W), fwd(0)), pl.BlockSpec((L, MIX_W), bwd(0))]
    out_shape = [jax.ShapeDtypeStruct((n, MIX_W), F32)] * 2
    if emit_state:
        out_specs.append(st_spec)
        out_shape.append(jax.ShapeDtypeStruct((bsz, 2, N_HEAD, DH, dk), F32))
    res = pl.pallas_call(
        functools.partial(kernel, has_init=has_init, emit_state=emit_state),
        grid=(bsz, nc),
        in_specs=in_specs,
        out_specs=out_specs,
        out_shape=out_shape,
        scratch_shapes=[pltpu.VMEM((2, N_HEAD, DH, dk), F32)],
        compiler_params=_cparams(("parallel", "arbitrary")),
        name=name,
    )(*args)
    return res[0], res[1], (res[2] if emit_state else None)


def _hgrn(p, bsz, seq, lb, state_t, emit_state):
    cols = [[(COL_BQ, MIX_W), (COL_BF + d * MIX_W, MIX_W), (COL_BI, MIX_W)] for d in range(2)]
    return _gated_scan(_hgrn_kernel, p, bsz, seq, cols, [lb],
                       [pl.BlockSpec((2, MIX_W), lambda b, c: (0, 0))], DH, state_t, emit_state,
                       "hgrn2_scan")


def _gla(p, bsz, seq, up_pad, up_b, state_t, emit_state):
    kw = N_HEAD * DK_C
    cols = [[(COL_CQ, kw), (COL_CK, kw), (COL_CV, MIX_W), (COL_SMALL, SMALL_W)] for _ in range(2)]
    return _gated_scan(_gla_kernel, p, bsz, seq, cols, [up_pad, up_b],
                       [pl.BlockSpec((2, SMALL_W, kw), lambda b, c: (0, 0, 0)),
                        pl.BlockSpec((2, kw), lambda b, c: (0, 0))], DK_C, state_t, emit_state,
                       "gla_scan")


def _shift_rows(x, k, rev, fill):
    n = x.shape[0]
    rowid = lax.broadcasted_iota(jnp.int32, x.shape, 0)
    if rev:
        return jnp.where(rowid >= n - k, fill, pltpu.roll(x, n - k, 0))
    return jnp.where(rowid < k, fill, pltpu.roll(x, k, 0))


def _lru_kernel(*refs, seq, has_init, emit_state):
    L = CHUNK_D
    nc = seq // L
    ins = list(refs)
    (dx_ref, dg_ref, cw_ref, cb_ref, wah_ref, wal_ref, ba_ref, wxh_ref, wxl_ref, bx_ref,
     lam_ref) = ins[0:11]
    pos = 11
    h0_ref = hout_ref = None
    if has_init:
        h0_ref = ins[pos]
        pos += 1
    y_ref = ins[pos]
    pos += 1
    if emit_state:
        hout_ref = ins[pos]
        pos += 1
    pad_ref = ins[pos]

    zeros8 = jnp.zeros((8, LANE), F32)
    pad_ref[0:8, :] = zeros8
    pad_ref[8:8 + seq, :] = dx_ref[...]
    pad_ref[8 + seq:16 + seq, :] = zeros8
    lam = lam_ref[...]
    sp = jnp.maximum(-lam, 0.0) + jnp.log1p(jnp.exp(-jnp.abs(lam)))
    cw = cw_ref[...]
    cb = cb_ref[...]

    def chunk_scan(c, carry, d):
        rev = d == 1
        start = pl.multiple_of(c * L, L)
        win = pad_ref[pl.ds(start, L + 16), :]
        xd = cb + sum(cw[j:j + 1, :] * win[7 + j:7 + j + L, :] for j in range(CONV_W))
        r = _sigmoid(_dot3(xd, wah_ref[d, 0], wal_ref[d, 0]) + ba_ref[d:d + 1, :])
        ig = _sigmoid(_dot3(xd, wxh_ref[d, 0], wxl_ref[d, 0]) + bx_ref[d:d + 1, :])
        log_a = -LRU_C * r * sp[d:d + 1, :]
        a = jnp.exp(log_a)
        u = jnp.sqrt(jnp.maximum(-_expm1(2.0 * log_a), 0.0)) * (ig * xd)
        k = 1
        while k < L:
            u = a * _shift_rows(u, k, rev, 0.0) + u
            a = a * _shift_rows(a, k, rev, 1.0)
            k *= 2
        h = a * carry + u
        last = 0 if rev else L - 1
        return start, h, h[last:last + 1, :]

    def fwd_body(c, carry):
        start, h, new = chunk_scan(c, carry, 0)
        y_ref[pl.ds(start, L), :] = h
        return new

    def bwd_body(i, carry):
        start, h, new = chunk_scan(nc - 1 - i, carry, 1)
        y_ref[pl.ds(start, L), :] = (y_ref[pl.ds(start, L), :] + h) * _gelu(dg_ref[pl.ds(start, L), :])
        return new

    if has_init:
        init_f, init_b = h0_ref[0, 0:1, :], h0_ref[0, 1:2, :]
    else:
        init_f = init_b = jnp.zeros((1, LANE), F32)
    fin_f = lax.fori_loop(0, nc, fwd_body, init_f)
    fin_b = lax.fori_loop(0, nc, bwd_body, init_b)
    if emit_state:
        hout_ref[0, 0:1, :] = fin_f
        hout_ref[0, 1:2, :] = fin_b


def _lru(p, bsz, seq, conv_w, conv_b, wa, ba, wx, bx, lam, h0, emit_state):
    n = bsz * seq
    has_init = h0 is not None
    wah, wal = _split(wa)
    wxh, wxl = _split(wx)
    gate_w = pl.BlockSpec((2, 1, DH, DH), lambda b, h: (0, h, 0, 0))
    vec2 = pl.BlockSpec((2, LANE), lambda b, h: (0, h))
    in_specs = [pl.BlockSpec((seq, LANE), lambda b, h: (b, COL_DX // LANE + h)),
                pl.BlockSpec((seq, LANE), lambda b, h: (b, COL_DG // LANE + h)),
                pl.BlockSpec((CONV_W, LANE), lambda b, h: (0, h)),
                pl.BlockSpec((1, LANE), lambda b, h: (0, h)),
                gate_w, gate_w, vec2, gate_w, gate_w, vec2, vec2]
    args = [p, p, conv_w, conv_b.reshape(1, MIX_W), wah, wal, ba, wxh, wxl, bx, lam]
    st_spec = pl.BlockSpec((1, 2, LANE), lambda b, h: (b, 0, h))
    if has_init:
        in_specs.append(st_spec)
        args.append(h0)
    out_specs = [pl.BlockSpec((seq, LANE), lambda b, h: (b, h))]
    out_shape = [jax.ShapeDtypeStruct((n, MIX_W), F32)]
    if emit_state:
        out_specs.append(st_spec)
        out_shape.append(jax.ShapeDtypeStruct((bsz, 2, MIX_W), F32))
    res = pl.pallas_call(
        functools.partial(_lru_kernel, seq=seq, has_init=has_init, emit_state=emit_state),
        grid=(bsz, N_HEAD),
        in_specs=in_specs,
        out_specs=out_specs,
        out_shape=out_shape,
        scratch_shapes=[pltpu.VMEM((seq + 16, LANE), F32)],
        compiler_params=_cparams(("parallel", "parallel")),
        name="conv_rglru",
    )(*args)
    return res[0], (res[1] if emit_state else None)


def _head_rms(x):
    parts = []
    for h in range(N_HEAD):
        xh = x[:, h * DH:(h + 1) * DH]
        parts.append(xh * lax.rsqrt(jnp.mean(xh * xh, axis=-1, keepdims=True) + EPS))
    return jnp.concatenate(parts, axis=-1)


def _merge_kernel(af_ref, ab_ref, bf_ref, bb_ref, cf_ref, cb_ref, yd_ref, ao_ref, bg_ref, cg_ref,
                  gt0_ref, gt1_ref, gt2_ref, gt3_ref, x_ref, mod_ref, g2_ref, wbr_ref, wout_ref,
                  xo_ref, h2_ref, h2b_ref):
    ya = _sigmoid(ao_ref[...]) * _head_rms(af_ref[...] + ab_ref[...])
    yb = _silu(bg_ref[...]) * _head_rms(bf_ref[...] + bb_ref[...])
    yc = _silu(cg_ref[...]) * _head_rms(cf_ref[...] + cb_ref[...])
    merged = None
    gate_refs = (gt0_ref, gt1_ref, gt2_ref, gt3_ref)
    for i, y in enumerate((ya, yb, yc, yd_ref[...])):
        proj = jnp.dot(y.astype(BF16), wbr_ref[i], preferred_element_type=F32)
        term = _sigmoid(gate_refs[i][...]) * proj
        merged = term if merged is None else merged + term
    out = jnp.dot(merged.astype(BF16), wout_ref[...], preferred_element_type=F32)
    g1 = mod_ref[0, :, 2 * D_MODEL:3 * D_MODEL]
    sh2 = mod_ref[0, :, 3 * D_MODEL:4 * D_MODEL]
    sc2 = mod_ref[0, :, 4 * D_MODEL:5 * D_MODEL]
    x = x_ref[...] + g1 * out
    xo_ref[...] = x
    inv = lax.rsqrt(jnp.mean(x * x, axis=-1, keepdims=True) + EPS)
    h2 = x * inv * g2_ref[...] * (1.0 + sc2) + sh2
    h2_ref[...] = h2
    h2b_ref[...] = h2.astype(BF16)


def _merge(mix_outs, yd, p, x, mod_l, norm2_g, w_branch, w_out, seq_len):
    n = x.shape[0]
    tn = 256
    rows = mod_l.shape[0]
    mod_map = (lambda i: (0, 0, 0)) if rows == 1 else (lambda i: ((i * tn) // seq_len, 0, 0))
    tok = lambda cb: (lambda i: (i, cb))
    in_specs = [pl.BlockSpec((tn, MIX_W), tok(0))] * 7
    in_specs += [pl.BlockSpec((tn, MIX_W), tok(COL_AO // MIX_W)),
                 pl.BlockSpec((tn, MIX_W), tok(COL_BG // MIX_W)),
                 pl.BlockSpec((tn, MIX_W), tok(COL_CG // MIX_W)),
                 *[pl.BlockSpec((tn, D_MODEL), tok(COL_GATES // D_MODEL + i)) for i in range(N_BRANCH)],
                 pl.BlockSpec((tn, D_MODEL), tok(0)),
                 pl.BlockSpec((1, 1, 6 * D_MODEL), mod_map),
                 pl.BlockSpec((1, D_MODEL), lambda i: (0, 0)),
                 pl.BlockSpec((N_BRANCH, MIX_W, D_MODEL), lambda i: (0, 0, 0)),
                 pl.BlockSpec((D_MODEL, D_MODEL), lambda i: (0, 0))]
    out_spec = pl.BlockSpec((tn, D_MODEL), tok(0))
    return pl.pallas_call(
        _merge_kernel,
        grid=(n // tn,),
        in_specs=in_specs,
        out_specs=[out_spec, out_spec, out_spec],
        out_shape=[jax.ShapeDtypeStruct((n, D_MODEL), F32), jax.ShapeDtypeStruct((n, D_MODEL), F32),
                   jax.ShapeDtypeStruct((n, D_MODEL), BF16)],
        compiler_params=_cparams(("parallel",)),
        name="branch_merge",
    )(*mix_outs, yd, p, p, p, p, p, p, p, x, mod_l, norm2_g, w_branch, w_out)


def _top16_rows(vals, n_rows):
    rowid = lax.broadcasted_iota(jnp.int32, vals.shape, 0).astype(F32)
    rank = jnp.full(vals.shape, float(PEER_TOPK), F32)
    tops = []
    for r in range(PEER_TOPK):
        m = jnp.max(vals, axis=0, keepdims=True)
        idx = jnp.min(jnp.where(vals == m, rowid, float(n_rows)), axis=0, keepdims=True)
        sel = rowid == idx
        rank = jnp.where(sel, float(r), rank)
        vals = jnp.where(sel, -jnp.inf, vals)
        tops.append(m)
    return jnp.concatenate(tops, axis=0), rank


def _route_kernel(h_ref, wqh_ref, wql_ref, kh_ref, kl_ref, a_ref, b_ref, r2_ref, c_ref):
    tn = h_ref.shape[0]
    K = PEER_TOPK
    q = _dot3(h_ref[...], wqh_ref[...], wql_ref[...])
    q_hi, q_lo = _split(q)
    nt = (((1,), (1,)), ((), ()))
    for h in range(PEER_HEADS):
        st, tops, ranks = [], [], []
        for half in range(2):
            cs = slice((2 * h + half) * N_KEYS, (2 * h + half + 1) * N_KEYS)
            kh, kl = kh_ref[h, half], kl_ref[h, half]
            s_t = (lax.dot_general(kh, q_hi[:, cs], nt, preferred_element_type=F32)
                   + lax.dot_general(kl, q_hi[:, cs], nt, preferred_element_type=F32)
                   + lax.dot_general(kh, q_lo[:, cs], nt, preferred_element_type=F32))
            top, rank = _top16_rows(s_t, N_KEYS)
            st.append(s_t)
            tops.append(top)
            ranks.append(rank)
        s1, s2 = tops
        cand = jnp.concatenate([s1[a:a + 1, :] + s2 for a in range(K)], axis=0)
        posid = lax.broadcasted_iota(jnp.int32, cand.shape, 0).astype(F32)
        chosen = jnp.zeros(cand.shape, F32)
        vals = cand
        for _ in range(K):
            m = jnp.max(vals, axis=0, keepdims=True)
            idx = jnp.min(jnp.where(vals == m, posid, float(K * K)), axis=0, keepdims=True)
            sel = posid == idx
            chosen = jnp.where(sel, 1.0, chosen)
            vals = jnp.where(sel, -jnp.inf, vals)
        e1 = jnp.exp(s1 - s1[0:1, :])
        e2 = jnp.exp(s2 - s2[0:1, :])
        pair = jnp.concatenate([e1[a:a + 1, :] * e2 for a in range(K)], axis=0)
        z = jnp.sum(chosen * pair, axis=0, keepdims=True)
        counts = [jnp.sum(chosen[a * K:(a + 1) * K, :], axis=0, keepdims=True) for a in range(K)]
        rank1, rank2 = ranks
        c_dense = jnp.zeros((N_KEYS, tn), F32)
        for a in range(K):
            c_dense = jnp.where(rank1 == float(a), counts[a], c_dense)
        a_ref[h] = jnp.where(rank1 < float(K), jnp.exp(st[0] - s1[0:1, :]), 0.0) / z
        b_ref[h] = jnp.where(rank2 < float(K), jnp.exp(st[1] - s2[0:1, :]), 0.0)
        r2_ref[h] = rank2
        c_ref[h] = c_dense


def _route(h2, wq_hi, wq_lo, keys_hi, keys_lo):
    n = h2.shape[0]
    tn = 256
    dense = pl.BlockSpec((PEER_HEADS, N_KEYS, tn), lambda i: (0, 0, i))
    dshape = jax.ShapeDtypeStruct((PEER_HEADS, N_KEYS, n), F32)
    wspec = pl.BlockSpec((D_MODEL, PEER_HEADS * 2 * N_KEYS), lambda i: (0, 0))
    kspec = pl.BlockSpec((PEER_HEADS, 2, N_KEYS, N_KEYS), lambda i: (0, 0, 0, 0))
    return pl.pallas_call(
        _route_kernel,
        grid=(n // tn,),
        in_specs=[pl.BlockSpec((tn, D_MODEL), lambda i: (i, 0)), wspec, wspec, kspec, kspec],
        out_specs=[dense] * 4,
        out_shape=[dshape] * 4,
        compiler_params=_cparams(("parallel",)),
        name="peer_route",
    )(h2, wq_hi, wq_lo, keys_hi, keys_lo)


PEER_TN = 512
PEER_TE = 1024


def _expert_kernel(hb_ref, u_ref, vt_ref, a_ref, b_ref, r2_ref, c_ref, x_ref, mod_ref, o_ref,
                   acc_ref):
    e = pl.program_id(1)

    @pl.when(e == 0)
    def _():
        acc_ref[...] = jnp.zeros_like(acc_ref)

    t_t = lax.dot_general(u_ref[...], hb_ref[...], (((1,), (1,)), ((), ())),
                          preferred_element_type=F32)
    act = _gelu(t_t)
    rows_per_tile = PEER_TE // N_KEYS
    w_parts = []
    for j in range(rows_per_tile):
        e1 = e * rows_per_tile + j
        g = None
        for h in range(PEER_HEADS):
            a_row = a_ref[h, pl.ds(e1, 1), :]
            c_row = c_ref[h, pl.ds(e1, 1), :]
            term = a_row * jnp.where(r2_ref[h] < c_row, b_ref[h], 0.0)
            g = term if g is None else g + term
        w_parts.append((g * act[j * N_KEYS:(j + 1) * N_KEYS, :]).astype(BF16))
    w_t = jnp.concatenate(w_parts, axis=0)
    acc_ref[...] += jnp.dot(vt_ref[...], w_t, preferred_element_type=F32)

    @pl.when(e == pl.num_programs(1) - 1)
    def _():
        g2 = mod_ref[0, :, 5 * D_MODEL:6 * D_MODEL]
        o_ref[...] = x_ref[...] + g2 * acc_ref[...].T


def _experts(h2b, u_b, vt_b, dense, x, mod_l, seq_len):
    n = x.shape[0]
    tn, te = PEER_TN, PEER_TE
    rows = mod_l.shape[0]
    mod_map = (lambda i, e: (0, 0, 0)) if rows == 1 else (lambda i, e: ((i * tn) // seq_len, 0, 0))
    dspec = pl.BlockSpec((PEER_HEADS, N_KEYS, tn), lambda i, e: (0, 0, i))
    return pl.pallas_call(
        _expert_kernel,
        grid=(n // tn, N_EXPERTS // te),
        in_specs=[pl.BlockSpec((tn, D_MODEL), lambda i, e: (i, 0)),
                  pl.BlockSpec((te, D_MODEL), lambda i, e: (e, 0)),
                  pl.BlockSpec((D_MODEL, te), lambda i, e: (0, e)),
                  dspec, dspec, dspec, dspec,
                  pl.BlockSpec((tn, D_MODEL), lambda i, e: (i, 0)),
                  pl.BlockSpec((1, 1, 6 * D_MODEL), mod_map)],
        out_specs=pl.BlockSpec((tn, D_MODEL), lambda i, e: (i, 0)),
        out_shape=jax.ShapeDtypeStruct((n, D_MODEL), F32),
        scratch_shapes=[pltpu.VMEM((D_MODEL, tn), F32)],
        compiler_params=_cparams(("parallel", "arbitrary")),
        name="peer_experts",
    )(h2b, u_b, vt_b, *dense, x, mod_l)


def _final_norm_kernel(x_ref, g_ref, o_ref):
    x = x_ref[...]
    o_ref[...] = x * lax.rsqrt(jnp.mean(x * x, axis=-1, keepdims=True) + EPS) * g_ref[...]


def _final_norm(x, g):
    n = x.shape[0]
    tn = 512
    return pl.pallas_call(
        _final_norm_kernel,
        grid=(n // tn,),
        in_specs=[pl.BlockSpec((tn, D_MODEL), lambda i: (i, 0)),
                  pl.BlockSpec((1, D_MODEL), lambda i: (0, 0))],
        out_specs=pl.BlockSpec((tn, D_MODEL), lambda i: (i, 0)),
        out_shape=jax.ShapeDtypeStruct((n, D_MODEL), F32),
        compiler_params=_cparams(("parallel",)),
        name="final_norm",
    )(x, g)


def _pack_columns(a):
    small = jnp.concatenate([a[..., _ORIG_AI:_ORIG_BQ], a[..., _ORIG_CLR:_ORIG_DX]], axis=-1)
    pad = jnp.zeros(a.shape[:-1] + (SMALL_W - small.shape[-1],), a.dtype)
    return jnp.concatenate([a[..., 0:_ORIG_AI], a[..., _ORIG_BQ:_ORIG_CLR], a[..., _ORIG_DX:_ORIG_END],
                            small, pad], axis=-1)


def _position_code(rows):
    quarter = D_MODEL // 4
    omega = 1.0 / (POS_BASE ** (jnp.arange(quarter, dtype=F32) / quarter))
    r, col = jnp.meshgrid(jnp.arange(rows, dtype=F32), jnp.arange(GRID_W, dtype=F32), indexing='ij')

    def enc(pos):
        ang = pos.reshape(-1, 1) * omega
        return jnp.concatenate([jnp.sin(ang), jnp.cos(ang)], axis=-1)
    return jnp.concatenate([enc(r), enc(col)], axis=-1)


def _layer(x, bsz, seq, mod_l, lp, state, emit_state):
    p = _inproj(x, mod_l, lp['norm1_g'], lp['w_in'], lp['b_in'], seq)
    if state is None:
        st_a = st_b = st_c = st_d = None
    else:
        c0, n0, m0, sb0, sc0, hd0 = state
        st_a = (c0, n0, m0)
        st_b = jnp.swapaxes(sb0, -1, -2)
        st_c = jnp.swapaxes(sc0, -1, -2)
        st_d = hd0
    a_f, a_b, new_a = _mlstm(p, bsz, seq, st_a, emit_state)
    b_f, b_b, new_b = _hgrn(p, bsz, seq, lp['hgrn_lb'], st_b, emit_state)
    c_f, c_b, new_c = _gla(p, bsz, seq, lp['gla_up_pad'], lp['gla_up_b'], st_c, emit_state)
    yd, new_d = _lru(p, bsz, seq, lp['conv_w'], lp['conv_b'], lp['lru_w_a'], lp['lru_b_a'],
                     lp['lru_w_x'], lp['lru_b_x'], lp['lru_lambda'], st_d, emit_state)
    x1, h2, h2b = _merge((a_f, a_b, b_f, b_b, c_f, c_b), yd, p, x, mod_l, lp['norm2_g'],
                         lp['w_branch'], lp['w_out'], seq)
    dense = _route(h2, lp['wq_hi'], lp['wq_lo'], lp['keys_hi'], lp['keys_lo'])
    x2 = _experts(h2b, lp['peer_u'], lp['peer_vt'], dense, x1, mod_l, seq)
    new_state = None
    if emit_state:
        new_state = (*new_a, jnp.swapaxes(new_b, -1, -2), jnp.swapaxes(new_c, -1, -2), new_d)
    return x2, new_state


def kernel(x_prompt, x_sample, state_mlstm_C, state_mlstm_n, state_mlstm_m, state_hgrn_S,
           state_gla_S, state_lru_h, c, c_ctx, norm1_g, norm2_g, final_norm_g, w_mod, b_mod,
           w_in, b_in, w_gla_up, b_gla_up, hgrn_lower_bounds, conv_w, conv_b, lru_w_a, lru_b_a,
           lru_w_x, lru_b_x, lru_lambda, w_branch, w_out, peer_w_q, peer_sub_keys, peer_u, peer_v):
    lb_soft = jax.nn.softmax(hgrn_lower_bounds.astype(F32), axis=0)
    lb_all = jnp.cumsum(lb_soft, axis=0) - lb_soft[0:1]
    w_pack = _pack_columns(w_in).astype(BF16)
    b_pack = _pack_columns(b_in).reshape(DEPTH, 1, N_PACK)
    kw = N_HEAD * DK_C
    up_pad = jnp.zeros((DEPTH, 2, SMALL_W, kw), F32)
    for d in range(2):
        lo = 2 * N_HEAD * 2 + d * R_C
        up_pad = up_pad.at[:, d, lo:lo + R_C, :].set(w_gla_up[:, d].astype(F32))
    wq_hi, wq_lo = _split(peer_w_q)
    keys_hi, keys_lo = _split(peer_sub_keys)
    u_b = peer_u.astype(BF16)
    vt_b = jnp.swapaxes(peer_v, 1, 2).astype(BF16)
    wbr_b = w_branch.astype(BF16)
    wout_b = w_out.astype(BF16)

    def layer_params(l):
        return {
            'norm1_g': norm1_g[l].reshape(1, D_MODEL), 'norm2_g': norm2_g[l].reshape(1, D_MODEL),
            'w_in': w_pack[l], 'b_in': b_pack[l], 'hgrn_lb': lb_all[l],
            'gla_up_pad': up_pad[l], 'gla_up_b': b_gla_up[l],
            'conv_w': conv_w[l], 'conv_b': conv_b[l], 'lru_w_a': lru_w_a[l], 'lru_b_a': lru_b_a[l],
            'lru_w_x': lru_w_x[l], 'lru_b_x': lru_b_x[l], 'lru_lambda': lru_lambda[l],
            'w_branch': wbr_b[l], 'w_out': wout_b[l], 'wq_hi': wq_hi[l], 'wq_lo': wq_lo[l],
            'keys_hi': keys_hi[l], 'keys_lo': keys_lo[l], 'peer_u': u_b[l], 'peer_vt': vt_b[l],
        }

    cond = jnp.concatenate([c, c_ctx[None, :]], axis=0).astype(F32)
    n_dec = c.shape[0]
    mod = _modulation(cond, w_mod, b_mod)
    final_g = final_norm_g.reshape(1, D_MODEL)

    bp, tp, _ = x_prompt.shape
    x = x_prompt.reshape(bp * tp, D_MODEL)
    ctx_states = []
    for l in range(DEPTH):
        mod_l = mod[l, n_dec:n_dec + 1].reshape(1, 1, 6 * D_MODEL)
        x, st = _layer(x, bp, tp, mod_l, layer_params(l), None, True)
        ctx_states.append(st)
    y_prompt = _final_norm(x, final_g).reshape(bp, tp, D_MODEL)
    new_states = tuple(jnp.stack([s[i] for s in ctx_states], axis=1) for i in range(6))

    bd, td, _ = x_sample.shape
    x = _add_position(x_sample, _position_code(td // GRID_W)).reshape(bd * td, D_MODEL)
    for l in range(DEPTH):
        cached = (state_mlstm_C[:, l].astype(F32), state_mlstm_n[:, l].astype(F32),
                  state_mlstm_m[:, l].astype(F32), state_hgrn_S[:, l].astype(F32),
                  state_gla_S[:, l].astype(F32), state_lru_h[:, l].astype(F32))
        mod_l = mod[l, 0:n_dec].reshape(n_dec, 1, 6 * D_MODEL)
        x, _ = _layer(x, bd, td, mod_l, layer_params(l), cached, False)
    y_sample = _final_norm(x, final_g).reshape(bd, td, D_MODEL)
    return (y_prompt, y_sample) + new_states
```

```python
import functools

import jax
import jax.numpy as jnp
from jax import lax
from jax.experimental import pallas as pl
from jax.experimental.pallas import tpu as pltpu

F32 = jnp.float32
BF16 = jnp.bfloat16
HIGHEST = lax.Precision.HIGHEST

D_MODEL = 1024
DEPTH = 4
GRID_W = 64
EPS = 1e-6
NEG_BIG = -1e30
MAX_EXP_ARG = 80.0
MIX_W = 512
N_HEAD = 4
DH = 128
DK_C = 64
R_C = 16
GLA_TAU = 16.0
CONV_W = 4
LRU_C = 8.0
N_BRANCH = 4
N_KEYS = 128
N_EXPERTS = N_KEYS * N_KEYS
PEER_HEADS = 8
PEER_TOPK = 16
POS_BASE = 10000.0

LANE = 128
VMEM_LIMIT = 56 * 1024 * 1024

COL_AQ, COL_AK, COL_AV, COL_AO = 0, 512, 1024, 1536
COL_BQ, COL_BF, COL_BI, COL_BG = 2048, 2560, 3584, 4096
COL_CQ, COL_CK, COL_CV, COL_CG = 4608, 4864, 5120, 5632
COL_DX, COL_DG = 6144, 6656
COL_GATES = 7168
COL_SMALL = 11264
SMALL_W = 256
N_PACK = COL_SMALL + SMALL_W

_ORIG_AI = 2048
_ORIG_BQ = 2064
_ORIG_CLR = 6160
_ORIG_DX = 6192
_ORIG_END = 11312

CHUNK_A = 128
CHUNK_G = 64
SUB_G = 16
CHUNK_D = 128


def _cparams(sem):
    return pltpu.CompilerParams(dimension_semantics=sem, vmem_limit_bytes=VMEM_LIMIT)


def _bdot(a, b):
    return jnp.dot(a.astype(BF16), b.astype(BF16), preferred_element_type=F32)


def _bdot_nt(a, b):
    return lax.dot_general(a.astype(BF16), b.astype(BF16), (((1,), (1,)), ((), ())),
                           preferred_element_type=F32)


def _bdot_tn(a, b):
    return lax.dot_general(a.astype(BF16), b.astype(BF16), (((0,), (0,)), ((), ())),
                           preferred_element_type=F32)


def _split(a):
    hi = a.astype(BF16)
    lo = (a - hi.astype(F32)).astype(BF16)
    return hi, lo


def _dot3(a, b_hi, b_lo):
    a_hi, a_lo = _split(a)
    return (jnp.dot(a_hi, b_hi, preferred_element_type=F32)
            + jnp.dot(a_lo, b_hi, preferred_element_type=F32)
            + jnp.dot(a_hi, b_lo, preferred_element_type=F32))


def _log_sigmoid(z):
    return jnp.minimum(z, 0.0) - jnp.log1p(jnp.exp(-jnp.abs(z)))


def _sigmoid(z):
    return 1.0 / (1.0 + jnp.exp(-z))


def _gelu(x):
    return 0.5 * x * (1.0 + jnp.tanh(0.7978845608028654 * (x + 0.044715 * x * x * x)))


def _silu(x):
    return x * _sigmoid(x)


def _expm1(y):
    u = jnp.exp(y)
    near = (u - 1.0) * y / jnp.log(u)
    return jnp.where(u == 1.0, y, jnp.where(u < 0.5, u - 1.0, near))


def _mod_kernel(c_ref, w_ref, b_ref, o_ref):
    a = _silu(c_ref[...])
    o_ref[0] = jnp.dot(a, w_ref[0], precision=HIGHEST, preferred_element_type=F32) + b_ref[0]


def _modulation(cond, w_mod, b_mod):
    r = cond.shape[0]
    tc = 1536
    return pl.pallas_call(
        _mod_kernel,
        grid=(DEPTH, 6 * D_MODEL // tc),
        in_specs=[pl.BlockSpec((r, D_MODEL), lambda l, j: (0, 0)),
                  pl.BlockSpec((1, D_MODEL, tc), lambda l, j: (l, 0, j)),
                  pl.BlockSpec((1, 1, tc), lambda l, j: (l, 0, j))],
        out_specs=pl.BlockSpec((1, r, tc), lambda l, j: (l, 0, j)),
        out_shape=jax.ShapeDtypeStruct((DEPTH, r, 6 * D_MODEL), F32),
        compiler_params=_cparams(("parallel", "parallel")),
        name="modulation",
    )(cond, w_mod, b_mod.reshape(DEPTH, 1, 6 * D_MODEL))


def _addpos_kernel(x_ref, p_ref, o_ref):
    o_ref[0] = x_ref[0] + p_ref[...]


def _add_position(x, pos):
    b, t, d = x.shape
    tt = 512
    return pl.pallas_call(
        _addpos_kernel,
        grid=(b, t // tt),
        in_specs=[pl.BlockSpec((1, tt, d), lambda i, j: (i, j, 0)),
                  pl.BlockSpec((tt, d), lambda i, j: (j, 0))],
        out_specs=pl.BlockSpec((1, tt, d), lambda i, j: (i, j, 0)),
        out_shape=jax.ShapeDtypeStruct(x.shape, F32),
        compiler_params=_cparams(("parallel", "parallel")),
        name="add_position",
    )(x, pos)


def _inproj_kernel(x_ref, mod_ref, g_ref, w_ref, b_ref, o_ref, h_ref):
    @pl.when(pl.program_id(1) == 0)
    def _():
        x = x_ref[...]
        inv = lax.rsqrt(jnp.mean(x * x, axis=-1, keepdims=True) + EPS)
        sh = mod_ref[0, :, 0:D_MODEL]
        sc = mod_ref[0, :, D_MODEL:2 * D_MODEL]
        h_ref[...] = (x * inv * g_ref[...] * (1.0 + sc) + sh).astype(BF16)

    o_ref[...] = jnp.dot(h_ref[...], w_ref[...], preferred_element_type=F32) + b_ref[...]


def _inproj(x, mod_l, norm_g, w_pack, b_pack, seq_len):
    n = x.shape[0]
    tn, tc = 512, 1152
    rows = mod_l.shape[0]
    if rows == 1:
        mod_map = lambda i, j: (0, 0, 0)
    else:
        mod_map = lambda i, j: ((i * tn) // seq_len, 0, 0)
    return pl.pallas_call(
        _inproj_kernel,
        grid=(n // tn, N_PACK // tc),
        in_specs=[pl.BlockSpec((tn, D_MODEL), lambda i, j: (i, 0)),
                  pl.BlockSpec((1, 1, 6 * D_MODEL), mod_map),
                  pl.BlockSpec((1, D_MODEL), lambda i, j: (0, 0)),
                  pl.BlockSpec((D_MODEL, tc), lambda i, j: (0, j)),
                  pl.BlockSpec((1, tc), lambda i, j: (0, j))],
        out_specs=pl.BlockSpec((tn, tc), lambda i, j: (i, j)),
        out_shape=jax.ShapeDtypeStruct((n, N_PACK), F32),
        scratch_shapes=[pltpu.VMEM((tn, D_MODEL), BF16)],
        compiler_params=_cparams(("parallel", "arbitrary")),
        name="in_projection",
    )(x, mod_l, norm_g, w_pack, b_pack)


def _causal_mask(length, rev):
    row = lax.broadcasted_iota(jnp.int32, (length, length), 0)
    col = lax.broadcasted_iota(jnp.int32, (length, length), 1)
    return (col >= row) if rev else (col <= row)


def _mlstm_kernel(*refs, has_init, emit_state):
    L = CHUNK_A
    ins = list(refs)
    dirs = [ins[0:4], ins[4:8]]
    pos = 8
    if has_init:
        c0_ref, n0_ref, m0_ref = ins[pos:pos + 3]
        pos += 3
    outs = ins[pos:pos + 2]
    pos += 2
    if emit_state:
        cout_ref, nout_ref, mout_ref = ins[pos:pos + 3]
        pos += 3
    cs_ref, ns_ref, ms_ref = ins[pos:pos + 3]
    c = pl.program_id(1)

    @pl.when(c == 0)
    def _():
        if has_init:
            cs_ref[...] = c0_ref[0]
            ns_ref[...] = n0_ref[0]
            ms_ref[...] = m0_ref[0]
        else:
            cs_ref[...] = jnp.zeros_like(cs_ref)
            ns_ref[...] = jnp.zeros_like(ns_ref)
            ms_ref[...] = jnp.zeros_like(ms_ref)

    for d in range(2):
        rev = d == 1
        q_ref, k_ref, v_ref, s_ref = dirs[d]
        o_ref = outs[d]
        mask = _causal_mask(L, rev)
        tri = mask.astype(F32)
        sm = s_ref[:, 0:LANE]
        lf = _log_sigmoid(sm)
        bc = jnp.dot(tri, lf, precision=HIGHEST, preferred_element_type=F32)
        sm_t = sm.T
        bc_t = bc.T
        last = 0 if rev else L - 1
        for h in range(N_HEAD):
            ci, cf = d * N_HEAD + h, 2 * N_HEAD + d * N_HEAD + h
            li_c, b_c = sm[:, ci:ci + 1], bc[:, cf:cf + 1]
            li_r, b_r = sm_t[ci:ci + 1, :], bc_t[cf:cf + 1, :]
            m_prev = ms_ref[d, h][:, 0:1]
            logw = jnp.where(mask, b_c - b_r + li_r, NEG_BIG)
            from_state = b_c + m_prev
            m_t = jnp.maximum(from_state, jnp.max(logw, axis=-1, keepdims=True))
            w_state = jnp.exp(from_state - m_t)
            hs = slice(h * DH, (h + 1) * DH)
            q = q_ref[:, hs]
            k = k_ref[:, hs] * (DH ** -0.5)
            v = v_ref[:, hs]
            scores = _bdot_nt(q, k) * jnp.exp(logw - m_t)
            c_st = cs_ref[d, h]
            n_st = ns_ref[d, h]
            num = w_state * _bdot(q, c_st) + _bdot(scores, v)
            den = (w_state * jnp.sum(q * n_st, axis=-1, keepdims=True)
                   + jnp.sum(scores, axis=-1, keepdims=True))
            floor = jnp.exp(jnp.minimum(-m_t, MAX_EXP_ARG))
            o_ref[:, hs] = num / jnp.maximum(jnp.abs(den), floor)
            m_new = m_t[last:last + 1, :]
            b_last = b_c[last:last + 1, :]
            kw = k * jnp.exp(b_last - b_c + li_c - m_new)
            decay = jnp.exp(b_last + m_prev - m_new)
            cs_ref[d, h] = decay * c_st + _bdot_tn(kw, v)
            ns_ref[d, h] = decay * n_st + jnp.sum(kw, axis=0, keepdims=True)
            ms_ref[d, h] = jnp.broadcast_to(m_new, (1, LANE))

    if emit_state:
        @pl.when(c == pl.num_programs(1) - 1)
        def _():
            cout_ref[0] = cs_ref[...]
            nout_ref[0] = ns_ref[...]
            mout_ref[0] = ms_ref[...]


def _mlstm(p, bsz, seq, state, emit_state):
    L = CHUNK_A
    nc = seq // L
    n = bsz * seq
    has_init = state is not None

    def fwd(cb):
        return lambda b, c: (b * nc + c, cb)

    def bwd(cb):
        return lambda b, c: (b * nc + (nc - 1 - c), cb)

    in_specs, args = [], []
    for mk in (fwd, bwd):
        for col in (COL_AQ, COL_AK, COL_AV):
            in_specs.append(pl.BlockSpec((L, MIX_W), mk(col // MIX_W)))
            args.append(p)
        in_specs.append(pl.BlockSpec((L, SMALL_W), mk(COL_SMALL // SMALL_W)))
        args.append(p)
    st_specs = [pl.BlockSpec((1, 2, N_HEAD, DH, DH), lambda b, c: (b, 0, 0, 0, 0)),
                pl.BlockSpec((1, 2, N_HEAD, 1, DH), lambda b, c: (b, 0, 0, 0, 0)),
                pl.BlockSpec((1, 2, N_HEAD, 1, LANE), lambda b, c: (b, 0, 0, 0, 0))]
    st_shapes = [jax.ShapeDtypeStruct((bsz, 2, N_HEAD, DH, DH), F32),
                 jax.ShapeDtypeStruct((bsz, 2, N_HEAD, 1, DH), F32),
                 jax.ShapeDtypeStruct((bsz, 2, N_HEAD, 1, LANE), F32)]
    if has_init:
        c0, n0, m0 = state
        in_specs += st_specs
        args += [c0, n0.reshape(bsz, 2, N_HEAD, 1, DH),
                 jnp.broadcast_to(m0[..., None, None], (bsz, 2, N_HEAD, 1, LANE))]
    out_specs = [pl.BlockSpec((L, MIX_W), fwd(0)), pl.BlockSpec((L, MIX_W), bwd(0))]
    out_shape = [jax.ShapeDtypeStruct((n, MIX_W), F32)] * 2
    if emit_state:
        out_specs += st_specs
        out_shape += st_shapes
    res = pl.pallas_call(
        functools.partial(_mlstm_kernel, has_init=has_init, emit_state=emit_state),
        grid=(bsz, nc),
        in_specs=in_specs,
        out_specs=out_specs,
        out_shape=out_shape,
        scratch_shapes=[pltpu.VMEM((2, N_HEAD, DH, DH), F32),
                        pltpu.VMEM((2, N_HEAD, 1, DH), F32),
                        pltpu.VMEM((2, N_HEAD, 1, LANE), F32)],
        compiler_params=_cparams(("parallel", "arbitrary")),
        name="mlstm_scan",
    )(*args)
    o_f, o_b = res[0], res[1]
    new_state = None
    if emit_state:
        new_state = (res[2], res[3].reshape(bsz, 2, N_HEAD, DH), res[4][:, :, :, 0, 0])
    return o_f, o_b, new_state


def _gla_direction(q, k, la, v, s_ref, d, o_ref, *, rev, dk):
    L, S = CHUNK_G, SUB_G
    nsub = L // S
    mask = _causal_mask(L, rev)
    tri = mask.astype(F32)
    b = jnp.dot(tri, la, precision=HIGHEST, preferred_element_type=F32)
    bx = b - la
    ref_rows = [bx[i * S + (S - 1 if rev else 0):i * S + (S - 1 if rev else 0) + 1, :]
                for i in range(nsub)]
    bref = jnp.concatenate([jnp.broadcast_to(r, (S, r.shape[1])) for r in ref_rows], axis=0)
    q_hat = q * jnp.exp(b - bref)
    rowid = lax.broadcasted_iota(jnp.int32, (L, 1), 0)
    k_hat = []
    for i in range(nsub):
        seen = (rowid >= i * S) if rev else (rowid < (i + 1) * S)
        k_hat.append(k * jnp.exp(jnp.where(seen, ref_rows[i] - b, NEG_BIG)))
    last = 0 if rev else L - 1
    b_last = b[last:last + 1, :]
    q_state = q * jnp.exp(b)
    k_state = k * jnp.exp(b_last - b)
    e_last = jnp.exp(b_last)
    for h in range(N_HEAD):
        ks = slice(h * dk, (h + 1) * dk)
        vs = slice(h * DH, (h + 1) * DH)
        vh = v[:, vs]
        blocks = [_bdot_nt(q_hat[i * S:(i + 1) * S, ks], k_hat[i][:, ks]) for i in range(nsub)]
        scores = jnp.where(mask, jnp.concatenate(blocks, axis=0), 0.0)
        s_t = s_ref[d, h]
        o_ref[:, vs] = _bdot_nt(q_state[:, ks], s_t) + _bdot(scores, vh)
        s_ref[d, h] = s_t * e_last[:, ks] + _bdot_tn(vh, k_state[:, ks])


def _scan_state_io(ins, pos, has_init, emit_state, n_out):
    s0_ref = sout_ref = None
    if has_init:
        s0_ref = ins[pos]
        pos += 1
    outs = ins[pos:pos + n_out]
    pos += n_out
    if emit_state:
        sout_ref = ins[pos]
        pos += 1
    return s0_ref, outs, sout_ref, ins[pos]


def _scan_state_init(s_ref, s0_ref):
    @pl.when(pl.program_id(1) == 0)
    def _():
        if s0_ref is not None:
            s_ref[...] = s0_ref[0]
        else:
            s_ref[...] = jnp.zeros_like(s_ref)


def _scan_state_emit(s_ref, sout_ref):
    if sout_ref is not None:
        @pl.when(pl.program_id(1) == pl.num_programs(1) - 1)
        def _():
            sout_ref[0] = s_ref[...]


def _hgrn_kernel(*refs, has_init, emit_state):
    ins = list(refs)
    lb_ref = ins[6]
    s0_ref, outs, sout_ref, s_ref = _scan_state_io(ins, 7, has_init, emit_state, 2)
    _scan_state_init(s_ref, s0_ref)
    for d in range(2):
        q_ref, z_ref, v_ref = ins[3 * d:3 * d + 3]
        z = z_ref[...]
        lb = lb_ref[d:d + 1, :]
        la = _log_sigmoid(z) + jnp.log1p(lb * jnp.exp(jnp.minimum(-z, MAX_EXP_ARG)))
        k = (1.0 - lb) * _sigmoid(-z)
        q = q_ref[...] * (DH ** -0.5)
        _gla_direction(q, k, la, v_ref[...], s_ref, d, outs[d], rev=d == 1, dk=DH)
    _scan_state_emit(s_ref, sout_ref)


def _gla_kernel(*refs, has_init, emit_state):
    ins = list(refs)
    up_ref, upb_ref = ins[8], ins[9]
    s0_ref, outs, sout_ref, s_ref = _scan_state_io(ins, 10, has_init, emit_state, 2)
    _scan_state_init(s_ref, s0_ref)
    for d in range(2):
        q_ref, k_ref, v_ref, sm_ref = ins[4 * d:4 * d + 4]
        zg = jnp.dot(sm_ref[...], up_ref[d], precision=HIGHEST,
                     preferred_element_type=F32) + upb_ref[d:d + 1, :]
        la = _log_sigmoid(zg) * (1.0 / GLA_TAU)
        q = q_ref[...] * (DK_C ** -0.5)
        _gla_direction(q, k_ref[...], la, v_ref[...], s_ref, d, outs[d], rev=d == 1, dk=DK_C)
    _scan_state_emit(s_ref, sout_ref)


def _gated_scan(kernel, p, bsz, seq, cols, extra, extra_specs, dk, state_t, emit_state, name):
    L = CHUNK_G
    nc = seq // L
    n = bsz * seq
    has_init = state_t is not None

    def fwd(cb):
        return lambda b, c: (b * nc + c, cb)

    def bwd(cb):
        return lambda b, c: (b * nc + (nc - 1 - c), cb)

    in_specs, args = [], []
    for d, mk in enumerate((fwd, bwd)):
        for off, width in cols[d]:
            in_specs.append(pl.BlockSpec((L, width), mk(off // width)))
            args.append(p)
    in_specs += extra_specs
    args += extra
    st_spec = pl.BlockSpec((1, 2, N_HEAD, DH, dk), lambda b, c: (b, 0, 0, 0, 0))
    if has_init:
        in_specs.append(st_spec)
        args.append(state_t)
    out_specs = [pl.BlockSpec((L, MIX_W), fwd(0)), pl.BlockSpec((L, MIX_W), bwd(0))]
    out_shape = [jax.ShapeDtypeStruct((n, MIX_W), F32)] * 2
    if emit_state:
        out_specs.append(st_spec)
        out_shape.append(jax.ShapeDtypeStruct((bsz, 2, N_HEAD, DH, dk), F32))
    res = pl.pallas_call(
        functools.partial(kernel, has_init=has_init, emit_state=emit_state),
        grid=(bsz, nc),
        in_specs=in_specs,
        out_specs=out_specs,
        out_shape=out_shape,
        scratch_shapes=[pltpu.VMEM((2, N_HEAD, DH, dk), F32)],
        compiler_params=_cparams(("parallel", "arbitrary")),
        name=name,
    )(*args)
    return res[0], res[1], (res[2] if emit_state else None)


def _hgrn(p, bsz, seq, lb, state_t, emit_state):
    cols = [[(COL_BQ, MIX_W), (COL_BF + d * MIX_W, MIX_W), (COL_BI, MIX_W)] for d in range(2)]
    return _gated_scan(_hgrn_kernel, p, bsz, seq, cols, [lb],
                       [pl.BlockSpec((2, MIX_W), lambda b, c: (0, 0))], DH, state_t, emit_state,
                       "hgrn2_scan")


def _gla(p, bsz, seq, up_pad, up_b, state_t, emit_state):
    kw = N_HEAD * DK_C
    cols = [[(COL_CQ, kw), (COL_CK, kw), (COL_CV, MIX_W), (COL_SMALL, SMALL_W)] for _ in range(2)]
    return _gated_scan(_gla_kernel, p, bsz, seq, cols, [up_pad, up_b],
                       [pl.BlockSpec((2, SMALL_W, kw), lambda b, c: (0, 0, 0)),
                        pl.BlockSpec((2, kw), lambda b, c: (0, 0))], DK_C, state_t, emit_state,
                       "gla_scan")


def _shift_rows(x, k, rev, fill):
    n = x.shape[0]
    rowid = lax.broadcasted_iota(jnp.int32, x.shape, 0)
    if rev:
        return jnp.where(rowid >= n - k, fill, pltpu.roll(x, n - k, 0))
    return jnp.where(rowid < k, fill, pltpu.roll(x, k, 0))


def _lru_kernel(*refs, seq, has_init, emit_state):
    L = CHUNK_D
    nc = seq // L
    ins = list(refs)
    (dx_ref, dg_ref, cw_ref, cb_ref, wah_ref, wal_ref, ba_ref, wxh_ref, wxl_ref, bx_ref,
     lam_ref) = ins[0:11]
    pos = 11
    h0_ref = hout_ref = None
    if has_init:
        h0_ref = ins[pos]
        pos += 1
    y_ref = ins[pos]
    pos += 1
    if emit_state:
        hout_ref = ins[pos]
        pos += 1
    pad_ref = ins[pos]

    zeros8 = jnp.zeros((8, LANE), F32)
    pad_ref[0:8, :] = zeros8
    pad_ref[8:8 + seq, :] = dx_ref[...]
    pad_ref[8 + seq:16 + seq, :] = zeros8
    lam = lam_ref[...]
    sp = jnp.maximum(-lam, 0.0) + jnp.log1p(jnp.exp(-jnp.abs(lam)))
    cw = cw_ref[...]
    cb = cb_ref[...]

    def chunk_scan(c, carry, d):
        rev = d == 1
        start = pl.multiple_of(c * L, L)
        win = pad_ref[pl.ds(start, L + 16), :]
        xd = cb + sum(cw[j:j + 1, :] * win[7 + j:7 + j + L, :] for j in range(CONV_W))
        r = _sigmoid(_dot3(xd, wah_ref[d, 0], wal_ref[d, 0]) + ba_ref[d:d + 1, :])
        ig = _sigmoid(_dot3(xd, wxh_ref[d, 0], wxl_ref[d, 0]) + bx_ref[d:d + 1, :])
        log_a = -LRU_C * r * sp[d:d + 1, :]
        a = jnp.exp(log_a)
        u = jnp.sqrt(jnp.maximum(-_expm1(2.0 * log_a), 0.0)) * (ig * xd)
        k = 1
        while k < L:
            u = a * _shift_rows(u, k, rev, 0.0) + u
            a = a * _shift_rows(a, k, rev, 1.0)
            k *= 2
        h = a * carry + u
        last = 0 if rev else L - 1
        return start, h, h[last:last + 1, :]

    def fwd_body(c, carry):
        start, h, new = chunk_scan(c, carry, 0)
        y_ref[pl.ds(start, L), :] = h
        return new

    def bwd_body(i, carry):
        start, h, new = chunk_scan(nc - 1 - i, carry, 1)
        y_ref[pl.ds(start, L), :] = (y_ref[pl.ds(start, L), :] + h) * _gelu(dg_ref[pl.ds(start, L), :])
        return new

    if has_init:
        init_f, init_b = h0_ref[0, 0:1, :], h0_ref[0, 1:2, :]
    else:
        init_f = init_b = jnp.zeros((1, LANE), F32)
    fin_f = lax.fori_loop(0, nc, fwd_body, init_f)
    fin_b = lax.fori_loop(0, nc, bwd_body, init_b)
    if emit_state:
        hout_ref[0, 0:1, :] = fin_f
        hout_ref[0, 1:2, :] = fin_b


def _lru(p, bsz, seq, conv_w, conv_b, wa, ba, wx, bx, lam, h0, emit_state):
    n = bsz * seq
    has_init = h0 is not None
    wah, wal = _split(wa)
    wxh, wxl = _split(wx)
    gate_w = pl.BlockSpec((2, 1, DH, DH), lambda b, h: (0, h, 0, 0))
    vec2 = pl.BlockSpec((2, LANE), lambda b, h: (0, h))
    in_specs = [pl.BlockSpec((seq, LANE), lambda b, h: (b, COL_DX // LANE + h)),
                pl.BlockSpec((seq, LANE), lambda b, h: (b, COL_DG // LANE + h)),
                pl.BlockSpec((CONV_W, LANE), lambda b, h: (0, h)),
                pl.BlockSpec((1, LANE), lambda b, h: (0, h)),
                gate_w, gate_w, vec2, gate_w, gate_w, vec2, vec2]
    args = [p, p, conv_w, conv_b.reshape(1, MIX_W), wah, wal, ba, wxh, wxl, bx, lam]
    st_spec = pl.BlockSpec((1, 2, LANE), lambda b, h: (b, 0, h))
    if has_init:
        in_specs.append(st_spec)
        args.append(h0)
    out_specs = [pl.BlockSpec((seq, LANE), lambda b, h: (b, h))]
    out_shape = [jax.ShapeDtypeStruct((n, MIX_W), F32)]
    if emit_state:
        out_specs.append(st_spec)
        out_shape.append(jax.ShapeDtypeStruct((bsz, 2, MIX_W), F32))
    res = pl.pallas_call(
        functools.partial(_lru_kernel, seq=seq, has_init=has_init, emit_state=emit_state),
        grid=(bsz, N_HEAD),
        in_specs=in_specs,
        out_specs=out_specs,
        out_shape=out_shape,
        scratch_shapes=[pltpu.VMEM((seq + 16, LANE), F32)],
        compiler_params=_cparams(("parallel", "parallel")),
        name="conv_rglru",
    )(*args)
    return res[0], (res[1] if emit_state else None)


def _head_rms(x):
    parts = []
    for h in range(N_HEAD):
        xh = x[:, h * DH:(h + 1) * DH]
        parts.append(xh * lax.rsqrt(jnp.mean(xh * xh, axis=-1, keepdims=True) + EPS))
    return jnp.concatenate(parts, axis=-1)


def _merge_kernel(af_ref, ab_ref, bf_ref, bb_ref, cf_ref, cb_ref, yd_ref, ao_ref, bg_ref, cg_ref,
                  gt0_ref, gt1_ref, gt2_ref, gt3_ref, x_ref, mod_ref, g2_ref, wbr_ref, wout_ref,
                  xo_ref, h2_ref, h2b_ref):
    ya = _sigmoid(ao_ref[...]) * _head_rms(af_ref[...] + ab_ref[...])
    yb = _silu(bg_ref[...]) * _head_rms(bf_ref[...] + bb_ref[...])
    yc = _silu(cg_ref[...]) * _head_rms(cf_ref[...] + cb_ref[...])
    merged = None
    gate_refs = (gt0_ref, gt1_ref, gt2_ref, gt3_ref)
    for i, y in enumerate((ya, yb, yc, yd_ref[...])):
        proj = jnp.dot(y.astype(BF16), wbr_ref[i], preferred_element_type=F32)
        term = _sigmoid(gate_refs[i][...]) * proj
        merged = term if merged is None else merged + term
    out = jnp.dot(merged.astype(BF16), wout_ref[...], preferred_element_type=F32)
    g1 = mod_ref[0, :, 2 * D_MODEL:3 * D_MODEL]
    sh2 = mod_ref[0, :, 3 * D_MODEL:4 * D_MODEL]
    sc2 = mod_ref[0, :, 4 * D_MODEL:5 * D_MODEL]
    x = x_ref[...] + g1 * out
    xo_ref[...] = x
    inv = lax.rsqrt(jnp.mean(x * x, axis=-1, keepdims=True) + EPS)
    h2 = x * inv * g2_ref[...] * (1.0 + sc2) + sh2
    h2_ref[...] = h2
    h2b_ref[...] = h2.astype(BF16)


def _merge(mix_outs, yd, p, x, mod_l, norm2_g, w_branch, w_out, seq_len):
    n = x.shape[0]
    tn = 256
    rows = mod_l.shape[0]
    mod_map = (lambda i: (0, 0, 0)) if rows == 1 else (lambda i: ((i * tn) // seq_len, 0, 0))
    tok = lambda cb: (lambda i: (i, cb))
    in_specs = [pl.BlockSpec((tn, MIX_W), tok(0))] * 7
    in_specs += [pl.BlockSpec((tn, MIX_W), tok(COL_AO // MIX_W)),
                 pl.BlockSpec((tn, MIX_W), tok(COL_BG // MIX_W)),
                 pl.BlockSpec((tn, MIX_W), tok(COL_CG // MIX_W)),
                 *[pl.BlockSpec((tn, D_MODEL), tok(COL_GATES // D_MODEL + i)) for i in range(N_BRANCH)],
                 pl.BlockSpec((tn, D_MODEL), tok(0)),
                 pl.BlockSpec((1, 1, 6 * D_MODEL), mod_map),
                 pl.BlockSpec((1, D_MODEL), lambda i: (0, 0)),
                 pl.BlockSpec((N_BRANCH, MIX_W, D_MODEL), lambda i: (0, 0, 0)),
                 pl.BlockSpec((D_MODEL, D_MODEL), lambda i: (0, 0))]
    out_spec = pl.BlockSpec((tn, D_MODEL), tok(0))
    return pl.pallas_call(
        _merge_kernel,
        grid=(n // tn,),
        in_specs=in_specs,
        out_specs=[out_spec, out_spec, out_spec],
        out_shape=[jax.ShapeDtypeStruct((n, D_MODEL), F32), jax.ShapeDtypeStruct((n, D_MODEL), F32),
                   jax.ShapeDtypeStruct((n, D_MODEL), BF16)],
        compiler_params=_cparams(("parallel",)),
        name="branch_merge",
    )(*mix_outs, yd, p, p, p, p, p, p, p, x, mod_l, norm2_g, w_branch, w_out)


ROUTE_TN = 256
ROUTE_LG = LANE


def _top16_exact(vals):
    n_rows = vals.shape[0]
    rowid = lax.broadcasted_iota(jnp.int32, vals.shape, 0).astype(F32)
    rank = jnp.full(vals.shape, float(PEER_TOPK), F32)
    tops = []
    for r in range(PEER_TOPK):
        m = jnp.max(vals, axis=0, keepdims=True)
        idx = jnp.min(jnp.where(vals == m, rowid, float(n_rows)), axis=0, keepdims=True)
        sel = rowid == idx
        rank = jnp.where(sel, float(r), rank)
        vals = jnp.where(sel, -jnp.inf, vals)
        tops.append(m)
    return jnp.concatenate(tops, axis=0), rank


def _top16_rows(vals, top_ref, rank_ref):
    work = vals
    rank = jnp.full(vals.shape, float(PEER_TOPK), F32)
    tops = []
    for r in range(PEER_TOPK):
        m = jnp.max(work, axis=0, keepdims=True)
        eq = work == m
        rank = jnp.where(eq, float(r), rank)
        work = jnp.where(eq, -jnp.inf, work)
        tops.append(m)
    top_ref[...] = jnp.concatenate(tops, axis=0)
    rank_ref[...] = rank
    n_sel = jnp.sum(jnp.where(rank < float(PEER_TOPK), 1.0, 0.0), axis=0, keepdims=True)
    tied = jnp.max(jnp.abs(n_sel - float(PEER_TOPK))) > 0.0

    @pl.when(tied)
    def _():
        top, rk = _top16_exact(vals)
        top_ref[...] = top
        rank_ref[...] = rk

    return top_ref[...], rank_ref[...]


_CAND_PIECES = (('a', 0, 0, 16), ('a', 0, 8, 16), ('a', 1, 0, 8), ('a', 2, 0, 5), ('a', 3, 0, 4),
                ('b', 0, 0, (4, 8)), ('b', 0, 8, (8, 16)), ('b', 1, 0, (4, 8)), ('b', 2, 0, (4, 5)))


def _pair_pieces(x1, x2, combine):
    out = []
    for kind, fixed, off, _ in _CAND_PIECES:
        if kind == 'a':
            out.append(combine(x1[fixed:fixed + 1, :], x2[off:off + 8, :]))
        else:
            out.append(combine(x1[off:off + 8, :], x2[fixed:fixed + 1, :]))
    return out


def _choose_pairs(s1, s2):
    K = PEER_TOPK
    n = s1.shape[1]
    iota8 = lax.broadcasted_iota(jnp.int32, (8, n), 0)
    sums = _pair_pieces(s1, s2, lambda x, y: x + y)
    vals, poss = [], []
    for (kind, fixed, off, lim), sm in zip(_CAND_PIECES, sums):
        idx = iota8 + off
        if kind == 'a':
            valid = idx < lim
            pos = fixed * K + idx
        else:
            valid = (idx >= lim[0]) & (idx < lim[1])
            pos = idx * K + fixed
        vals.append(jnp.where(valid, sm, -jnp.inf))
        poss.append(jnp.where(valid, pos, K * K).astype(F32))
    vals = jnp.concatenate(vals, axis=0)
    posid = jnp.concatenate(poss, axis=0)
    chosen = jnp.zeros(vals.shape, F32)
    for _ in range(K):
        m = jnp.max(vals, axis=0, keepdims=True)
        idx = jnp.min(jnp.where(vals == m, posid, float(K * K)), axis=0, keepdims=True)
        sel = posid == idx
        chosen = jnp.where(sel, 1.0, chosen)
        vals = jnp.where(sel, -jnp.inf, vals)
    return [chosen[8 * i:8 * i + 8, :] for i in range(len(_CAND_PIECES))]


def _route_kernel(h_ref, wqh_ref, wql_ref, kh_ref, kl_ref, a_ref, b_ref, r2_ref, c_ref,
                  qh_ref, ql_ref, s_ref, top_ref, rank_ref):
    K = PEER_TOPK
    q = _dot3(h_ref[...], wqh_ref[...], wql_ref[...])
    q_hi, q_lo = _split(q)
    for i in range(2 * PEER_HEADS):
        qh_ref[i] = q_hi[:, i * N_KEYS:(i + 1) * N_KEYS]
        ql_ref[i] = q_lo[:, i * N_KEYS:(i + 1) * N_KEYS]
    nt = (((1,), (1,)), ((), ()))

    def head_body(h, carry):
        for half in range(2):
            kh, kl = kh_ref[h, half], kl_ref[h, half]
            qh, ql = qh_ref[2 * h + half], ql_ref[2 * h + half]
            s_ref[half] = (lax.dot_general(kh, qh, nt, preferred_element_type=F32)
                           + lax.dot_general(kl, qh, nt, preferred_element_type=F32)
                           + lax.dot_general(kh, ql, nt, preferred_element_type=F32))
        for lg in range(ROUTE_TN // ROUTE_LG):
            ls = slice(lg * ROUTE_LG, (lg + 1) * ROUTE_LG)
            st1, st2 = s_ref[0, :, ls], s_ref[1, :, ls]
            s1, rank1 = _top16_rows(st1, top_ref, rank_ref)
            s2, rank2 = _top16_rows(st2, top_ref, rank_ref)
            ch = _choose_pairs(s1, s2)
            e1 = jnp.exp(s1 - s1[0:1, :])
            e2 = jnp.exp(s2 - s2[0:1, :])
            pair = _pair_pieces(e1, e2, lambda x, y: x * y)
            z = sum(jnp.sum(c * p, axis=0, keepdims=True) for c, p in zip(ch, pair))
            iota8 = lax.broadcasted_iota(jnp.int32, (8, ROUTE_LG), 0)
            low = ch[5] + ch[7] + ch[8]
            for a, cnt in ((3, ch[4]), (2, ch[3]), (1, ch[2]), (0, ch[0] + ch[1])):
                low = jnp.where(iota8 == a, jnp.sum(cnt, axis=0, keepdims=True), low)
            counts = jnp.concatenate([low, ch[6]], axis=0)
            c_dense = jnp.zeros((N_KEYS, ROUTE_LG), F32)
            for a in range(K):
                c_dense = jnp.where(rank1 == float(a), counts[a:a + 1, :], c_dense)
            a_ref[h, :, ls] = jnp.where(rank1 < float(K), jnp.exp(st1 - s1[0:1, :]), 0.0) / z
            b_ref[h, :, ls] = jnp.where(rank2 < float(K), jnp.exp(st2 - s2[0:1, :]), 0.0).astype(BF16)
            r2_ref[h, :, ls] = rank2.astype(BF16)
            c_ref[h, :, ls] = c_dense
        return carry

    lax.fori_loop(0, PEER_HEADS, head_body, 0)


def _route(h2, wq_hi, wq_lo, keys_hi, keys_lo):
    n = h2.shape[0]
    tn = ROUTE_TN
    dense = pl.BlockSpec((PEER_HEADS, N_KEYS, tn), lambda i: (0, 0, i))
    wspec = pl.BlockSpec((D_MODEL, PEER_HEADS * 2 * N_KEYS), lambda i: (0, 0))
    kspec = pl.BlockSpec((PEER_HEADS, 2, N_KEYS, N_KEYS), lambda i: (0, 0, 0, 0))
    return pl.pallas_call(
        _route_kernel,
        grid=(n // tn,),
        in_specs=[pl.BlockSpec((tn, D_MODEL), lambda i: (i, 0)), wspec, wspec, kspec, kspec],
        out_specs=[dense] * 4,
        out_shape=[jax.ShapeDtypeStruct((PEER_HEADS, N_KEYS, n), dt) for dt in (F32, BF16, BF16, F32)],
        scratch_shapes=[pltpu.VMEM((2 * PEER_HEADS, tn, N_KEYS), BF16),
                        pltpu.VMEM((2 * PEER_HEADS, tn, N_KEYS), BF16),
                        pltpu.VMEM((2, N_KEYS, tn), F32),
                        pltpu.VMEM((PEER_TOPK, ROUTE_LG), F32),
                        pltpu.VMEM((N_KEYS, ROUTE_LG), F32)],
        compiler_params=_cparams(("parallel",)),
        name="peer_route",
    )(h2, wq_hi, wq_lo, keys_hi, keys_lo)


PEER_TN = 512
PEER_TE = 1024


def _expert_kernel(hb_ref, u_ref, vt_ref, a_ref, b_ref, r2_ref, c_ref, x_ref, mod_ref, o_ref,
                   acc_ref):
    e = pl.program_id(1)

    @pl.when(e == 0)
    def _():
        acc_ref[...] = jnp.zeros_like(acc_ref)

    t_t = lax.dot_general(u_ref[...], hb_ref[...], (((1,), (1,)), ((), ())),
                          preferred_element_type=F32)
    act = _gelu(t_t)
    rows_per_tile = PEER_TE // N_KEYS
    w_parts = []
    for j in range(rows_per_tile):
        e1 = e * rows_per_tile + j
        g = None
        for h in range(PEER_HEADS):
            a_row = a_ref[h, pl.ds(e1, 1), :].astype(BF16)
            c_row = c_ref[h, pl.ds(e1, 1), :].astype(BF16)
            term = a_row * jnp.where(r2_ref[h] < c_row, b_ref[h], 0.0)
            g = term if g is None else g + term
        w_parts.append(g * act[j * N_KEYS:(j + 1) * N_KEYS, :].astype(BF16))
    w_t = jnp.concatenate(w_parts, axis=0)
    acc_ref[...] += jnp.dot(vt_ref[...], w_t, preferred_element_type=F32)

    @pl.when(e == pl.num_programs(1) - 1)
    def _():
        g2 = mod_ref[0, :, 5 * D_MODEL:6 * D_MODEL]
        o_ref[...] = x_ref[...] + g2 * acc_ref[...].T


def _experts(h2b, u_b, vt_b, dense, x, mod_l, seq_len):
    n = x.shape[0]
    tn, te = PEER_TN, PEER_TE
    rows = mod_l.shape[0]
    mod_map = (lambda i, e: (0, 0, 0)) if rows == 1 else (lambda i, e: ((i * tn) // seq_len, 0, 0))
    dspec = pl.BlockSpec((PEER_HEADS, N_KEYS, tn), lambda i, e: (0, 0, i))
    return pl.pallas_call(
        _expert_kernel,
        grid=(n // tn, N_EXPERTS // te),
        in_specs=[pl.BlockSpec((tn, D_MODEL), lambda i, e: (i, 0)),
                  pl.BlockSpec((te, D_MODEL), lambda i, e: (e, 0)),
                  pl.BlockSpec((D_MODEL, te), lambda i, e: (0, e)),
                  dspec, dspec, dspec, dspec,
                  pl.BlockSpec((tn, D_MODEL), lambda i, e: (i, 0)),
                  pl.BlockSpec((1, 1, 6 * D_MODEL), mod_map)],
        out_specs=pl.BlockSpec((tn, D_MODEL), lambda i, e: (i, 0)),
        out_shape=jax.ShapeDtypeStruct((n, D_MODEL), F32),
        scratch_shapes=[pltpu.VMEM((D_MODEL, tn), F32)],
        compiler_params=_cparams(("parallel", "arbitrary")),
        name="peer_experts",
    )(h2b, u_b, vt_b, *dense, x, mod_l)


def _final_norm_kernel(x_ref, g_ref, o_ref):
    x = x_ref[...]
    o_ref[...] = x * lax.rsqrt(jnp.mean(x * x, axis=-1, keepdims=True) + EPS) * g_ref[...]


def _final_norm(x, g):
    n = x.shape[0]
    tn = 512
    return pl.pallas_call(
        _final_norm_kernel,
        grid=(n // tn,),
        in_specs=[pl.BlockSpec((tn, D_MODEL), lambda i: (i, 0)),
                  pl.BlockSpec((1, D_MODEL), lambda i: (0, 0))],
        out_specs=pl.BlockSpec((tn, D_MODEL), lambda i: (i, 0)),
        out_shape=jax.ShapeDtypeStruct((n, D_MODEL), F32),
        compiler_params=_cparams(("parallel",)),
        name="final_norm",
    )(x, g)


def _pack_columns(a):
    small = jnp.concatenate([a[..., _ORIG_AI:_ORIG_BQ], a[..., _ORIG_CLR:_ORIG_DX]], axis=-1)
    pad = jnp.zeros(a.shape[:-1] + (SMALL_W - small.shape[-1],), a.dtype)
    return jnp.concatenate([a[..., 0:_ORIG_AI], a[..., _ORIG_BQ:_ORIG_CLR], a[..., _ORIG_DX:_ORIG_END],
                            small, pad], axis=-1)


def _position_code(rows):
    quarter = D_MODEL // 4
    omega = 1.0 / (POS_BASE ** (jnp.arange(quarter, dtype=F32) / quarter))
    r, col = jnp.meshgrid(jnp.arange(rows, dtype=F32), jnp.arange(GRID_W, dtype=F32), indexing='ij')

    def enc(pos):
        ang = pos.reshape(-1, 1) * omega
        return jnp.concatenate([jnp.sin(ang), jnp.cos(ang)], axis=-1)
    return jnp.concatenate([enc(r), enc(col)], axis=-1)


def _layer(x, bsz, seq, mod_l, lp, state, emit_state):
    p = _inproj(x, mod_l, lp['norm1_g'], lp['w_in'], lp['b_in'], seq)
    if state is None:
        st_a = st_b = st_c = st_d = None
    else:
        c0, n0, m0, sb0, sc0, hd0 = state
        st_a = (c0, n0, m0)
        st_b = jnp.swapaxes(sb0, -1, -2)
        st_c = jnp.swapaxes(sc0, -1, -2)
        st_d = hd0
    a_f, a_b, new_a = _mlstm(p, bsz, seq, st_a, emit_state)
    b_f, b_b, new_b = _hgrn(p, bsz, seq, lp['hgrn_lb'], st_b, emit_state)
    c_f, c_b, new_c = _gla(p, bsz, seq, lp['gla_up_pad'], lp['gla_up_b'], st_c, emit_state)
    yd, new_d = _lru(p, bsz, seq, lp['conv_w'], lp['conv_b'], lp['lru_w_a'], lp['lru_b_a'],
                     lp['lru_w_x'], lp['lru_b_x'], lp['lru_lambda'], st_d, emit_state)
    x1, h2, h2b = _merge((a_f, a_b, b_f, b_b, c_f, c_b), yd, p, x, mod_l, lp['norm2_g'],
                         lp['w_branch'], lp['w_out'], seq)
    dense = _route(h2, lp['wq_hi'], lp['wq_lo'], lp['keys_hi'], lp['keys_lo'])
    x2 = _experts(h2b, lp['peer_u'], lp['peer_vt'], dense, x1, mod_l, seq)
    new_state = None
    if emit_state:
        new_state = (*new_a, jnp.swapaxes(new_b, -1, -2), jnp.swapaxes(new_c, -1, -2), new_d)
    return x2, new_state


def kernel(x_prompt, x_sample, state_mlstm_C, state_mlstm_n, state_mlstm_m, state_hgrn_S,
           state_gla_S, state_lru_h, c, c_ctx, norm1_g, norm2_g, final_norm_g, w_mod, b_mod,
           w_in, b_in, w_gla_up, b_gla_up, hgrn_lower_bounds, conv_w, conv_b, lru_w_a, lru_b_a,
           lru_w_x, lru_b_x, lru_lambda, w_branch, w_out, peer_w_q, peer_sub_keys, peer_u, peer_v):
    lb_soft = jax.nn.softmax(hgrn_lower_bounds.astype(F32), axis=0)
    lb_all = jnp.cumsum(lb_soft, axis=0) - lb_soft[0:1]
    w_pack = _pack_columns(w_in).astype(BF16)
    b_pack = _pack_columns(b_in).reshape(DEPTH, 1, N_PACK)
    kw = N_HEAD * DK_C
    up_pad = jnp.zeros((DEPTH, 2, SMALL_W, kw), F32)
    for d in range(2):
        lo = 2 * N_HEAD * 2 + d * R_C
        up_pad = up_pad.at[:, d, lo:lo + R_C, :].set(w_gla_up[:, d].astype(F32))
    wq_hi, wq_lo = _split(peer_w_q)
    keys_hi, keys_lo = _split(peer_sub_keys)
    u_b = peer_u.astype(BF16)
    vt_b = jnp.swapaxes(peer_v, 1, 2).astype(BF16)
    wbr_b = w_branch.astype(BF16)
    wout_b = w_out.astype(BF16)

    def layer_params(l):
        return {
            'norm1_g': norm1_g[l].reshape(1, D_MODEL), 'norm2_g': norm2_g[l].reshape(1, D_MODEL),
            'w_in': w_pack[l], 'b_in': b_pack[l], 'hgrn_lb': lb_all[l],
            'gla_up_pad': up_pad[l], 'gla_up_b': b_gla_up[l],
            'conv_w': conv_w[l], 'conv_b': conv_b[l], 'lru_w_a': lru_w_a[l], 'lru_b_a': lru_b_a[l],
            'lru_w_x': lru_w_x[l], 'lru_b_x': lru_b_x[l], 'lru_lambda': lru_lambda[l],
            'w_branch': wbr_b[l], 'w_out': wout_b[l], 'wq_hi': wq_hi[l], 'wq_lo': wq_lo[l],
            'keys_hi': keys_hi[l], 'keys_lo': keys_lo[l], 'peer_u': u_b[l], 'peer_vt': vt_b[l],
        }

    cond = jnp.concatenate([c, c_ctx[None, :]], axis=0).astype(F32)
    n_dec = c.shape[0]
    mod = _modulation(cond, w_mod, b_mod)
    final_g = final_norm_g.reshape(1, D_MODEL)

    bp, tp, _ = x_prompt.shape
    x = x_prompt.reshape(bp * tp, D_MODEL)
    ctx_states = []
    for l in range(DEPTH):
        mod_l = mod[l, n_dec:n_dec + 1].reshape(1, 1, 6 * D_MODEL)
        x, st = _layer(x, bp, tp, mod_l, layer_params(l), None, True)
        ctx_states.append(st)
    y_prompt = _final_norm(x, final_g).reshape(bp, tp, D_MODEL)
    new_states = tuple(jnp.stack([s[i] for s in ctx_states], axis=1) for i in range(6))

    bd, td, _ = x_sample.shape
    x = _add_position(x_sample, _position_code(td // GRID_W)).reshape(bd * td, D_MODEL)
    for l in range(DEPTH):
        cached = (state_mlstm_C[:, l].astype(F32), state_mlstm_n[:, l].astype(F32),
                  state_mlstm_m[:, l].astype(F32), state_hgrn_S[:, l].astype(F32),
                  state_gla_S[:, l].astype(F32), state_lru_h[:, l].astype(F32))
        mod_l = mod[l, 0:n_dec].reshape(n_dec, 1, 6 * D_MODEL)
        x, _ = _layer(x, bd, td, mod_l, layer_params(l), cached, False)
    y_sample = _final_norm(x, final_g).reshape(bd, td, D_MODEL)
    return (y_prompt, y_sample) + new_states
```

```python
import functools

import jax
import jax.numpy as jnp
from jax import lax
from jax.experimental import pallas as pl
from jax.experimental.pallas import tpu as pltpu

F32 = jnp.float32
BF16 = jnp.bfloat16
HIGHEST = lax.Precision.HIGHEST

D_MODEL = 1024
DEPTH = 4
GRID_W = 64
EPS = 1e-6
NEG_BIG = -1e30
MAX_EXP_ARG = 80.0
MIX_W = 512
N_HEAD = 4
DH = 128
DK_C = 64
R_C = 16
GLA_TAU = 16.0
CONV_W = 4
LRU_C = 8.0
N_BRANCH = 4
N_KEYS = 128
N_EXPERTS = N_KEYS * N_KEYS
PEER_HEADS = 8
PEER_TOPK = 16
POS_BASE = 10000.0

LANE = 128
VMEM_LIMIT = 56 * 1024 * 1024

COL_AQ, COL_AK, COL_AV, COL_AO = 0, 512, 1024, 1536
COL_BQ, COL_BF, COL_BI, COL_BG = 2048, 2560, 3584, 4096
COL_CQ, COL_CK, COL_CV, COL_CG = 4608, 4864, 5120, 5632
COL_DX, COL_DG = 6144, 6656
COL_GATES = 7168
COL_SMALL = 11264
SMALL_W = 256
N_PACK = COL_SMALL + SMALL_W

_ORIG_AI = 2048
_ORIG_BQ = 2064
_ORIG_CLR = 6160
_ORIG_DX = 6192
_ORIG_END = 11312

CHUNK_A = 128
CHUNK_G = 64
SUB_G = 16
CHUNK_D = 128


def _cparams(sem):
    return pltpu.CompilerParams(dimension_semantics=sem, vmem_limit_bytes=VMEM_LIMIT)


def _bdot(a, b):
    return jnp.dot(a.astype(BF16), b.astype(BF16), preferred_element_type=F32)


def _bdot_nt(a, b):
    return lax.dot_general(a.astype(BF16), b.astype(BF16), (((1,), (1,)), ((), ())),
                           preferred_element_type=F32)


def _bdot_tn(a, b):
    return lax.dot_general(a.astype(BF16), b.astype(BF16), (((0,), (0,)), ((), ())),
                           preferred_element_type=F32)


def _split(a):
    hi = a.astype(BF16)
    lo = (a - hi.astype(F32)).astype(BF16)
    return hi, lo


def _dot3(a, b_hi, b_lo):
    a_hi, a_lo = _split(a)
    return (jnp.dot(a_hi, b_hi, preferred_element_type=F32)
            + jnp.dot(a_lo, b_hi, preferred_element_type=F32)
            + jnp.dot(a_hi, b_lo, preferred_element_type=F32))


def _log_sigmoid(z):
    return jnp.minimum(z, 0.0) - jnp.log1p(jnp.exp(-jnp.abs(z)))


def _sigmoid(z):
    return 1.0 / (1.0 + jnp.exp(-z))


def _gelu(x):
    return 0.5 * x * (1.0 + jnp.tanh(0.7978845608028654 * (x + 0.044715 * x * x * x)))


def _silu(x):
    return x * _sigmoid(x)


def _expm1(y):
    u = jnp.exp(y)
    near = (u - 1.0) * y / jnp.log(u)
    return jnp.where(u == 1.0, y, jnp.where(u < 0.5, u - 1.0, near))


def _mod_kernel(c_ref, w_ref, b_ref, o_ref):
    a = _silu(c_ref[...])
    o_ref[0] = jnp.dot(a, w_ref[0], precision=HIGHEST, preferred_element_type=F32) + b_ref[0]


def _modulation(cond, w_mod, b_mod):
    r = cond.shape[0]
    tc = 1536
    return pl.pallas_call(
        _mod_kernel,
        grid=(DEPTH, 6 * D_MODEL // tc),
        in_specs=[pl.BlockSpec((r, D_MODEL), lambda l, j: (0, 0)),
                  pl.BlockSpec((1, D_MODEL, tc), lambda l, j: (l, 0, j)),
                  pl.BlockSpec((1, 1, tc), lambda l, j: (l, 0, j))],
        out_specs=pl.BlockSpec((1, r, tc), lambda l, j: (l, 0, j)),
        out_shape=jax.ShapeDtypeStruct((DEPTH, r, 6 * D_MODEL), F32),
        compiler_params=_cparams(("parallel", "parallel")),
        name="modulation",
    )(cond, w_mod, b_mod.reshape(DEPTH, 1, 6 * D_MODEL))


def _addpos_kernel(x_ref, p_ref, o_ref):
    o_ref[0] = x_ref[0] + p_ref[...]


def _add_position(x, pos):
    b, t, d = x.shape
    tt = 512
    return pl.pallas_call(
        _addpos_kernel,
        grid=(b, t // tt),
        in_specs=[pl.BlockSpec((1, tt, d), lambda i, j: (i, j, 0)),
                  pl.BlockSpec((tt, d), lambda i, j: (j, 0))],
        out_specs=pl.BlockSpec((1, tt, d), lambda i, j: (i, j, 0)),
        out_shape=jax.ShapeDtypeStruct(x.shape, F32),
        compiler_params=_cparams(("parallel", "parallel")),
        name="add_position",
    )(x, pos)


def _inproj_kernel(x_ref, mod_ref, g_ref, w_ref, b_ref, o_ref, h_ref):
    @pl.when(pl.program_id(1) == 0)
    def _():
        x = x_ref[...]
        inv = lax.rsqrt(jnp.mean(x * x, axis=-1, keepdims=True) + EPS)
        sh = mod_ref[0, :, 0:D_MODEL]
        sc = mod_ref[0, :, D_MODEL:2 * D_MODEL]
        h_ref[...] = (x * inv * g_ref[...] * (1.0 + sc) + sh).astype(BF16)

    o_ref[...] = jnp.dot(h_ref[...], w_ref[...], preferred_element_type=F32) + b_ref[...]


def _inproj(x, mod_l, norm_g, w_pack, b_pack, seq_len, layer):
    n = x.shape[0]
    tn, tc = 512, 1152
    rows = mod_l.shape[0]
    if rows == 1:
        mod_map = lambda i, j: (0, 0, 0)
    else:
        mod_map = lambda i, j: ((i * tn) // seq_len, 0, 0)
    return pl.pallas_call(
        _inproj_kernel,
        grid=(n // tn, N_PACK // tc),
        in_specs=[pl.BlockSpec((tn, D_MODEL), lambda i, j: (i, 0)),
                  pl.BlockSpec((1, 1, 6 * D_MODEL), mod_map),
                  pl.BlockSpec((1, D_MODEL), lambda i, j: (0, 0)),
                  pl.BlockSpec((None, D_MODEL, tc), lambda i, j: (layer, 0, j)),
                  pl.BlockSpec((1, tc), lambda i, j: (0, j))],
        out_specs=pl.BlockSpec((tn, tc), lambda i, j: (i, j)),
        out_shape=jax.ShapeDtypeStruct((n, N_PACK), F32),
        scratch_shapes=[pltpu.VMEM((tn, D_MODEL), BF16)],
        compiler_params=_cparams(("parallel", "arbitrary")),
        name="in_projection",
    )(x, mod_l, norm_g, w_pack, b_pack)


SCAN_GROUP = 4


def _per_sequence(body, n_group, shared):
    def kern(*refs):
        def run(phase):
            for g in range(n_group):
                body(*[r if i in shared else r.at[g] for i, r in enumerate(refs)], phase=phase)

        pl.when(pl.program_id(1) == 0)(lambda: run('init'))
        run('main')
        pl.when(pl.program_id(1) == pl.num_programs(1) - 1)(lambda: run('emit'))
    return kern


def _causal_mask(length, rev):
    row = lax.broadcasted_iota(jnp.int32, (length, length), 0)
    col = lax.broadcasted_iota(jnp.int32, (length, length), 1)
    return (col >= row) if rev else (col <= row)


def _mlstm_kernel(*refs, has_init, emit_state, phase):
    L = CHUNK_A
    ins = list(refs)
    dirs = [ins[0:4], ins[4:8]]
    pos = 8
    if has_init:
        c0_ref, n0_ref, m0_ref = ins[pos:pos + 3]
        pos += 3
    outs = ins[pos:pos + 2]
    pos += 2
    if emit_state:
        cout_ref, nout_ref, mout_ref = ins[pos:pos + 3]
        pos += 3
    cs_ref, ns_ref, ms_ref = ins[pos:pos + 3]

    if phase == 'init':
        if has_init:
            cs_ref[...] = c0_ref[...]
            ns_ref[...] = n0_ref[...]
            ms_ref[...] = m0_ref[...]
        else:
            cs_ref[...] = jnp.zeros_like(cs_ref)
            ns_ref[...] = jnp.zeros_like(ns_ref)
            ms_ref[...] = jnp.zeros_like(ms_ref)
        return
    if phase == 'emit':
        if emit_state:
            cout_ref[...] = cs_ref[...]
            nout_ref[...] = ns_ref[...]
            mout_ref[...] = ms_ref[...]
        return

    for d in range(2):
        rev = d == 1
        q_ref, k_ref, v_ref, s_ref = dirs[d]
        o_ref = outs[d]
        mask = _causal_mask(L, rev)
        tri = mask.astype(F32)
        sm = s_ref[:, 0:LANE]
        lf = _log_sigmoid(sm)
        bc = jnp.dot(tri, lf, precision=HIGHEST, preferred_element_type=F32)
        sm_t = sm.T
        bc_t = bc.T
        last = 0 if rev else L - 1
        for h in range(N_HEAD):
            ci, cf = d * N_HEAD + h, 2 * N_HEAD + d * N_HEAD + h
            li_c, b_c = sm[:, ci:ci + 1], bc[:, cf:cf + 1]
            li_r, b_r = sm_t[ci:ci + 1, :], bc_t[cf:cf + 1, :]
            m_prev = ms_ref[d, h][:, 0:1]
            logw = jnp.where(mask, b_c - b_r + li_r, NEG_BIG)
            from_state = b_c + m_prev
            m_t = jnp.maximum(from_state, jnp.max(logw, axis=-1, keepdims=True))
            w_state = jnp.exp(from_state - m_t)
            hs = slice(h * DH, (h + 1) * DH)
            q = q_ref[:, hs]
            k = k_ref[:, hs] * (DH ** -0.5)
            v = v_ref[:, hs]
            scores = _bdot_nt(q, k) * jnp.exp(logw - m_t)
            c_st = cs_ref[d, h]
            n_st = ns_ref[d, h]
            num = w_state * _bdot(q, c_st) + _bdot(scores, v)
            den = (w_state * jnp.sum(q * n_st, axis=-1, keepdims=True)
                   + jnp.sum(scores, axis=-1, keepdims=True))
            floor = jnp.exp(jnp.minimum(-m_t, MAX_EXP_ARG))
            o_ref[:, hs] = num / jnp.maximum(jnp.abs(den), floor)
            m_new = m_t[last:last + 1, :]
            b_last = b_c[last:last + 1, :]
            kw = k * jnp.exp(b_last - b_c + li_c - m_new)
            decay = jnp.exp(b_last + m_prev - m_new)
            cs_ref[d, h] = decay * c_st + _bdot_tn(kw, v)
            ns_ref[d, h] = decay * n_st + jnp.sum(kw, axis=0, keepdims=True)
            ms_ref[d, h] = jnp.broadcast_to(m_new, (1, LANE))


def _scan_maps(nc):
    fwd = lambda cb: (lambda b, c: (b, c, cb))
    bwd = lambda cb: (lambda b, c: (b, nc - 1 - c, cb))
    return fwd, bwd


def _mlstm(p, bsz, seq, state, emit_state):
    L = CHUNK_A
    nc = seq // L
    n = bsz * seq
    g = min(bsz, SCAN_GROUP)
    has_init = state is not None
    p3 = p.reshape(bsz, seq, N_PACK)
    fwd, bwd = _scan_maps(nc)

    in_specs, args = [], []
    for mk in (fwd, bwd):
        for col in (COL_AQ, COL_AK, COL_AV):
            in_specs.append(pl.BlockSpec((g, L, MIX_W), mk(col // MIX_W)))
            args.append(p3)
        in_specs.append(pl.BlockSpec((g, L, SMALL_W), mk(COL_SMALL // SMALL_W)))
        args.append(p3)
    st_specs = [pl.BlockSpec((g, 2, N_HEAD, DH, DH), lambda b, c: (b, 0, 0, 0, 0)),
                pl.BlockSpec((g, 2, N_HEAD, 1, DH), lambda b, c: (b, 0, 0, 0, 0)),
                pl.BlockSpec((g, 2, N_HEAD, 1, LANE), lambda b, c: (b, 0, 0, 0, 0))]
    st_shapes = [jax.ShapeDtypeStruct((bsz, 2, N_HEAD, DH, DH), F32),
                 jax.ShapeDtypeStruct((bsz, 2, N_HEAD, 1, DH), F32),
                 jax.ShapeDtypeStruct((bsz, 2, N_HEAD, 1, LANE), F32)]
    if has_init:
        c0, n0, m0 = state
        in_specs += st_specs
        args += [c0, n0.reshape(bsz, 2, N_HEAD, 1, DH),
                 jnp.broadcast_to(m0[..., None, None], (bsz, 2, N_HEAD, 1, LANE))]
    out_specs = [pl.BlockSpec((g, L, MIX_W), fwd(0)), pl.BlockSpec((g, L, MIX_W), bwd(0))]
    out_shape = [jax.ShapeDtypeStruct((bsz, seq, MIX_W), F32)] * 2
    if emit_state:
        out_specs += st_specs
        out_shape += st_shapes
    body = functools.partial(_mlstm_kernel, has_init=has_init, emit_state=emit_state)
    res = pl.pallas_call(
        _per_sequence(body, g, ()),
        grid=(bsz // g, nc),
        in_specs=in_specs,
        out_specs=out_specs,
        out_shape=out_shape,
        scratch_shapes=[pltpu.VMEM((g, 2, N_HEAD, DH, DH), F32),
                        pltpu.VMEM((g, 2, N_HEAD, 1, DH), F32),
                        pltpu.VMEM((g, 2, N_HEAD, 1, LANE), F32)],
        compiler_params=_cparams(("parallel", "arbitrary")),
        name="mlstm_scan",
    )(*args)
    o_f, o_b = res[0].reshape(n, MIX_W), res[1].reshape(n, MIX_W)
    new_state = None
    if emit_state:
        new_state = (res[2], res[3].reshape(bsz, 2, N_HEAD, DH), res[4][:, :, :, 0, 0])
    return o_f, o_b, new_state


def _gla_direction(q, k, la, v, s_ref, d, o_ref, *, rev, dk):
    L, S = CHUNK_G, SUB_G
    nsub = L // S
    mask = _causal_mask(L, rev)
    tri = mask.astype(F32)
    b = jnp.dot(tri, la, precision=HIGHEST, preferred_element_type=F32)
    bx = b - la
    ref_rows = [bx[i * S + (S - 1 if rev else 0):i * S + (S - 1 if rev else 0) + 1, :]
                for i in range(nsub)]
    bref = jnp.concatenate([jnp.broadcast_to(r, (S, r.shape[1])) for r in ref_rows], axis=0)
    q_hat = q * jnp.exp(b - bref)
    rowid = lax.broadcasted_iota(jnp.int32, (L, 1), 0)
    k_hat = []
    for i in range(nsub):
        seen = (rowid >= i * S) if rev else (rowid < (i + 1) * S)
        k_hat.append(k * jnp.exp(jnp.where(seen, ref_rows[i] - b, NEG_BIG)))
    last = 0 if rev else L - 1
    b_last = b[last:last + 1, :]
    q_state = q * jnp.exp(b)
    k_state = k * jnp.exp(b_last - b)
    e_last = jnp.exp(b_last)
    for h in range(N_HEAD):
        ks = slice(h * dk, (h + 1) * dk)
        vs = slice(h * DH, (h + 1) * DH)
        vh = v[:, vs]
        blocks = [_bdot_nt(q_hat[i * S:(i + 1) * S, ks], k_hat[i][:, ks]) for i in range(nsub)]
        scores = jnp.where(mask, jnp.concatenate(blocks, axis=0), 0.0)
        s_t = s_ref[d, h]
        o_ref[:, vs] = _bdot_nt(q_state[:, ks], s_t) + _bdot(scores, vh)
        s_ref[d, h] = s_t * e_last[:, ks] + _bdot_tn(vh, k_state[:, ks])


def _scan_state_io(ins, pos, has_init, emit_state, n_out):
    s0_ref = sout_ref = None
    if has_init:
        s0_ref = ins[pos]
        pos += 1
    outs = ins[pos:pos + n_out]
    pos += n_out
    if emit_state:
        sout_ref = ins[pos]
        pos += 1
    return s0_ref, outs, sout_ref, ins[pos]


def _scan_state_phase(phase, s_ref, s0_ref, sout_ref):
    if phase == 'init':
        if s0_ref is not None:
            s_ref[...] = s0_ref[...]
        else:
            s_ref[...] = jnp.zeros_like(s_ref)
    elif phase == 'emit' and sout_ref is not None:
        sout_ref[...] = s_ref[...]
    return phase != 'main'


def _hgrn_kernel(*refs, has_init, emit_state, phase):
    ins = list(refs)
    lb_ref = ins[6]
    s0_ref, outs, sout_ref, s_ref = _scan_state_io(ins, 7, has_init, emit_state, 2)
    if _scan_state_phase(phase, s_ref, s0_ref, sout_ref):
        return
    for d in range(2):
        q_ref, z_ref, v_ref = ins[3 * d:3 * d + 3]
        z = z_ref[...]
        lb = lb_ref[d:d + 1, :]
        la = _log_sigmoid(z) + jnp.log1p(lb * jnp.exp(jnp.minimum(-z, MAX_EXP_ARG)))
        k = (1.0 - lb) * _sigmoid(-z)
        q = q_ref[...] * (DH ** -0.5)
        _gla_direction(q, k, la, v_ref[...], s_ref, d, outs[d], rev=d == 1, dk=DH)


def _gla_kernel(*refs, has_init, emit_state, phase):
    ins = list(refs)
    up_ref, upb_ref = ins[8], ins[9]
    s0_ref, outs, sout_ref, s_ref = _scan_state_io(ins, 10, has_init, emit_state, 2)
    if _scan_state_phase(phase, s_ref, s0_ref, sout_ref):
        return
    for d in range(2):
        q_ref, k_ref, v_ref, sm_ref = ins[4 * d:4 * d + 4]
        zg = jnp.dot(sm_ref[...], up_ref[d], precision=HIGHEST,
                     preferred_element_type=F32) + upb_ref[d:d + 1, :]
        la = _log_sigmoid(zg) * (1.0 / GLA_TAU)
        q = q_ref[...] * (DK_C ** -0.5)
        _gla_direction(q, k_ref[...], la, v_ref[...], s_ref, d, outs[d], rev=d == 1, dk=DK_C)


def _gated_scan(kernel, p, bsz, seq, cols, extra, extra_specs, dk, state_t, emit_state, name):
    L = CHUNK_G
    nc = seq // L
    n = bsz * seq
    g = min(bsz, SCAN_GROUP)
    has_init = state_t is not None
    p3 = p.reshape(bsz, seq, N_PACK)
    fwd, bwd = _scan_maps(nc)

    in_specs, args = [], []
    for d, mk in enumerate((fwd, bwd)):
        for off, width in cols[d]:
            in_specs.append(pl.BlockSpec((g, L, width), mk(off // width)))
            args.append(p3)
    shared = tuple(range(len(args), len(args) + len(extra)))
    in_specs += extra_specs
    args += extra
    st_spec = pl.BlockSpec((g, 2, N_HEAD, DH, dk), lambda b, c: (b, 0, 0, 0, 0))
    if has_init:
        in_specs.append(st_spec)
        args.append(state_t)
    out_specs = [pl.BlockSpec((g, L, MIX_W), fwd(0)), pl.BlockSpec((g, L, MIX_W), bwd(0))]
    out_shape = [jax.ShapeDtypeStruct((bsz, seq, MIX_W), F32)] * 2
    if emit_state:
        out_specs.append(st_spec)
        out_shape.append(jax.ShapeDtypeStruct((bsz, 2, N_HEAD, DH, dk), F32))
    body = functools.partial(kernel, has_init=has_init, emit_state=emit_state)
    res = pl.pallas_call(
        _per_sequence(body, g, shared),
        grid=(bsz // g, nc),
        in_specs=in_specs,
        out_specs=out_specs,
        out_shape=out_shape,
        scratch_shapes=[pltpu.VMEM((g, 2, N_HEAD, DH, dk), F32)],
        compiler_params=_cparams(("parallel", "arbitrary")),
        name=name,
    )(*args)
    return res[0].reshape(n, MIX_W), res[1].reshape(n, MIX_W), (res[2] if emit_state else None)


def _hgrn(p, bsz, seq, lb, state_t, emit_state):
    cols = [[(COL_BQ, MIX_W), (COL_BF + d * MIX_W, MIX_W), (COL_BI, MIX_W)] for d in range(2)]
    return _gated_scan(_hgrn_kernel, p, bsz, seq, cols, [lb],
                       [pl.BlockSpec((2, MIX_W), lambda b, c: (0, 0))], DH, state_t, emit_state,
                       "hgrn2_scan")


def _gla(p, bsz, seq, up_pad, up_b, state_t, emit_state):
    kw = N_HEAD * DK_C
    cols = [[(COL_CQ, kw), (COL_CK, kw), (COL_CV, MIX_W), (COL_SMALL, SMALL_W)] for _ in range(2)]
    return _gated_scan(_gla_kernel, p, bsz, seq, cols, [up_pad, up_b],
                       [pl.BlockSpec((2, SMALL_W, kw), lambda b, c: (0, 0, 0)),
                        pl.BlockSpec((2, kw), lambda b, c: (0, 0))], DK_C, state_t, emit_state,
                       "gla_scan")


def _shift_rows(x, k, rev, fill):
    n = x.shape[0]
    rowid = lax.broadcasted_iota(jnp.int32, x.shape, 0)
    if rev:
        return jnp.where(rowid >= n - k, fill, pltpu.roll(x, n - k, 0))
    return jnp.where(rowid < k, fill, pltpu.roll(x, k, 0))


def _lru_kernel(*refs, seq, has_init, emit_state):
    L = CHUNK_D
    nc = seq // L
    ins = list(refs)
    (dx_ref, dg_ref, cw_ref, cb_ref, wah_ref, wal_ref, ba_ref, wxh_ref, wxl_ref, bx_ref,
     lam_ref) = ins[0:11]
    pos = 11
    h0_ref = hout_ref = None
    if has_init:
        h0_ref = ins[pos]
        pos += 1
    y_ref = ins[pos]
    pos += 1
    if emit_state:
        hout_ref = ins[pos]
        pos += 1
    pad_ref, hb_ref = ins[pos], ins[pos + 1]

    zeros8 = jnp.zeros((8, LANE), F32)
    pad_ref[0:8, :] = zeros8
    pad_ref[8:8 + seq, :] = dx_ref[...]
    pad_ref[8 + seq:16 + seq, :] = zeros8
    lam = lam_ref[...]
    sp = jnp.maximum(-lam, 0.0) + jnp.log1p(jnp.exp(-jnp.abs(lam)))
    cw = cw_ref[...]
    cb = cb_ref[...]

    def chunk_scan(c, carry, d):
        rev = d == 1
        start = pl.multiple_of(c * L, L)
        win = pad_ref[pl.ds(start, L + 16), :]
        xd = cb + sum(cw[j:j + 1, :] * win[7 + j:7 + j + L, :] for j in range(CONV_W))
        r = _sigmoid(_dot3(xd, wah_ref[d, 0], wal_ref[d, 0]) + ba_ref[d:d + 1, :])
        ig = _sigmoid(_dot3(xd, wxh_ref[d, 0], wxl_ref[d, 0]) + bx_ref[d:d + 1, :])
        log_a = -LRU_C * r * sp[d:d + 1, :]
        a = jnp.exp(log_a)
        u = jnp.sqrt(jnp.maximum(-_expm1(2.0 * log_a), 0.0)) * (ig * xd)
        k = 1
        while k < L:
            u = a * _shift_rows(u, k, rev, 0.0) + u
            a = a * _shift_rows(a, k, rev, 1.0)
            k *= 2
        h = a * carry + u
        last = 0 if rev else L - 1
        return start, h, h[last:last + 1, :]

    def scan_body(i, carry):
        start_f, h_f, new_f = chunk_scan(i, carry[0], 0)
        start_b, h_b, new_b = chunk_scan(nc - 1 - i, carry[1], 1)
        y_ref[pl.ds(start_f, L), :] = h_f
        hb_ref[pl.ds(start_b, L), :] = h_b
        return new_f, new_b

    def gate_body(c, carry):
        rows = pl.ds(pl.multiple_of(c * L, L), L)
        y_ref[rows, :] = (y_ref[rows, :] + hb_ref[rows, :]) * _gelu(dg_ref[rows, :])
        return carry

    if has_init:
        init = (h0_ref[0, 0:1, :], h0_ref[0, 1:2, :])
    else:
        init = (jnp.zeros((1, LANE), F32), jnp.zeros((1, LANE), F32))
    fin_f, fin_b = lax.fori_loop(0, nc, scan_body, init)
    lax.fori_loop(0, nc, gate_body, 0)
    if emit_state:
        hout_ref[0, 0:1, :] = fin_f
        hout_ref[0, 1:2, :] = fin_b


def _lru(p, bsz, seq, conv_w, conv_b, wa, ba, wx, bx, lam, h0, emit_state):
    n = bsz * seq
    has_init = h0 is not None
    wah, wal = _split(wa)
    wxh, wxl = _split(wx)
    gate_w = pl.BlockSpec((2, 1, DH, DH), lambda b, h: (0, h, 0, 0))
    vec2 = pl.BlockSpec((2, LANE), lambda b, h: (0, h))
    in_specs = [pl.BlockSpec((seq, LANE), lambda b, h: (b, COL_DX // LANE + h)),
                pl.BlockSpec((seq, LANE), lambda b, h: (b, COL_DG // LANE + h)),
                pl.BlockSpec((CONV_W, LANE), lambda b, h: (0, h)),
                pl.BlockSpec((1, LANE), lambda b, h: (0, h)),
                gate_w, gate_w, vec2, gate_w, gate_w, vec2, vec2]
    args = [p, p, conv_w, conv_b.reshape(1, MIX_W), wah, wal, ba, wxh, wxl, bx, lam]
    st_spec = pl.BlockSpec((1, 2, LANE), lambda b, h: (b, 0, h))
    if has_init:
        in_specs.append(st_spec)
        args.append(h0)
    out_specs = [pl.BlockSpec((seq, LANE), lambda b, h: (b, h))]
    out_shape = [jax.ShapeDtypeStruct((n, MIX_W), F32)]
    if emit_state:
        out_specs.append(st_spec)
        out_shape.append(jax.ShapeDtypeStruct((bsz, 2, MIX_W), F32))
    res = pl.pallas_call(
        functools.partial(_lru_kernel, seq=seq, has_init=has_init, emit_state=emit_state),
        grid=(bsz, N_HEAD),
        in_specs=in_specs,
        out_specs=out_specs,
        out_shape=out_shape,
        scratch_shapes=[pltpu.VMEM((seq + 16, LANE), F32), pltpu.VMEM((seq, LANE), F32)],
        compiler_params=_cparams(("parallel", "parallel")),
        name="conv_rglru",
    )(*args)
    return res[0], (res[1] if emit_state else None)


def _head_rms(x):
    parts = []
    for h in range(N_HEAD):
        xh = x[:, h * DH:(h + 1) * DH]
        parts.append(xh * lax.rsqrt(jnp.mean(xh * xh, axis=-1, keepdims=True) + EPS))
    return jnp.concatenate(parts, axis=-1)


def _merge_kernel(af_ref, ab_ref, bf_ref, bb_ref, cf_ref, cb_ref, yd_ref, ao_ref, bg_ref, cg_ref,
                  gt0_ref, gt1_ref, gt2_ref, gt3_ref, x_ref, mod_ref, g2_ref, wbr_ref, wout_ref,
                  xo_ref, h2_ref, h2b_ref):
    ya = _sigmoid(ao_ref[...]) * _head_rms(af_ref[...] + ab_ref[...])
    yb = _silu(bg_ref[...]) * _head_rms(bf_ref[...] + bb_ref[...])
    yc = _silu(cg_ref[...]) * _head_rms(cf_ref[...] + cb_ref[...])
    merged = None
    gate_refs = (gt0_ref, gt1_ref, gt2_ref, gt3_ref)
    for i, y in enumerate((ya, yb, yc, yd_ref[...])):
        proj = jnp.dot(y.astype(BF16), wbr_ref[i], preferred_element_type=F32)
        term = _sigmoid(gate_refs[i][...]) * proj
        merged = term if merged is None else merged + term
    out = jnp.dot(merged.astype(BF16), wout_ref[...], preferred_element_type=F32)
    g1 = mod_ref[0, :, 2 * D_MODEL:3 * D_MODEL]
    sh2 = mod_ref[0, :, 3 * D_MODEL:4 * D_MODEL]
    sc2 = mod_ref[0, :, 4 * D_MODEL:5 * D_MODEL]
    x = x_ref[...] + g1 * out
    xo_ref[...] = x
    inv = lax.rsqrt(jnp.mean(x * x, axis=-1, keepdims=True) + EPS)
    h2 = x * inv * g2_ref[...] * (1.0 + sc2) + sh2
    h2_ref[...] = h2
    h2b_ref[...] = h2.astype(BF16)


def _merge(mix_outs, yd, p, x, mod_l, norm2_g, w_branch, w_out, seq_len, layer):
    n = x.shape[0]
    tn = 256
    rows = mod_l.shape[0]
    mod_map = (lambda i: (0, 0, 0)) if rows == 1 else (lambda i: ((i * tn) // seq_len, 0, 0))
    tok = lambda cb: (lambda i: (i, cb))
    in_specs = [pl.BlockSpec((tn, MIX_W), tok(0))] * 7
    in_specs += [pl.BlockSpec((tn, MIX_W), tok(COL_AO // MIX_W)),
                 pl.BlockSpec((tn, MIX_W), tok(COL_BG // MIX_W)),
                 pl.BlockSpec((tn, MIX_W), tok(COL_CG // MIX_W)),
                 *[pl.BlockSpec((tn, D_MODEL), tok(COL_GATES // D_MODEL + i)) for i in range(N_BRANCH)],
                 pl.BlockSpec((tn, D_MODEL), tok(0)),
                 pl.BlockSpec((1, 1, 6 * D_MODEL), mod_map),
                 pl.BlockSpec((1, D_MODEL), lambda i: (0, 0)),
                 pl.BlockSpec((None, N_BRANCH, MIX_W, D_MODEL), lambda i: (layer, 0, 0, 0)),
                 pl.BlockSpec((None, D_MODEL, D_MODEL), lambda i: (layer, 0, 0))]
    out_spec = pl.BlockSpec((tn, D_MODEL), tok(0))
    return pl.pallas_call(
        _merge_kernel,
        grid=(n // tn,),
        in_specs=in_specs,
        out_specs=[out_spec, out_spec, out_spec],
        out_shape=[jax.ShapeDtypeStruct((n, D_MODEL), F32), jax.ShapeDtypeStruct((n, D_MODEL), F32),
                   jax.ShapeDtypeStruct((n, D_MODEL), BF16)],
        compiler_params=_cparams(("parallel",)),
        name="branch_merge",
    )(*mix_outs, yd, p, p, p, p, p, p, p, x, mod_l, norm2_g, w_branch, w_out)


ROUTE_TN = 256
ROUTE_LG = LANE


def _top16_exact(vals):
    n_rows = vals.shape[0]
    rowid = lax.broadcasted_iota(jnp.int32, vals.shape, 0).astype(F32)
    rank = jnp.full(vals.shape, float(PEER_TOPK), F32)
    tops = []
    for r in range(PEER_TOPK):
        m = jnp.max(vals, axis=0, keepdims=True)
        idx = jnp.min(jnp.where(vals == m, rowid, float(n_rows)), axis=0, keepdims=True)
        sel = rowid == idx
        rank = jnp.where(sel, float(r), rank)
        vals = jnp.where(sel, -jnp.inf, vals)
        tops.append(m)
    return jnp.concatenate(tops, axis=0), rank


_TAKEN_BASE = -3.0e38
_TAKEN_STEP = 2.0e36
_TAKEN_BELOW = _TAKEN_BASE + 0.5 * _TAKEN_STEP


def _top16_quick(vals):
    work = vals
    tops = []
    for r in range(PEER_TOPK):
        m = jnp.max(work, axis=0, keepdims=True)
        work = jnp.where(work == m, _TAKEN_BASE - r * _TAKEN_STEP, work)
        tops.append(m)
    taken = work <= _TAKEN_BELOW
    rank = jnp.where(taken, jnp.round((_TAKEN_BASE - work) * (1.0 / _TAKEN_STEP)), float(PEER_TOPK))
    n_taken = jnp.sum(jnp.where(taken, 1.0, 0.0), axis=0, keepdims=True)
    return jnp.concatenate(tops, axis=0), rank, n_taken


_CAND_PIECES = (('a', 0, 0, 16), ('a', 0, 8, 16), ('a', 1, 0, 8), ('a', 2, 0, 5), ('a', 3, 0, 4),
                ('b', 0, 0, (4, 8)), ('b', 0, 8, (8, 16)), ('b', 1, 0, (4, 8)), ('b', 2, 0, (4, 5)))


def _pair_pieces(x1, x2, combine):
    out = []
    for kind, fixed, off, _ in _CAND_PIECES:
        if kind == 'a':
            out.append(combine(x1[fixed:fixed + 1, :], x2[off:off + 8, :]))
        else:
            out.append(combine(x1[off:off + 8, :], x2[fixed:fixed + 1, :]))
    return out


def _candidate_sums(s1, s2):
    K = PEER_TOPK
    n = s1.shape[1]
    iota8 = lax.broadcasted_iota(jnp.int32, (8, n), 0)
    sums = _pair_pieces(s1, s2, lambda x, y: x + y)
    vals, poss = [], []
    for (kind, fixed, off, lim), sm in zip(_CAND_PIECES, sums):
        idx = iota8 + off
        if kind == 'a':
            valid = idx < lim
            pos = fixed * K + idx
        else:
            valid = (idx >= lim[0]) & (idx < lim[1])
            pos = idx * K + fixed
        vals.append(jnp.where(valid, sm, -jnp.inf))
        poss.append(jnp.where(valid, pos, K * K).astype(F32))
    return jnp.concatenate(vals, axis=0), jnp.concatenate(poss, axis=0)


def _choose_exact(vals, posid):
    K = PEER_TOPK
    chosen = jnp.zeros(vals.shape, F32)
    for _ in range(K):
        m = jnp.max(vals, axis=0, keepdims=True)
        idx = jnp.min(jnp.where(vals == m, posid, float(K * K)), axis=0, keepdims=True)
        sel = posid == idx
        chosen = jnp.where(sel, 1.0, chosen)
        vals = jnp.where(sel, -jnp.inf, vals)
    return chosen


def _choose_quick(vals):
    for _ in range(PEER_TOPK):
        m = jnp.max(vals, axis=0, keepdims=True)
        vals = jnp.where(vals == m, _TAKEN_BASE, vals)
    chosen = jnp.where(vals == _TAKEN_BASE, 1.0, 0.0)
    return chosen, jnp.sum(chosen, axis=0, keepdims=True)


def _route_kernel(h_ref, wqh_ref, wql_ref, kh_ref, kl_ref, a_ref, b_ref, r2_ref, c_ref,
                  qh_ref, ql_ref, s_ref, top_ref, rank_ref, ch_ref):
    K = PEER_TOPK
    n_lg = ROUTE_TN // ROUTE_LG
    q = _dot3(h_ref[...], wqh_ref[...], wql_ref[...])
    q_hi, q_lo = _split(q)
    for i in range(2 * PEER_HEADS):
        qh_ref[i] = q_hi[:, i * N_KEYS:(i + 1) * N_KEYS]
        ql_ref[i] = q_lo[:, i * N_KEYS:(i + 1) * N_KEYS]
    nt = (((1,), (1,)), ((), ()))
    lanes = [slice(lg * ROUTE_LG, (lg + 1) * ROUTE_LG) for lg in range(n_lg)]

    def miscount(counts):
        worst = None
        for cnt in counts:
            dev = jnp.abs(cnt - float(K))
            worst = dev if worst is None else jnp.maximum(worst, dev)
        return jnp.max(worst) > 0.0

    def head_body(h, carry):
        for half in range(2):
            kh, kl = kh_ref[h, half], kl_ref[h, half]
            qh, ql = qh_ref[2 * h + half], ql_ref[2 * h + half]
            s_ref[half] = (lax.dot_general(kh, qh, nt, preferred_element_type=F32)
                           + lax.dot_general(kl, qh, nt, preferred_element_type=F32)
                           + lax.dot_general(kh, ql, nt, preferred_element_type=F32))

        counts = []
        for half in range(2):
            for lg in range(n_lg):
                top, rank, n_taken = _top16_quick(s_ref[half, :, lanes[lg]])
                top_ref[half, lg] = top
                rank_ref[half, lg] = rank
                counts.append(n_taken)

        @pl.when(miscount(counts))
        def _():
            for half in range(2):
                for lg in range(n_lg):
                    top, rank = _top16_exact(s_ref[half, :, lanes[lg]])
                    top_ref[half, lg] = top
                    rank_ref[half, lg] = rank

        counts = []
        for lg in range(n_lg):
            vals, _ = _candidate_sums(top_ref[0, lg], top_ref[1, lg])
            chosen, n_taken = _choose_quick(vals)
            ch_ref[lg] = chosen
            counts.append(n_taken)

        @pl.when(miscount(counts))
        def _():
            for lg in range(n_lg):
                vals, posid = _candidate_sums(top_ref[0, lg], top_ref[1, lg])
                ch_ref[lg] = _choose_exact(vals, posid)

        for lg in range(n_lg):
            ls = lanes[lg]
            st1, st2 = s_ref[0, :, ls], s_ref[1, :, ls]
            s1, s2 = top_ref[0, lg], top_ref[1, lg]
            rank1, rank2 = rank_ref[0, lg], rank_ref[1, lg]
            chosen = ch_ref[lg]
            ch = [chosen[8 * i:8 * i + 8, :] for i in range(len(_CAND_PIECES))]
            e1 = jnp.exp(s1 - s1[0:1, :])
            e2 = jnp.exp(s2 - s2[0:1, :])
            pair = _pair_pieces(e1, e2, lambda x, y: x * y)
            z = sum(jnp.sum(c * p, axis=0, keepdims=True) for c, p in zip(ch, pair))
            iota8 = lax.broadcasted_iota(jnp.int32, (8, ROUTE_LG), 0)
            low = ch[5] + ch[7] + ch[8]
            for a, cnt in ((3, ch[4]), (2, ch[3]), (1, ch[2]), (0, ch[0] + ch[1])):
                low = jnp.where(iota8 == a, jnp.sum(cnt, axis=0, keepdims=True), low)
            counts = jnp.concatenate([low, ch[6]], axis=0)
            c_dense = jnp.zeros((N_KEYS, ROUTE_LG), F32)
            for a in range(K):
                c_dense = jnp.where(rank1 == float(a), counts[a:a + 1, :], c_dense)
            a_ref[h, :, ls] = jnp.where(rank1 < float(K), jnp.exp(st1 - s1[0:1, :]), 0.0) / z
            b_ref[h, :, ls] = jnp.where(rank2 < float(K), jnp.exp(st2 - s2[0:1, :]), 0.0).astype(BF16)
            r2_ref[h, :, ls] = rank2.astype(BF16)
            c_ref[h, :, ls] = c_dense
        return carry

    lax.fori_loop(0, PEER_HEADS, head_body, 0)


def _route(h2, wq_hi, wq_lo, keys_hi, keys_lo, layer):
    n = h2.shape[0]
    tn = ROUTE_TN
    dense = pl.BlockSpec((PEER_HEADS, N_KEYS, tn), lambda i: (0, 0, i))
    wspec = pl.BlockSpec((None, D_MODEL, PEER_HEADS * 2 * N_KEYS), lambda i: (layer, 0, 0))
    kspec = pl.BlockSpec((None, PEER_HEADS, 2, N_KEYS, N_KEYS), lambda i: (layer, 0, 0, 0, 0))
    return pl.pallas_call(
        _route_kernel,
        grid=(n // tn,),
        in_specs=[pl.BlockSpec((tn, D_MODEL), lambda i: (i, 0)), wspec, wspec, kspec, kspec],
        out_specs=[dense] * 4,
        out_shape=[jax.ShapeDtypeStruct((PEER_HEADS, N_KEYS, n), dt) for dt in (F32, BF16, BF16, F32)],
        scratch_shapes=[pltpu.VMEM((2 * PEER_HEADS, tn, N_KEYS), BF16),
                        pltpu.VMEM((2 * PEER_HEADS, tn, N_KEYS), BF16),
                        pltpu.VMEM((2, N_KEYS, tn), F32),
                        pltpu.VMEM((2, tn // ROUTE_LG, PEER_TOPK, ROUTE_LG), F32),
                        pltpu.VMEM((2, tn // ROUTE_LG, N_KEYS, ROUTE_LG), F32),
                        pltpu.VMEM((tn // ROUTE_LG, 8 * len(_CAND_PIECES), ROUTE_LG), F32)],
        compiler_params=_cparams(("parallel",)),
        name="peer_route",
    )(h2, wq_hi, wq_lo, keys_hi, keys_lo)


PEER_TN = 512
PEER_TE = 1024


def _expert_kernel(hb_ref, u_ref, vt_ref, a_ref, b_ref, r2_ref, c_ref, x_ref, mod_ref, o_ref,
                   acc_ref, t0_ref, t1_ref):
    s = pl.program_id(1)
    n_tiles = pl.num_programs(1) - 1
    rows_per_tile = PEER_TE // N_KEYS

    @pl.when(s == 0)
    def _():
        acc_ref[...] = jnp.zeros_like(acc_ref)
        t1_ref[...] = jnp.zeros_like(t1_ref)

    def step(t_new_ref, t_old_ref):
        t_new_ref[...] = lax.dot_general(u_ref[...], hb_ref[...], (((1,), (1,)), ((), ())),
                                         preferred_element_type=F32)
        e_old = jnp.maximum(s - 1, 0)
        update = None
        for half in range(2):
            w_parts = []
            for j in range(half * rows_per_tile // 2, (half + 1) * rows_per_tile // 2):
                e1 = e_old * rows_per_tile + j
                g = None
                for h in range(PEER_HEADS):
                    a_row = a_ref[h, pl.ds(e1, 1), :].astype(BF16)
                    c_row = c_ref[h, pl.ds(e1, 1), :].astype(BF16)
                    term = a_row * jnp.where(r2_ref[h] < c_row, b_ref[h], 0.0)
                    g = term if g is None else g + term
                act = _gelu(t_old_ref[j * N_KEYS:(j + 1) * N_KEYS, :])
                w_parts.append(g * act.astype(BF16))
            w_t = jnp.concatenate(w_parts, axis=0)
            cols = slice(half * PEER_TE // 2, (half + 1) * PEER_TE // 2)
            part = jnp.dot(vt_ref[:, cols], w_t, preferred_element_type=F32)
            update = part if update is None else update + part
        acc_ref[...] += update

    @pl.when(s % 2 == 0)
    def _():
        step(t0_ref, t1_ref)

    @pl.when(s % 2 == 1)
    def _():
        step(t1_ref, t0_ref)

    @pl.when(s == n_tiles)
    def _():
        g2 = mod_ref[0, :, 5 * D_MODEL:6 * D_MODEL]
        o_ref[...] = x_ref[...] + g2 * acc_ref[...].T


def _experts(h2b, u_b, vt_b, dense, x, mod_l, seq_len, layer):
    n = x.shape[0]
    tn, te = PEER_TN, PEER_TE
    n_tiles = N_EXPERTS // te
    rows = mod_l.shape[0]
    mod_map = (lambda i, s: (0, 0, 0)) if rows == 1 else (lambda i, s: ((i * tn) // seq_len, 0, 0))
    dspec = pl.BlockSpec((PEER_HEADS, N_KEYS, tn), lambda i, s: (0, 0, i))
    return pl.pallas_call(
        _expert_kernel,
        grid=(n // tn, n_tiles + 1),
        in_specs=[pl.BlockSpec((tn, D_MODEL), lambda i, s: (i, 0)),
                  pl.BlockSpec((None, te, D_MODEL),
                               lambda i, s: (layer, jnp.minimum(s, n_tiles - 1), 0)),
                  pl.BlockSpec((None, D_MODEL, te), lambda i, s: (layer, 0, jnp.maximum(s - 1, 0))),
                  dspec, dspec, dspec, dspec,
                  pl.BlockSpec((tn, D_MODEL), lambda i, s: (i, 0)),
                  pl.BlockSpec((1, 1, 6 * D_MODEL), mod_map)],
        out_specs=pl.BlockSpec((tn, D_MODEL), lambda i, s: (i, 0)),
        out_shape=jax.ShapeDtypeStruct((n, D_MODEL), F32),
        scratch_shapes=[pltpu.VMEM((D_MODEL, tn), F32), pltpu.VMEM((te, tn), F32),
                        pltpu.VMEM((te, tn), F32)],
        compiler_params=_cparams(("parallel", "arbitrary")),
        name="peer_experts",
    )(h2b, u_b, vt_b, *dense, x, mod_l)


def _final_norm_kernel(x_ref, g_ref, o_ref):
    x = x_ref[...]
    o_ref[...] = x * lax.rsqrt(jnp.mean(x * x, axis=-1, keepdims=True) + EPS) * g_ref[...]


def _final_norm(x, g):
    n = x.shape[0]
    tn = 512
    return pl.pallas_call(
        _final_norm_kernel,
        grid=(n // tn,),
        in_specs=[pl.BlockSpec((tn, D_MODEL), lambda i: (i, 0)),
                  pl.BlockSpec((1, D_MODEL), lambda i: (0, 0))],
        out_specs=pl.BlockSpec((tn, D_MODEL), lambda i: (i, 0)),
        out_shape=jax.ShapeDtypeStruct((n, D_MODEL), F32),
        compiler_params=_cparams(("parallel",)),
        name="final_norm",
    )(x, g)


def _pack_columns(a):
    small = jnp.concatenate([a[..., _ORIG_AI:_ORIG_BQ], a[..., _ORIG_CLR:_ORIG_DX]], axis=-1)
    pad = jnp.zeros(a.shape[:-1] + (SMALL_W - small.shape[-1],), a.dtype)
    return jnp.concatenate([a[..., 0:_ORIG_AI], a[..., _ORIG_BQ:_ORIG_CLR], a[..., _ORIG_DX:_ORIG_END],
                            small, pad], axis=-1)


def _position_code(rows):
    quarter = D_MODEL // 4
    omega = 1.0 / (POS_BASE ** (jnp.arange(quarter, dtype=F32) / quarter))
    r, col = jnp.meshgrid(jnp.arange(rows, dtype=F32), jnp.arange(GRID_W, dtype=F32), indexing='ij')

    def enc(pos):
        ang = pos.reshape(-1, 1) * omega
        return jnp.concatenate([jnp.sin(ang), jnp.cos(ang)], axis=-1)
    return jnp.concatenate([enc(r), enc(col)], axis=-1)


def _layer(x, bsz, seq, mod_l, lp, state, emit_state):
    layer = lp['layer']
    p = _inproj(x, mod_l, lp['norm1_g'], lp['w_in'], lp['b_in'], seq, layer)
    if state is None:
        st_a = st_b = st_c = st_d = None
    else:
        c0, n0, m0, sb0, sc0, hd0 = state
        st_a = (c0, n0, m0)
        st_b = jnp.swapaxes(sb0, -1, -2)
        st_c = jnp.swapaxes(sc0, -1, -2)
        st_d = hd0
    a_f, a_b, new_a = _mlstm(p, bsz, seq, st_a, emit_state)
    b_f, b_b, new_b = _hgrn(p, bsz, seq, lp['hgrn_lb'], st_b, emit_state)
    c_f, c_b, new_c = _gla(p, bsz, seq, lp['gla_up_pad'], lp['gla_up_b'], st_c, emit_state)
    yd, new_d = _lru(p, bsz, seq, lp['conv_w'], lp['conv_b'], lp['lru_w_a'], lp['lru_b_a'],
                     lp['lru_w_x'], lp['lru_b_x'], lp['lru_lambda'], st_d, emit_state)
    x1, h2, h2b = _merge((a_f, a_b, b_f, b_b, c_f, c_b), yd, p, x, mod_l, lp['norm2_g'],
                         lp['w_branch'], lp['w_out'], seq, layer)
    dense = _route(h2, lp['wq_hi'], lp['wq_lo'], lp['keys_hi'], lp['keys_lo'], layer)
    x2 = _experts(h2b, lp['peer_u'], lp['peer_vt'], dense, x1, mod_l, seq, layer)
    new_state = None
    if emit_state:
        new_state = (*new_a, jnp.swapaxes(new_b, -1, -2), jnp.swapaxes(new_c, -1, -2), new_d)
    return x2, new_state


def kernel(x_prompt, x_sample, state_mlstm_C, state_mlstm_n, state_mlstm_m, state_hgrn_S,
           state_gla_S, state_lru_h, c, c_ctx, norm1_g, norm2_g, final_norm_g, w_mod, b_mod,
           w_in, b_in, w_gla_up, b_gla_up, hgrn_lower_bounds, conv_w, conv_b, lru_w_a, lru_b_a,
           lru_w_x, lru_b_x, lru_lambda, w_branch, w_out, peer_w_q, peer_sub_keys, peer_u, peer_v):
    lb_soft = jax.nn.softmax(hgrn_lower_bounds.astype(F32), axis=0)
    lb_all = jnp.cumsum(lb_soft, axis=0) - lb_soft[0:1]
    w_pack = _pack_columns(w_in).astype(BF16)
    b_pack = _pack_columns(b_in).reshape(DEPTH, 1, N_PACK)
    kw = N_HEAD * DK_C
    up_pad = jnp.zeros((DEPTH, 2, SMALL_W, kw), F32)
    for d in range(2):
        lo = 2 * N_HEAD * 2 + d * R_C
        up_pad = up_pad.at[:, d, lo:lo + R_C, :].set(w_gla_up[:, d].astype(F32))
    wq_hi, wq_lo = _split(peer_w_q)
    keys_hi, keys_lo = _split(peer_sub_keys)
    u_b = peer_u.astype(BF16)
    vt_b = jnp.swapaxes(peer_v, 1, 2).astype(BF16)
    wbr_b = w_branch.astype(BF16)
    wout_b = w_out.astype(BF16)

    def layer_params(l):
        return {
            'norm1_g': norm1_g[l].reshape(1, D_MODEL), 'norm2_g': norm2_g[l].reshape(1, D_MODEL),
            'layer': l, 'w_in': w_pack, 'b_in': b_pack[l], 'hgrn_lb': lb_all[l],
            'gla_up_pad': up_pad[l], 'gla_up_b': b_gla_up[l],
            'conv_w': conv_w[l], 'conv_b': conv_b[l], 'lru_w_a': lru_w_a[l], 'lru_b_a': lru_b_a[l],
            'lru_w_x': lru_w_x[l], 'lru_b_x': lru_b_x[l], 'lru_lambda': lru_lambda[l],
            'w_branch': wbr_b, 'w_out': wout_b, 'wq_hi': wq_hi, 'wq_lo': wq_lo,
            'keys_hi': keys_hi, 'keys_lo': keys_lo, 'peer_u': u_b, 'peer_vt': vt_b,
        }

    cond = jnp.concatenate([c, c_ctx[None, :]], axis=0).astype(F32)
    n_dec = c.shape[0]
    mod = _modulation(cond, w_mod, b_mod)
    final_g = final_norm_g.reshape(1, D_MODEL)

    bp, tp, _ = x_prompt.shape
    x = x_prompt.reshape(bp * tp, D_MODEL)
    ctx_states = []
    for l in range(DEPTH):
        mod_l = mod[l, n_dec:n_dec + 1].reshape(1, 1, 6 * D_MODEL)
        x, st = _layer(x, bp, tp, mod_l, layer_params(l), None, True)
        ctx_states.append(st)
    y_prompt = _final_norm(x, final_g).reshape(bp, tp, D_MODEL)
    new_states = tuple(jnp.stack([s[i] for s in ctx_states], axis=1) for i in range(6))

    bd, td, _ = x_sample.shape
    x = _add_position(x_sample, _position_code(td // GRID_W)).reshape(bd * td, D_MODEL)
    for l in range(DEPTH):
        cached = (state_mlstm_C[:, l].astype(F32), state_mlstm_n[:, l].astype(F32),
                  state_mlstm_m[:, l].astype(F32), state_hgrn_S[:, l].astype(F32),
                  state_gla_S[:, l].astype(F32), state_lru_h[:, l].astype(F32))
        mod_l = mod[l, 0:n_dec].reshape(n_dec, 1, 6 * D_MODEL)
        x, _ = _layer(x, bd, td, mod_l, layer_params(l), cached, False)
    y_sample = _final_norm(x, final_g).reshape(bd, td, D_MODEL)
    return (y_prompt, y_sample) + new_states
```

```python
import functools

import jax
import jax.numpy as jnp
from jax import lax
from jax.experimental import pallas as pl
from jax.experimental.pallas import tpu as pltpu

F32 = jnp.float32
BF16 = jnp.bfloat16
HIGHEST = lax.Precision.HIGHEST

D_MODEL = 1024
DEPTH = 4
GRID_W = 64
EPS = 1e-6
NEG_BIG = -1e30
MAX_EXP_ARG = 80.0
MIX_W = 512
N_HEAD = 4
DH = 128
DK_C = 64
R_C = 16
GLA_TAU = 16.0
CONV_W = 4
LRU_C = 8.0
N_BRANCH = 4
N_KEYS = 128
N_EXPERTS = N_KEYS * N_KEYS
PEER_HEADS = 8
PEER_TOPK = 16
POS_BASE = 10000.0

LANE = 128
VMEM_LIMIT = 56 * 1024 * 1024

COL_GATES = 0
COL_AQ, COL_AK, COL_AV, COL_AO = 4096, 4608, 5120, 5632
COL_BQ, COL_BI, COL_BG = 6144, 6656, 7168
COL_CQ, COL_CK, COL_CV, COL_CG = 7680, 7936, 8192, 8704
COL_DG = 9216
N_WIDE = 9728
COL_BF, COL_DX = 0, 1024
COL_SMALL = 1536
SMALL_W = 256
N_FINE = COL_SMALL + SMALL_W

_WIDE_SRC = ((7216, 11312), (0, 2048), (2064, 2576), (3600, 4624), (4624, 6160), (6704, 7216))
_FINE_SRC = ((2576, 3600), (6192, 6704), (2048, 2064), (6160, 6192))

CHUNK_A = 128
CHUNK_G = 64
SUB_G = 16
CHUNK_D = 128


def _cparams(sem):
    return pltpu.CompilerParams(dimension_semantics=sem, vmem_limit_bytes=VMEM_LIMIT)


def _bdot(a, b):
    return jnp.dot(a.astype(BF16), b.astype(BF16), preferred_element_type=F32)


def _bdot_nt(a, b):
    return lax.dot_general(a.astype(BF16), b.astype(BF16), (((1,), (1,)), ((), ())),
                           preferred_element_type=F32)


def _bdot_tn(a, b):
    return lax.dot_general(a.astype(BF16), b.astype(BF16), (((0,), (0,)), ((), ())),
                           preferred_element_type=F32)


def _split(a):
    hi = a.astype(BF16)
    lo = (a - hi.astype(F32)).astype(BF16)
    return hi, lo


def _dot3(a, b_hi, b_lo):
    a_hi, a_lo = _split(a)
    return (jnp.dot(a_hi, b_hi, preferred_element_type=F32)
            + jnp.dot(a_lo, b_hi, preferred_element_type=F32)
            + jnp.dot(a_hi, b_lo, preferred_element_type=F32))


def _log_sigmoid(z):
    return jnp.minimum(z, 0.0) - jnp.log1p(jnp.exp(-jnp.abs(z)))


def _sigmoid(z):
    return 1.0 / (1.0 + jnp.exp(-z))


def _gelu(x):
    return 0.5 * x * (1.0 + jnp.tanh(0.7978845608028654 * (x + 0.044715 * x * x * x)))


def _silu(x):
    return x * _sigmoid(x)


def _expm1(y):
    u = jnp.exp(y)
    near = (u - 1.0) * y / jnp.log(u)
    return jnp.where(u == 1.0, y, jnp.where(u < 0.5, u - 1.0, near))


def _mod_kernel(c_ref, w_ref, b_ref, o_ref):
    a = _silu(c_ref[...])
    o_ref[0] = jnp.dot(a, w_ref[0], precision=HIGHEST, preferred_element_type=F32) + b_ref[0]


def _modulation(cond, w_mod, b_mod):
    r = cond.shape[0]
    tc = 1536
    return pl.pallas_call(
        _mod_kernel,
        grid=(DEPTH, 6 * D_MODEL // tc),
        in_specs=[pl.BlockSpec((r, D_MODEL), lambda l, j: (0, 0)),
                  pl.BlockSpec((1, D_MODEL, tc), lambda l, j: (l, 0, j)),
                  pl.BlockSpec((1, 1, tc), lambda l, j: (l, 0, j))],
        out_specs=pl.BlockSpec((1, r, tc), lambda l, j: (l, 0, j)),
        out_shape=jax.ShapeDtypeStruct((DEPTH, r, 6 * D_MODEL), F32),
        compiler_params=_cparams(("parallel", "parallel")),
        name="modulation",
    )(cond, w_mod, b_mod.reshape(DEPTH, 1, 6 * D_MODEL))


def _addpos_kernel(x_ref, p_ref, o_ref):
    o_ref[0] = x_ref[0] + p_ref[...]


def _add_position(x, pos):
    b, t, d = x.shape
    tt = 512
    return pl.pallas_call(
        _addpos_kernel,
        grid=(b, t // tt),
        in_specs=[pl.BlockSpec((1, tt, d), lambda i, j: (i, j, 0)),
                  pl.BlockSpec((tt, d), lambda i, j: (j, 0))],
        out_specs=pl.BlockSpec((1, tt, d), lambda i, j: (i, j, 0)),
        out_shape=jax.ShapeDtypeStruct(x.shape, F32),
        compiler_params=_cparams(("parallel", "parallel")),
        name="add_position",
    )(x, pos)


def _inproj_kernel(x_ref, mod_ref, g_ref, w_ref, b_ref, o_ref, h_ref):
    @pl.when(pl.program_id(1) == 0)
    def _():
        x = x_ref[...]
        inv = lax.rsqrt(jnp.mean(x * x, axis=-1, keepdims=True) + EPS)
        sh = mod_ref[0, :, 0:D_MODEL]
        sc = mod_ref[0, :, D_MODEL:2 * D_MODEL]
        h_ref[...] = (x * inv * g_ref[...] * (1.0 + sc) + sh).astype(BF16)

    acc = jnp.dot(h_ref[...], w_ref[...], preferred_element_type=F32) + b_ref[...]
    o_ref[...] = acc.astype(o_ref.dtype)


def _inproj(x, mod_l, norm_g, w_pack, b_pack, seq_len, layer, tc, out_dtype, name):
    n = x.shape[0]
    width = w_pack.shape[-1]
    tn = 512
    rows = mod_l.shape[0]
    if rows == 1:
        mod_map = lambda i, j: (0, 0, 0)
    else:
        mod_map = lambda i, j: ((i * tn) // seq_len, 0, 0)
    return pl.pallas_call(
        _inproj_kernel,
        grid=(n // tn, width // tc),
        in_specs=[pl.BlockSpec((tn, D_MODEL), lambda i, j: (i, 0)),
                  pl.BlockSpec((1, 1, 6 * D_MODEL), mod_map),
                  pl.BlockSpec((1, D_MODEL), lambda i, j: (0, 0)),
                  pl.BlockSpec((None, D_MODEL, tc), lambda i, j: (layer, 0, j)),
                  pl.BlockSpec((1, tc), lambda i, j: (0, j))],
        out_specs=pl.BlockSpec((tn, tc), lambda i, j: (i, j)),
        out_shape=jax.ShapeDtypeStruct((n, width), out_dtype),
        scratch_shapes=[pltpu.VMEM((tn, D_MODEL), BF16)],
        compiler_params=_cparams(("parallel", "arbitrary")),
        name=name,
    )(x, mod_l, norm_g, w_pack, b_pack)


SCAN_GROUP = 4


def _per_sequence(body, n_group, shared):
    def kern(*refs):
        def run(phase):
            for g in range(n_group):
                body(*[r if i in shared else r.at[g] for i, r in enumerate(refs)], phase=phase)

        pl.when(pl.program_id(1) == 0)(lambda: run('init'))
        run('main')
        pl.when(pl.program_id(1) == pl.num_programs(1) - 1)(lambda: run('emit'))
    return kern


def _cumsum_rows(x, rev):
    k = 1
    while k < x.shape[0]:
        x = x + _shift_rows(x, k, rev, 0.0)
        k *= 2
    return x


def _causal_mask(length, rev):
    row = lax.broadcasted_iota(jnp.int32, (length, length), 0)
    col = lax.broadcasted_iota(jnp.int32, (length, length), 1)
    return (col >= row) if rev else (col <= row)


def _mlstm_kernel(*refs, has_init, emit_state, phase):
    L = CHUNK_A
    ins = list(refs)
    dirs = [ins[0:4], ins[4:8]]
    pos = 8
    if has_init:
        c0_ref, n0_ref, m0_ref = ins[pos:pos + 3]
        pos += 3
    outs = ins[pos:pos + 2]
    pos += 2
    if emit_state:
        cout_ref, nout_ref, mout_ref = ins[pos:pos + 3]
        pos += 3
    cs_ref, ns_ref, ms_ref = ins[pos:pos + 3]

    if phase == 'init':
        if has_init:
            cs_ref[...] = c0_ref[...]
            ns_ref[...] = n0_ref[...]
            ms_ref[...] = m0_ref[...]
        else:
            cs_ref[...] = jnp.zeros_like(cs_ref)
            ns_ref[...] = jnp.zeros_like(ns_ref)
            ms_ref[...] = jnp.zeros_like(ms_ref)
        return
    if phase == 'emit':
        if emit_state:
            cout_ref[...] = cs_ref[...]
            nout_ref[...] = ns_ref[...]
            mout_ref[...] = ms_ref[...]
        return

    for d in range(2):
        rev = d == 1
        q_ref, k_ref, v_ref, s_ref = dirs[d]
        o_ref = outs[d]
        mask = _causal_mask(L, rev)
        sm = s_ref[:, 0:LANE]
        lf = _log_sigmoid(sm)
        bc = jnp.dot(mask.astype(F32), lf, precision=HIGHEST,
                     preferred_element_type=F32)
        sm_t = sm.T
        bc_t = bc.T
        last = 0 if rev else L - 1
        for h in range(N_HEAD):
            ci, cf = d * N_HEAD + h, 2 * N_HEAD + d * N_HEAD + h
            li_c, b_c = sm[:, ci:ci + 1], bc[:, cf:cf + 1]
            li_r, b_r = sm_t[ci:ci + 1, :], bc_t[cf:cf + 1, :]
            m_prev = ms_ref[d, h][:, 0:1]
            logw = jnp.where(mask, b_c - b_r + li_r, NEG_BIG)
            from_state = b_c + m_prev
            m_t = jnp.maximum(from_state, jnp.max(logw, axis=-1, keepdims=True))
            w_state = jnp.exp(from_state - m_t)
            hs = slice(h * DH, (h + 1) * DH)
            q = q_ref[:, hs].astype(F32)
            k = k_ref[:, hs].astype(F32) * (DH ** -0.5)
            v = v_ref[:, hs]
            scores = _bdot_nt(q, k) * jnp.exp(logw - m_t)
            c_st = cs_ref[d, h]
            n_st = ns_ref[d, h]
            num = w_state * _bdot(q, c_st) + _bdot(scores, v)
            den = (w_state * jnp.sum(q * n_st, axis=-1, keepdims=True)
                   + jnp.sum(scores, axis=-1, keepdims=True))
            floor = jnp.exp(jnp.minimum(-m_t, MAX_EXP_ARG))
            o_ref[:, hs] = num / jnp.maximum(jnp.abs(den), floor)
            m_new = m_t[last:last + 1, :]
            b_last = b_c[last:last + 1, :]
            kw = k * jnp.exp(b_last - b_c + li_c - m_new)
            decay = jnp.exp(b_last + m_prev - m_new)
            cs_ref[d, h] = decay * c_st + _bdot_tn(kw, v)
            ns_ref[d, h] = decay * n_st + jnp.sum(kw, axis=0, keepdims=True)
            ms_ref[d, h] = jnp.broadcast_to(m_new, (1, LANE))


def _scan_maps(nc):
    fwd = lambda cb: (lambda b, c: (b, c, cb))
    bwd = lambda cb: (lambda b, c: (b, nc - 1 - c, cb))
    return fwd, bwd


def _mlstm(pw, pf, bsz, seq, state, emit_state):
    L = CHUNK_A
    nc = seq // L
    n = bsz * seq
    g = min(bsz, SCAN_GROUP)
    has_init = state is not None
    pw3 = pw.reshape(bsz, seq, N_WIDE)
    pf3 = pf.reshape(bsz, seq, N_FINE)
    fwd, bwd = _scan_maps(nc)

    in_specs, args = [], []
    for mk in (fwd, bwd):
        for col in (COL_AQ, COL_AK, COL_AV):
            in_specs.append(pl.BlockSpec((g, L, MIX_W), mk(col // MIX_W)))
            args.append(pw3)
        in_specs.append(pl.BlockSpec((g, L, SMALL_W), mk(COL_SMALL // SMALL_W)))
        args.append(pf3)
    st_specs = [pl.BlockSpec((g, 2, N_HEAD, DH, DH), lambda b, c: (b, 0, 0, 0, 0)),
                pl.BlockSpec((g, 2, N_HEAD, 1, DH), lambda b, c: (b, 0, 0, 0, 0)),
                pl.BlockSpec((g, 2, N_HEAD, 1, LANE), lambda b, c: (b, 0, 0, 0, 0))]
    st_shapes = [jax.ShapeDtypeStruct((bsz, 2, N_HEAD, DH, DH), F32),
                 jax.ShapeDtypeStruct((bsz, 2, N_HEAD, 1, DH), F32),
                 jax.ShapeDtypeStruct((bsz, 2, N_HEAD, 1, LANE), F32)]
    if has_init:
        c0, n0, m0 = state
        in_specs += st_specs
        args += [c0, n0.reshape(bsz, 2, N_HEAD, 1, DH),
                 jnp.broadcast_to(m0[..., None, None], (bsz, 2, N_HEAD, 1, LANE))]
    out_specs = [pl.BlockSpec((g, L, MIX_W), fwd(0)), pl.BlockSpec((g, L, MIX_W), bwd(0))]
    out_shape = [jax.ShapeDtypeStruct((bsz, seq, MIX_W), F32)] * 2
    if emit_state:
        out_specs += st_specs
        out_shape += st_shapes
    body = functools.partial(_mlstm_kernel, has_init=has_init, emit_state=emit_state)
    res = pl.pallas_call(
        _per_sequence(body, g, ()),
        grid=(bsz // g, nc),
        in_specs=in_specs,
        out_specs=out_specs,
        out_shape=out_shape,
        scratch_shapes=[pltpu.VMEM((g, 2, N_HEAD, DH, DH), F32),
                        pltpu.VMEM((g, 2, N_HEAD, 1, DH), F32),
                        pltpu.VMEM((g, 2, N_HEAD, 1, LANE), F32)],
        compiler_params=_cparams(("parallel", "arbitrary")),
        name="mlstm_scan",
    )(*args)
    o_f, o_b = res[0].reshape(n, MIX_W), res[1].reshape(n, MIX_W)
    new_state = None
    if emit_state:
        new_state = (res[2], res[3].reshape(bsz, 2, N_HEAD, DH), res[4][:, :, :, 0, 0])
    return o_f, o_b, new_state


def _gla_direction(q, k, la, v, s_ref, d, o_ref, *, rev, dk):
    L, S = CHUNK_G, SUB_G
    nsub = L // S
    mask = _causal_mask(L, rev)
    b = _cumsum_rows(la, rev)
    bx = b - la
    ref_rows = [bx[i * S + (S - 1 if rev else 0):i * S + (S - 1 if rev else 0) + 1, :]
                for i in range(nsub)]
    bref = jnp.concatenate([jnp.broadcast_to(r, (S, r.shape[1])) for r in ref_rows], axis=0)
    q_hat = q * jnp.exp(b - bref)
    rowid = lax.broadcasted_iota(jnp.int32, (L, 1), 0)
    k_hat = []
    for i in range(nsub):
        seen = (rowid >= i * S) if rev else (rowid < (i + 1) * S)
        k_hat.append(k * jnp.exp(jnp.where(seen, ref_rows[i] - b, NEG_BIG)))
    last = 0 if rev else L - 1
    b_last = b[last:last + 1, :]
    q_state = q * jnp.exp(b)
    k_state = k * jnp.exp(b_last - b)
    e_last = jnp.exp(b_last)
    for h in range(N_HEAD):
        ks = slice(h * dk, (h + 1) * dk)
        vs = slice(h * DH, (h + 1) * DH)
        vh = v[:, vs]
        blocks = [_bdot_nt(q_hat[i * S:(i + 1) * S, ks], k_hat[i][:, ks]) for i in range(nsub)]
        scores = jnp.where(mask, jnp.concatenate(blocks, axis=0), 0.0)
        s_t = s_ref[d, h]
        o_ref[:, vs] = _bdot_nt(q_state[:, ks], s_t) + _bdot(scores, vh)
        s_ref[d, h] = s_t * e_last[:, ks] + _bdot_tn(vh, k_state[:, ks])


def _scan_state_io(ins, pos, has_init, emit_state, n_out):
    s0_ref = sout_ref = None
    if has_init:
        s0_ref = ins[pos]
        pos += 1
    outs = ins[pos:pos + n_out]
    pos += n_out
    if emit_state:
        sout_ref = ins[pos]
        pos += 1
    return s0_ref, outs, sout_ref, ins[pos]


def _scan_state_phase(phase, s_ref, s0_ref, sout_ref):
    if phase == 'init':
        if s0_ref is not None:
            s_ref[...] = s0_ref[...]
        else:
            s_ref[...] = jnp.zeros_like(s_ref)
    elif phase == 'emit' and sout_ref is not None:
        sout_ref[...] = s_ref[...]
    return phase != 'main'


def _hgrn_kernel(*refs, has_init, emit_state, phase):
    ins = list(refs)
    lb_ref = ins[6]
    s0_ref, outs, sout_ref, s_ref = _scan_state_io(ins, 7, has_init, emit_state, 2)
    if _scan_state_phase(phase, s_ref, s0_ref, sout_ref):
        return
    for d in range(2):
        q_ref, z_ref, v_ref = ins[3 * d:3 * d + 3]
        z = z_ref[...]
        lb = lb_ref[d:d + 1, :]
        la = _log_sigmoid(z) + jnp.log1p(lb * jnp.exp(jnp.minimum(-z, MAX_EXP_ARG)))
        k = (1.0 - lb) * _sigmoid(-z)
        q = q_ref[...].astype(F32) * (DH ** -0.5)
        _gla_direction(q, k, la, v_ref[...], s_ref, d, outs[d], rev=d == 1, dk=DH)


def _gla_kernel(*refs, has_init, emit_state, phase):
    ins = list(refs)
    up_ref, upb_ref = ins[8], ins[9]
    s0_ref, outs, sout_ref, s_ref = _scan_state_io(ins, 10, has_init, emit_state, 2)
    if _scan_state_phase(phase, s_ref, s0_ref, sout_ref):
        return
    for d in range(2):
        q_ref, k_ref, v_ref, sm_ref = ins[4 * d:4 * d + 4]
        zg = jnp.dot(sm_ref[...], up_ref[d], precision=HIGHEST,
                     preferred_element_type=F32) + upb_ref[d:d + 1, :]
        la = _log_sigmoid(zg) * (1.0 / GLA_TAU)
        q = q_ref[...].astype(F32) * (DK_C ** -0.5)
        k = k_ref[...].astype(F32)
        _gla_direction(q, k, la, v_ref[...], s_ref, d, outs[d], rev=d == 1, dk=DK_C)


def _gated_scan(kernel, p, bsz, seq, cols, extra, extra_specs, dk, state_t, emit_state, name):
    L = CHUNK_G
    nc = seq // L
    n = bsz * seq
    g = min(bsz, SCAN_GROUP)
    has_init = state_t is not None
    src = {'wide': p[0].reshape(bsz, seq, N_WIDE), 'fine': p[1].reshape(bsz, seq, N_FINE)}
    fwd, bwd = _scan_maps(nc)

    in_specs, args = [], []
    for d, mk in enumerate((fwd, bwd)):
        for which, off, width in cols[d]:
            in_specs.append(pl.BlockSpec((g, L, width), mk(off // width)))
            args.append(src[which])
    shared = tuple(range(len(args), len(args) + len(extra)))
    in_specs += extra_specs
    args += extra
    st_spec = pl.BlockSpec((g, 2, N_HEAD, DH, dk), lambda b, c: (b, 0, 0, 0, 0))
    if has_init:
        in_specs.append(st_spec)
        args.append(state_t)
    out_specs = [pl.BlockSpec((g, L, MIX_W), fwd(0)), pl.BlockSpec((g, L, MIX_W), bwd(0))]
    out_shape = [jax.ShapeDtypeStruct((bsz, seq, MIX_W), F32)] * 2
    if emit_state:
        out_specs.append(st_spec)
        out_shape.append(jax.ShapeDtypeStruct((bsz, 2, N_HEAD, DH, dk), F32))
    body = functools.partial(kernel, has_init=has_init, emit_state=emit_state)
    res = pl.pallas_call(
        _per_sequence(body, g, shared),
        grid=(bsz // g, nc),
        in_specs=in_specs,
        out_specs=out_specs,
        out_shape=out_shape,
        scratch_shapes=[pltpu.VMEM((g, 2, N_HEAD, DH, dk), F32)],
        compiler_params=_cparams(("parallel", "arbitrary")),
        name=name,
    )(*args)
    return res[0].reshape(n, MIX_W), res[1].reshape(n, MIX_W), (res[2] if emit_state else None)


def _hgrn(p, bsz, seq, lb, state_t, emit_state):
    cols = [[('wide', COL_BQ, MIX_W), ('fine', COL_BF + d * MIX_W, MIX_W), ('wide', COL_BI, MIX_W)]
            for d in range(2)]
    return _gated_scan(_hgrn_kernel, p, bsz, seq, cols, [lb],
                       [pl.BlockSpec((2, MIX_W), lambda b, c: (0, 0))], DH, state_t, emit_state,
                       "hgrn2_scan")


def _gla(p, bsz, seq, up_pad, up_b, state_t, emit_state):
    kw = N_HEAD * DK_C
    cols = [[('wide', COL_CQ, kw), ('wide', COL_CK, kw), ('wide', COL_CV, MIX_W),
             ('fine', COL_SMALL, SMALL_W)] for _ in range(2)]
    return _gated_scan(_gla_kernel, p, bsz, seq, cols, [up_pad, up_b],
                       [pl.BlockSpec((2, SMALL_W, kw), lambda b, c: (0, 0, 0)),
                        pl.BlockSpec((2, kw), lambda b, c: (0, 0))], DK_C, state_t, emit_state,
                       "gla_scan")


def _shift_rows(x, k, rev, fill):
    n = x.shape[0]
    rowid = lax.broadcasted_iota(jnp.int32, x.shape, 0)
    if rev:
        return jnp.where(rowid >= n - k, fill, pltpu.roll(x, n - k, 0))
    return jnp.where(rowid < k, fill, pltpu.roll(x, k, 0))


def _lru_kernel(*refs, seq, has_init, emit_state):
    L = CHUNK_D
    nc = seq // L
    ins = list(refs)
    (dx_ref, dg_ref, cw_ref, cb_ref, wah_ref, wal_ref, ba_ref, wxh_ref, wxl_ref, bx_ref,
     lam_ref) = ins[0:11]
    pos = 11
    h0_ref = hout_ref = None
    if has_init:
        h0_ref = ins[pos]
        pos += 1
    y_ref = ins[pos]
    pos += 1
    if emit_state:
        hout_ref = ins[pos]
        pos += 1
    pad_ref, hb_ref = ins[pos], ins[pos + 1]

    zeros8 = jnp.zeros((8, LANE), F32)
    pad_ref[0:8, :] = zeros8
    pad_ref[8:8 + seq, :] = dx_ref[...]
    pad_ref[8 + seq:16 + seq, :] = zeros8
    lam = lam_ref[...]
    sp = jnp.maximum(-lam, 0.0) + jnp.log1p(jnp.exp(-jnp.abs(lam)))
    cw = cw_ref[...]
    cb = cb_ref[...]

    def chunk_scan(c, carry, d):
        rev = d == 1
        start = pl.multiple_of(c * L, L)
        win = pad_ref[pl.ds(start, L + 16), :]
        xd = cb + sum(cw[j:j + 1, :] * win[7 + j:7 + j + L, :] for j in range(CONV_W))
        r = _sigmoid(_dot3(xd, wah_ref[d, 0], wal_ref[d, 0]) + ba_ref[d:d + 1, :])
        ig = _sigmoid(_dot3(xd, wxh_ref[d, 0], wxl_ref[d, 0]) + bx_ref[d:d + 1, :])
        log_a = -LRU_C * r * sp[d:d + 1, :]
        a = jnp.exp(log_a)
        u = jnp.sqrt(jnp.maximum(-_expm1(2.0 * log_a), 0.0)) * (ig * xd)
        k = 1
        while k < L:
            u = a * _shift_rows(u, k, rev, 0.0) + u
            a = a * _shift_rows(a, k, rev, 1.0)
            k *= 2
        h = a * carry + u
        last = 0 if rev else L - 1
        return start, h, h[last:last + 1, :]

    def scan_body(i, carry):
        start_f, h_f, new_f = chunk_scan(i, carry[0], 0)
        start_b, h_b, new_b = chunk_scan(nc - 1 - i, carry[1], 1)
        y_ref[pl.ds(start_f, L), :] = h_f
        hb_ref[pl.ds(start_b, L), :] = h_b
        return new_f, new_b

    def gate_body(c, carry):
        rows = pl.ds(pl.multiple_of(c * L, L), L)
        y_ref[rows, :] = (y_ref[rows, :] + hb_ref[rows, :]) * _gelu(dg_ref[rows, :].astype(F32))
        return carry

    if has_init:
        init = (h0_ref[0, 0:1, :], h0_ref[0, 1:2, :])
    else:
        init = (jnp.zeros((1, LANE), F32), jnp.zeros((1, LANE), F32))
    fin_f, fin_b = lax.fori_loop(0, nc, scan_body, init)
    lax.fori_loop(0, nc, gate_body, 0)
    if emit_state:
        hout_ref[0, 0:1, :] = fin_f
        hout_ref[0, 1:2, :] = fin_b


def _lru(pw, pf, bsz, seq, conv_w, conv_b, wa, ba, wx, bx, lam, h0, emit_state):
    n = bsz * seq
    has_init = h0 is not None
    wah, wal = _split(wa)
    wxh, wxl = _split(wx)
    gate_w = pl.BlockSpec((2, 1, DH, DH), lambda b, h: (0, h, 0, 0))
    vec2 = pl.BlockSpec((2, LANE), lambda b, h: (0, h))
    in_specs = [pl.BlockSpec((seq, LANE), lambda b, h: (b, COL_DX // LANE + h)),
                pl.BlockSpec((seq, LANE), lambda b, h: (b, COL_DG // LANE + h)),
                pl.BlockSpec((CONV_W, LANE), lambda b, h: (0, h)),
                pl.BlockSpec((1, LANE), lambda b, h: (0, h)),
                gate_w, gate_w, vec2, gate_w, gate_w, vec2, vec2]
    args = [pf, pw, conv_w, conv_b.reshape(1, MIX_W), wah, wal, ba, wxh, wxl, bx, lam]
    st_spec = pl.BlockSpec((1, 2, LANE), lambda b, h: (b, 0, h))
    if has_init:
        in_specs.append(st_spec)
        args.append(h0)
    out_specs = [pl.BlockSpec((seq, LANE), lambda b, h: (b, h))]
    out_shape = [jax.ShapeDtypeStruct((n, MIX_W), F32)]
    if emit_state:
        out_specs.append(st_spec)
        out_shape.append(jax.ShapeDtypeStruct((bsz, 2, MIX_W), F32))
    res = pl.pallas_call(
        functools.partial(_lru_kernel, seq=seq, has_init=has_init, emit_state=emit_state),
        grid=(bsz, N_HEAD),
        in_specs=in_specs,
        out_specs=out_specs,
        out_shape=out_shape,
        scratch_shapes=[pltpu.VMEM((seq + 16, LANE), F32), pltpu.VMEM((seq, LANE), F32)],
        compiler_params=_cparams(("parallel", "parallel")),
        name="conv_rglru",
    )(*args)
    return res[0], (res[1] if emit_state else None)


def _head_rms(x):
    parts = []
    for h in range(N_HEAD):
        xh = x[:, h * DH:(h + 1) * DH]
        parts.append(xh * lax.rsqrt(jnp.mean(xh * xh, axis=-1, keepdims=True) + EPS))
    return jnp.concatenate(parts, axis=-1)


def _merge_kernel(af_ref, ab_ref, bf_ref, bb_ref, cf_ref, cb_ref, yd_ref, ao_ref, bg_ref, cg_ref,
                  gt0_ref, gt1_ref, gt2_ref, gt3_ref, x_ref, mod_ref, g2_ref, wbr_ref, wout_ref,
                  xo_ref, h2_ref, h2b_ref):
    ya = _sigmoid(ao_ref[...].astype(F32)) * _head_rms(af_ref[...] + ab_ref[...])
    yb = _silu(bg_ref[...].astype(F32)) * _head_rms(bf_ref[...] + bb_ref[...])
    yc = _silu(cg_ref[...].astype(F32)) * _head_rms(cf_ref[...] + cb_ref[...])
    merged = None
    gate_refs = (gt0_ref, gt1_ref, gt2_ref, gt3_ref)
    for i, y in enumerate((ya, yb, yc, yd_ref[...])):
        proj = jnp.dot(y.astype(BF16), wbr_ref[i], preferred_element_type=F32)
        term = _sigmoid(gate_refs[i][...].astype(F32)) * proj
        merged = term if merged is None else merged + term
    out = jnp.dot(merged.astype(BF16), wout_ref[...], preferred_element_type=F32)
    g1 = mod_ref[0, :, 2 * D_MODEL:3 * D_MODEL]
    sh2 = mod_ref[0, :, 3 * D_MODEL:4 * D_MODEL]
    sc2 = mod_ref[0, :, 4 * D_MODEL:5 * D_MODEL]
    x = x_ref[...] + g1 * out
    xo_ref[...] = x
    inv = lax.rsqrt(jnp.mean(x * x, axis=-1, keepdims=True) + EPS)
    h2 = x * inv * g2_ref[...] * (1.0 + sc2) + sh2
    h2_ref[...] = h2
    h2b_ref[...] = h2.astype(BF16)


def _merge(mix_outs, yd, p, x, mod_l, norm2_g, w_branch, w_out, seq_len, layer):
    n = x.shape[0]
    tn = 256
    rows = mod_l.shape[0]
    mod_map = (lambda i: (0, 0, 0)) if rows == 1 else (lambda i: ((i * tn) // seq_len, 0, 0))
    tok = lambda cb: (lambda i: (i, cb))
    in_specs = [pl.BlockSpec((tn, MIX_W), tok(0))] * 7
    in_specs += [pl.BlockSpec((tn, MIX_W), tok(COL_AO // MIX_W)),
                 pl.BlockSpec((tn, MIX_W), tok(COL_BG // MIX_W)),
                 pl.BlockSpec((tn, MIX_W), tok(COL_CG // MIX_W)),
                 *[pl.BlockSpec((tn, D_MODEL), tok(COL_GATES // D_MODEL + i)) for i in range(N_BRANCH)],
                 pl.BlockSpec((tn, D_MODEL), tok(0)),
                 pl.BlockSpec((1, 1, 6 * D_MODEL), mod_map),
                 pl.BlockSpec((1, D_MODEL), lambda i: (0, 0)),
                 pl.BlockSpec((None, N_BRANCH, MIX_W, D_MODEL), lambda i: (layer, 0, 0, 0)),
                 pl.BlockSpec((None, D_MODEL, D_MODEL), lambda i: (layer, 0, 0))]
    out_spec = pl.BlockSpec((tn, D_MODEL), tok(0))
    return pl.pallas_call(
        _merge_kernel,
        grid=(n // tn,),
        in_specs=in_specs,
        out_specs=[out_spec, out_spec, out_spec],
        out_shape=[jax.ShapeDtypeStruct((n, D_MODEL), F32), jax.ShapeDtypeStruct((n, D_MODEL), F32),
                   jax.ShapeDtypeStruct((n, D_MODEL), BF16)],
        compiler_params=_cparams(("parallel",)),
        name="branch_merge",
    )(*mix_outs, yd, p, p, p, p, p, p, p, x, mod_l, norm2_g, w_branch, w_out)


ROUTE_TN = 256
ROUTE_LG = LANE


def _top16_exact(vals):
    n_rows = vals.shape[0]
    rowid = lax.broadcasted_iota(jnp.int32, vals.shape, 0).astype(F32)
    rank = jnp.full(vals.shape, float(PEER_TOPK), F32)
    tops = []
    for r in range(PEER_TOPK):
        m = jnp.max(vals, axis=0, keepdims=True)
        idx = jnp.min(jnp.where(vals == m, rowid, float(n_rows)), axis=0, keepdims=True)
        sel = rowid == idx
        rank = jnp.where(sel, float(r), rank)
        vals = jnp.where(sel, -jnp.inf, vals)
        tops.append(m)
    return jnp.concatenate(tops, axis=0), rank


_TAKEN_BASE = -3.0e38
_TAKEN_STEP = 2.0e36
_TAKEN_BELOW = _TAKEN_BASE + 0.5 * _TAKEN_STEP


def _top16_quick(vals):
    work = vals
    tops = []
    for r in range(PEER_TOPK):
        m = jnp.max(work, axis=0, keepdims=True)
        work = jnp.where(work == m, _TAKEN_BASE - r * _TAKEN_STEP, work)
        tops.append(m)
    taken = work <= _TAKEN_BELOW
    rank = jnp.where(taken, jnp.round((_TAKEN_BASE - work) * (1.0 / _TAKEN_STEP)), float(PEER_TOPK))
    n_taken = jnp.sum(jnp.where(taken, 1.0, 0.0), axis=0, keepdims=True)
    return jnp.concatenate(tops, axis=0), rank, n_taken


_CAND_PIECES = (('a', 0, 0, 16), ('a', 0, 8, 16), ('a', 1, 0, 8), ('a', 2, 0, 5), ('a', 3, 0, 4),
                ('b', 0, 0, (4, 8)), ('b', 0, 8, (8, 16)), ('b', 1, 0, (4, 8)), ('b', 2, 0, (4, 5)))


def _pair_pieces(x1, x2, combine):
    out = []
    for kind, fixed, off, _ in _CAND_PIECES:
        if kind == 'a':
            out.append(combine(x1[fixed:fixed + 1, :], x2[off:off + 8, :]))
        else:
            out.append(combine(x1[off:off + 8, :], x2[fixed:fixed + 1, :]))
    return out


def _candidate_sums(s1, s2):
    K = PEER_TOPK
    n = s1.shape[1]
    iota8 = lax.broadcasted_iota(jnp.int32, (8, n), 0)
    sums = _pair_pieces(s1, s2, lambda x, y: x + y)
    vals, poss = [], []
    for (kind, fixed, off, lim), sm in zip(_CAND_PIECES, sums):
        idx = iota8 + off
        if kind == 'a':
            valid = idx < lim
            pos = fixed * K + idx
        else:
            valid = (idx >= lim[0]) & (idx < lim[1])
            pos = idx * K + fixed
        vals.append(jnp.where(valid, sm, -jnp.inf))
        poss.append(jnp.where(valid, pos, K * K).astype(F32))
    return jnp.concatenate(vals, axis=0), jnp.concatenate(poss, axis=0)


def _choose_exact(vals, posid):
    K = PEER_TOPK
    chosen = jnp.zeros(vals.shape, F32)
    for _ in range(K):
        m = jnp.max(vals, axis=0, keepdims=True)
        idx = jnp.min(jnp.where(vals == m, posid, float(K * K)), axis=0, keepdims=True)
        sel = posid == idx
        chosen = jnp.where(sel, 1.0, chosen)
        vals = jnp.where(sel, -jnp.inf, vals)
    return chosen


def _choose_quick(vals):
    for _ in range(PEER_TOPK):
        m = jnp.max(vals, axis=0, keepdims=True)
        vals = jnp.where(vals == m, _TAKEN_BASE, vals)
    chosen = jnp.where(vals == _TAKEN_BASE, 1.0, 0.0)
    return chosen, jnp.sum(chosen, axis=0, keepdims=True)


def _route_kernel(h_ref, wqh_ref, wql_ref, kh_ref, kl_ref, a_ref, b_ref, r2_ref, c_ref,
                  qh_ref, ql_ref, s_ref, top_ref, rank_ref, ch_ref):
    K = PEER_TOPK
    n_lg = ROUTE_TN // ROUTE_LG
    q = _dot3(h_ref[...], wqh_ref[...], wql_ref[...])
    q_hi, q_lo = _split(q)
    for i in range(2 * PEER_HEADS):
        qh_ref[i] = q_hi[:, i * N_KEYS:(i + 1) * N_KEYS]
        ql_ref[i] = q_lo[:, i * N_KEYS:(i + 1) * N_KEYS]
    nt = (((1,), (1,)), ((), ()))
    lanes = [slice(lg * ROUTE_LG, (lg + 1) * ROUTE_LG) for lg in range(n_lg)]

    def miscount(counts):
        worst = None
        for cnt in counts:
            dev = jnp.abs(cnt - float(K))
            worst = dev if worst is None else jnp.maximum(worst, dev)
        return jnp.max(worst) > 0.0

    def head_body(h, carry):
        for half in range(2):
            kh, kl = kh_ref[h, half], kl_ref[h, half]
            qh, ql = qh_ref[2 * h + half], ql_ref[2 * h + half]
            s_ref[half] = (lax.dot_general(kh, qh, nt, preferred_element_type=F32)
                           + lax.dot_general(kl, qh, nt, preferred_element_type=F32)
                           + lax.dot_general(kh, ql, nt, preferred_element_type=F32))

        counts = []
        for half in range(2):
            for lg in range(n_lg):
                top, rank, n_taken = _top16_quick(s_ref[half, :, lanes[lg]])
                top_ref[half, lg] = top
                rank_ref[half, lg] = rank
                counts.append(n_taken)

        @pl.when(miscount(counts))
        def _():
            for half in range(2):
                for lg in range(n_lg):
                    top, rank = _top16_exact(s_ref[half, :, lanes[lg]])
                    top_ref[half, lg] = top
                    rank_ref[half, lg] = rank

        counts = []
        for lg in range(n_lg):
            vals, _ = _candidate_sums(top_ref[0, lg], top_ref[1, lg])
            chosen, n_taken = _choose_quick(vals)
            ch_ref[lg] = chosen
            counts.append(n_taken)

        @pl.when(miscount(counts))
        def _():
            for lg in range(n_lg):
                vals, posid = _candidate_sums(top_ref[0, lg], top_ref[1, lg])
                ch_ref[lg] = _choose_exact(vals, posid)

        for lg in range(n_lg):
            ls = lanes[lg]
            st1, st2 = s_ref[0, :, ls], s_ref[1, :, ls]
            s1, s2 = top_ref[0, lg], top_ref[1, lg]
            rank1, rank2 = rank_ref[0, lg], rank_ref[1, lg]
            chosen = ch_ref[lg]
            ch = [chosen[8 * i:8 * i + 8, :] for i in range(len(_CAND_PIECES))]
            e1 = jnp.exp(s1 - s1[0:1, :])
            e2 = jnp.exp(s2 - s2[0:1, :])
            pair = _pair_pieces(e1, e2, lambda x, y: x * y)
            z = sum(jnp.sum(c * p, axis=0, keepdims=True) for c, p in zip(ch, pair))
            iota8 = lax.broadcasted_iota(jnp.int32, (8, ROUTE_LG), 0)
            low = ch[5] + ch[7] + ch[8]
            for a, cnt in ((3, ch[4]), (2, ch[3]), (1, ch[2]), (0, ch[0] + ch[1])):
                low = jnp.where(iota8 == a, jnp.sum(cnt, axis=0, keepdims=True), low)
            counts = jnp.concatenate([low, ch[6]], axis=0)
            c_dense = jnp.zeros((N_KEYS, ROUTE_LG), F32)
            for a in range(K):
                c_dense = jnp.where(rank1 == float(a), counts[a:a + 1, :], c_dense)
            a_ref[h, :, ls] = jnp.where(rank1 < float(K), jnp.exp(st1 - s1[0:1, :]), 0.0) / z
            b_ref[h, :, ls] = jnp.where(rank2 < float(K), jnp.exp(st2 - s2[0:1, :]), 0.0).astype(BF16)
            r2_ref[h, :, ls] = rank2.astype(BF16)
            c_ref[h, :, ls] = c_dense
        return carry

    lax.fori_loop(0, PEER_HEADS, head_body, 0)


def _route(h2, wq_hi, wq_lo, keys_hi, keys_lo, layer):
    n = h2.shape[0]
    tn = ROUTE_TN
    dense = pl.BlockSpec((PEER_HEADS, N_KEYS, tn), lambda i: (0, 0, i))
    wspec = pl.BlockSpec((None, D_MODEL, PEER_HEADS * 2 * N_KEYS), lambda i: (layer, 0, 0))
    kspec = pl.BlockSpec((None, PEER_HEADS, 2, N_KEYS, N_KEYS), lambda i: (layer, 0, 0, 0, 0))
    return pl.pallas_call(
        _route_kernel,
        grid=(n // tn,),
        in_specs=[pl.BlockSpec((tn, D_MODEL), lambda i: (i, 0)), wspec, wspec, kspec, kspec],
        out_specs=[dense] * 4,
        out_shape=[jax.ShapeDtypeStruct((PEER_HEADS, N_KEYS, n), dt) for dt in (F32, BF16, BF16, F32)],
        scratch_shapes=[pltpu.VMEM((2 * PEER_HEADS, tn, N_KEYS), BF16),
                        pltpu.VMEM((2 * PEER_HEADS, tn, N_KEYS), BF16),
                        pltpu.VMEM((2, N_KEYS, tn), F32),
                        pltpu.VMEM((2, tn // ROUTE_LG, PEER_TOPK, ROUTE_LG), F32),
                        pltpu.VMEM((2, tn // ROUTE_LG, N_KEYS, ROUTE_LG), F32),
                        pltpu.VMEM((tn // ROUTE_LG, 8 * len(_CAND_PIECES), ROUTE_LG), F32)],
        compiler_params=_cparams(("parallel",)),
        name="peer_route",
    )(h2, wq_hi, wq_lo, keys_hi, keys_lo)


PEER_TN = 512
PEER_TE = 1024


PEER_SPLIT = 1


def _gelu_bf16(t):
    x = t.astype(BF16)
    y2 = x * (1.0 + 0.044715 * (x * x)) * (-2.0 * 0.7978845608028654)
    return x / (1.0 + jnp.exp(y2))


BF16_ROWS = 16


def _row_to_packed(row):
    tile = jnp.broadcast_to(row, (BF16_ROWS, row.shape[1])).astype(BF16)
    return jnp.concatenate([tile] * (N_KEYS // BF16_ROWS), axis=0)


def _expert_kernel(hb_ref, u_ref, vt_ref, a_ref, b_ref, r2_ref, c_ref, x_ref, mod_ref, o_ref,
                   acc_ref):
    e = pl.program_id(1)
    rows_per_tile = PEER_TE // N_KEYS
    rows_per_part = rows_per_tile // PEER_SPLIT
    part_w = PEER_TE // PEER_SPLIT

    @pl.when(e == 0)
    def _():
        acc_ref[...] = jnp.zeros_like(acc_ref)

    update = None
    for part in range(PEER_SPLIT):
        t_t = lax.dot_general(u_ref[part * part_w:(part + 1) * part_w, :], hb_ref[...],
                              (((1,), (1,)), ((), ())), preferred_element_type=F32)
        w_parts = []
        for jj in range(rows_per_part):
            j = part * rows_per_part + jj
            e1 = e * rows_per_tile + j
            g = None
            for h in range(PEER_HEADS):
                a_row = _row_to_packed(a_ref[h, pl.ds(e1, 1), :])
                c_row = _row_to_packed(c_ref[h, pl.ds(e1, 1), :])
                term = a_row * jnp.where(r2_ref[h] < c_row, b_ref[h], 0.0)
                g = term if g is None else g + term
            w_parts.append(g * _gelu_bf16(t_t[jj * N_KEYS:(jj + 1) * N_KEYS, :]))
        w_t = jnp.concatenate(w_parts, axis=0)
        upd = jnp.dot(vt_ref[:, part * part_w:(part + 1) * part_w], w_t, preferred_element_type=F32)
        update = upd if update is None else update + upd
    acc_ref[...] += update

    @pl.when(e == pl.num_programs(1) - 1)
    def _():
        g2 = mod_ref[0, :, 5 * D_MODEL:6 * D_MODEL]
        o_ref[...] = x_ref[...] + g2 * acc_ref[...].T


def _experts(h2b, u_b, vt_b, dense, x, mod_l, seq_len, layer):
    n = x.shape[0]
    tn, te = PEER_TN, PEER_TE
    rows = mod_l.shape[0]
    mod_map = (lambda i, e: (0, 0, 0)) if rows == 1 else (lambda i, e: ((i * tn) // seq_len, 0, 0))
    dspec = pl.BlockSpec((PEER_HEADS, N_KEYS, tn), lambda i, e: (0, 0, i))
    return pl.pallas_call(
        _expert_kernel,
        grid=(n // tn, N_EXPERTS // te),
        in_specs=[pl.BlockSpec((tn, D_MODEL), lambda i, e: (i, 0)),
                  pl.BlockSpec((None, te, D_MODEL), lambda i, e: (layer, e, 0)),
                  pl.BlockSpec((None, D_MODEL, te), lambda i, e: (layer, 0, e)),
                  dspec, dspec, dspec, dspec,
                  pl.BlockSpec((tn, D_MODEL), lambda i, e: (i, 0)),
                  pl.BlockSpec((1, 1, 6 * D_MODEL), mod_map)],
        out_specs=pl.BlockSpec((tn, D_MODEL), lambda i, e: (i, 0)),
        out_shape=jax.ShapeDtypeStruct((n, D_MODEL), F32),
        scratch_shapes=[pltpu.VMEM((D_MODEL, tn), F32)],
        compiler_params=_cparams(("parallel", "arbitrary")),
        name="peer_experts",
    )(h2b, u_b, vt_b, *dense, x, mod_l)


def _final_norm_kernel(x_ref, g_ref, o_ref):
    x = x_ref[...]
    o_ref[...] = x * lax.rsqrt(jnp.mean(x * x, axis=-1, keepdims=True) + EPS) * g_ref[...]


def _final_norm(x, g):
    n = x.shape[0]
    tn = 512
    return pl.pallas_call(
        _final_norm_kernel,
        grid=(n // tn,),
        in_specs=[pl.BlockSpec((tn, D_MODEL), lambda i: (i, 0)),
                  pl.BlockSpec((1, D_MODEL), lambda i: (0, 0))],
        out_specs=pl.BlockSpec((tn, D_MODEL), lambda i: (i, 0)),
        out_shape=jax.ShapeDtypeStruct((n, D_MODEL), F32),
        compiler_params=_cparams(("parallel",)),
        name="final_norm",
    )(x, g)


def _pack_columns(a, src, width):
    parts = [a[..., lo:hi] for lo, hi in src]
    used = sum(hi - lo for lo, hi in src)
    if used < width:
        parts.append(jnp.zeros(a.shape[:-1] + (width - used,), a.dtype))
    return jnp.concatenate(parts, axis=-1)


def _position_code(rows):
    quarter = D_MODEL // 4
    omega = 1.0 / (POS_BASE ** (jnp.arange(quarter, dtype=F32) / quarter))
    r, col = jnp.meshgrid(jnp.arange(rows, dtype=F32), jnp.arange(GRID_W, dtype=F32), indexing='ij')

    def enc(pos):
        ang = pos.reshape(-1, 1) * omega
        return jnp.concatenate([jnp.sin(ang), jnp.cos(ang)], axis=-1)
    return jnp.concatenate([enc(r), enc(col)], axis=-1)


def _layer(x, bsz, seq, mod_l, lp, state, emit_state):
    layer = lp['layer']
    pw = _inproj(x, mod_l, lp['norm1_g'], lp['w_wide'], lp['b_wide'], seq, layer, 2432, BF16,
                 "in_projection_wide")
    pf = _inproj(x, mod_l, lp['norm1_g'], lp['w_fine'], lp['b_fine'], seq, layer, 896, F32,
                 "in_projection_fine")
    if state is None:
        st_a = st_b = st_c = st_d = None
    else:
        c0, n0, m0, sb0, sc0, hd0 = state
        st_a = (c0, n0, m0)
        st_b = jnp.swapaxes(sb0, -1, -2)
        st_c = jnp.swapaxes(sc0, -1, -2)
        st_d = hd0
    a_f, a_b, new_a = _mlstm(pw, pf, bsz, seq, st_a, emit_state)
    b_f, b_b, new_b = _hgrn((pw, pf), bsz, seq, lp['hgrn_lb'], st_b, emit_state)
    c_f, c_b, new_c = _gla((pw, pf), bsz, seq, lp['gla_up_pad'], lp['gla_up_b'], st_c, emit_state)
    yd, new_d = _lru(pw, pf, bsz, seq, lp['conv_w'], lp['conv_b'], lp['lru_w_a'], lp['lru_b_a'],
                     lp['lru_w_x'], lp['lru_b_x'], lp['lru_lambda'], st_d, emit_state)
    x1, h2, h2b = _merge((a_f, a_b, b_f, b_b, c_f, c_b), yd, pw, x, mod_l, lp['norm2_g'],
                         lp['w_branch'], lp['w_out'], seq, layer)
    dense = _route(h2, lp['wq_hi'], lp['wq_lo'], lp['keys_hi'], lp['keys_lo'], layer)
    x2 = _experts(h2b, lp['peer_u'], lp['peer_vt'], dense, x1, mod_l, seq, layer)
    new_state = None
    if emit_state:
        new_state = (*new_a, jnp.swapaxes(new_b, -1, -2), jnp.swapaxes(new_c, -1, -2), new_d)
    return x2, new_state


def kernel(x_prompt, x_sample, state_mlstm_C, state_mlstm_n, state_mlstm_m, state_hgrn_S,
           state_gla_S, state_lru_h, c, c_ctx, norm1_g, norm2_g, final_norm_g, w_mod, b_mod,
           w_in, b_in, w_gla_up, b_gla_up, hgrn_lower_bounds, conv_w, conv_b, lru_w_a, lru_b_a,
           lru_w_x, lru_b_x, lru_lambda, w_branch, w_out, peer_w_q, peer_sub_keys, peer_u, peer_v):
    lb_soft = jax.nn.softmax(hgrn_lower_bounds.astype(F32), axis=0)
    lb_all = jnp.cumsum(lb_soft, axis=0) - lb_soft[0:1]
    w_wide = _pack_columns(w_in, _WIDE_SRC, N_WIDE).astype(BF16)
    w_fine = _pack_columns(w_in, _FINE_SRC, N_FINE).astype(BF16)
    b_wide = _pack_columns(b_in, _WIDE_SRC, N_WIDE).reshape(DEPTH, 1, N_WIDE)
    b_fine = _pack_columns(b_in, _FINE_SRC, N_FINE).reshape(DEPTH, 1, N_FINE)
    kw = N_HEAD * DK_C
    up_pad = jnp.zeros((DEPTH, 2, SMALL_W, kw), F32)
    for d in range(2):
        lo = 2 * N_HEAD * 2 + d * R_C
        up_pad = up_pad.at[:, d, lo:lo + R_C, :].set(w_gla_up[:, d].astype(F32))
    wq_hi, wq_lo = _split(peer_w_q)
    keys_hi, keys_lo = _split(peer_sub_keys)
    u_b = peer_u.astype(BF16)
    vt_b = jnp.swapaxes(peer_v, 1, 2).astype(BF16)
    wbr_b = w_branch.astype(BF16)
    wout_b = w_out.astype(BF16)

    def layer_params(l):
        return {
            'norm1_g': norm1_g[l].reshape(1, D_MODEL), 'norm2_g': norm2_g[l].reshape(1, D_MODEL),
            'layer': l, 'w_wide': w_wide, 'b_wide': b_wide[l], 'w_fine': w_fine, 'b_fine': b_fine[l],
            'hgrn_lb': lb_all[l],
            'gla_up_pad': up_pad[l], 'gla_up_b': b_gla_up[l],
            'conv_w': conv_w[l], 'conv_b': conv_b[l], 'lru_w_a': lru_w_a[l], 'lru_b_a': lru_b_a[l],
            'lru_w_x': lru_w_x[l], 'lru_b_x': lru_b_x[l], 'lru_lambda': lru_lambda[l],
            'w_branch': wbr_b, 'w_out': wout_b, 'wq_hi': wq_hi, 'wq_lo': wq_lo,
            'keys_hi': keys_hi, 'keys_lo': keys_lo, 'peer_u': u_b, 'peer_vt': vt_b,
        }

    cond = jnp.concatenate([c, c_ctx[None, :]], axis=0).astype(F32)
    n_dec = c.shape[0]
    mod = _modulation(cond, w_mod, b_mod)
    final_g = final_norm_g.reshape(1, D_MODEL)

    bp, tp, _ = x_prompt.shape
    x = x_prompt.reshape(bp * tp, D_MODEL)
    ctx_states = []
    for l in range(DEPTH):
        mod_l = mod[l, n_dec:n_dec + 1].reshape(1, 1, 6 * D_MODEL)
        x, st = _layer(x, bp, tp, mod_l, layer_params(l), None, True)
        ctx_states.append(st)
    y_prompt = _final_norm(x, final_g).reshape(bp, tp, D_MODEL)
    new_states = tuple(jnp.stack([s[i] for s in ctx_states], axis=1) for i in range(6))

    bd, td, _ = x_sample.shape
    x = _add_position(x_sample, _position_code(td // GRID_W)).reshape(bd * td, D_MODEL)
    for l in range(DEPTH):
        cached = (state_mlstm_C[:, l].astype(F32), state_mlstm_n[:, l].astype(F32),
                  state_mlstm_m[:, l].astype(F32), state_hgrn_S[:, l].astype(F32),
                  state_gla_S[:, l].astype(F32), state_lru_h[:, l].astype(F32))
        mod_l = mod[l, 0:n_dec].reshape(n_dec, 1, 6 * D_MODEL)
        x, _ = _layer(x, bd, td, mod_l, layer_params(l), cached, False)
    y_sample = _final_norm(x, final_g).reshape(bd, td, D_MODEL)
    return (y_prompt, y_sample) + new_states
```

```python
import functools

import jax
import jax.numpy as jnp
from jax import lax
from jax.experimental import pallas as pl
from jax.experimental.pallas import tpu as pltpu

F32 = jnp.float32
BF16 = jnp.bfloat16
HIGHEST = lax.Precision.HIGHEST

D_MODEL = 1024
DEPTH = 4
GRID_W = 64
EPS = 1e-6
NEG_BIG = -1e30
MAX_EXP_ARG = 80.0
MIX_W = 512
N_HEAD = 4
DH = 128
DK_C = 64
R_C = 16
GLA_TAU = 16.0
CONV_W = 4
LRU_C = 8.0
N_BRANCH = 4
N_KEYS = 128
N_EXPERTS = N_KEYS * N_KEYS
PEER_HEADS = 8
PEER_TOPK = 16
POS_BASE = 10000.0

LANE = 128
VMEM_LIMIT = 56 * 1024 * 1024

COL_GATES = 0
COL_AQ, COL_AK, COL_AV, COL_AO = 4096, 4608, 5120, 5632
COL_BQ, COL_BI, COL_BG = 6144, 6656, 7168
COL_CQ, COL_CK, COL_CV, COL_CG = 7680, 7936, 8192, 8704
COL_DG = 9216
N_WIDE = 9728
COL_BF, COL_DX = 0, 1024
COL_SMALL = 1536
SMALL_W = 256
N_FINE = COL_SMALL + SMALL_W

_WIDE_SRC = ((7216, 11312), (0, 2048), (2064, 2576), (3600, 4624), (4624, 6160), (6704, 7216))
_FINE_SRC = ((2576, 3600), (6192, 6704), (2048, 2064), (6160, 6192))

CHUNK_A = 128
CHUNK_G = 64
SUB_G = 16
CHUNK_D = 128


def _cparams(sem):
    return pltpu.CompilerParams(dimension_semantics=sem, vmem_limit_bytes=VMEM_LIMIT)


def _bdot(a, b):
    return jnp.dot(a.astype(BF16), b.astype(BF16), preferred_element_type=F32)


def _bdot_nt(a, b):
    return lax.dot_general(a.astype(BF16), b.astype(BF16), (((1,), (1,)), ((), ())),
                           preferred_element_type=F32)


def _bdot_tn(a, b):
    return lax.dot_general(a.astype(BF16), b.astype(BF16), (((0,), (0,)), ((), ())),
                           preferred_element_type=F32)


def _split(a):
    hi = a.astype(BF16)
    lo = (a - hi.astype(F32)).astype(BF16)
    return hi, lo


def _dot3(a, b_hi, b_lo):
    a_hi, a_lo = _split(a)
    return (jnp.dot(a_hi, b_hi, preferred_element_type=F32)
            + jnp.dot(a_lo, b_hi, preferred_element_type=F32)
            + jnp.dot(a_hi, b_lo, preferred_element_type=F32))


def _log_sigmoid(z):
    return jnp.minimum(z, 0.0) - jnp.log1p(jnp.exp(-jnp.abs(z)))


def _sigmoid(z):
    return 1.0 / (1.0 + jnp.exp(-z))


def _gelu(x):
    return 0.5 * x * (1.0 + jnp.tanh(0.7978845608028654 * (x + 0.044715 * x * x * x)))


def _silu(x):
    return x * _sigmoid(x)


def _expm1(y):
    u = jnp.exp(y)
    near = (u - 1.0) * y / jnp.log(u)
    return jnp.where(u == 1.0, y, jnp.where(u < 0.5, u - 1.0, near))


def _mod_kernel(c_ref, w_ref, b_ref, o_ref):
    a = _silu(c_ref[...])
    o_ref[0] = jnp.dot(a, w_ref[0], precision=HIGHEST, preferred_element_type=F32) + b_ref[0]


def _modulation(cond, w_mod, b_mod):
    r = cond.shape[0]
    tc = 1536
    return pl.pallas_call(
        _mod_kernel,
        grid=(DEPTH, 6 * D_MODEL // tc),
        in_specs=[pl.BlockSpec((r, D_MODEL), lambda l, j: (0, 0)),
                  pl.BlockSpec((1, D_MODEL, tc), lambda l, j: (l, 0, j)),
                  pl.BlockSpec((1, 1, tc), lambda l, j: (l, 0, j))],
        out_specs=pl.BlockSpec((1, r, tc), lambda l, j: (l, 0, j)),
        out_shape=jax.ShapeDtypeStruct((DEPTH, r, 6 * D_MODEL), F32),
        compiler_params=_cparams(("parallel", "parallel")),
        name="modulation",
    )(cond, w_mod, b_mod.reshape(DEPTH, 1, 6 * D_MODEL))


def _addpos_kernel(x_ref, p_ref, o_ref):
    o_ref[0] = x_ref[0] + p_ref[...]


def _add_position(x, pos):
    b, t, d = x.shape
    tt = 512
    return pl.pallas_call(
        _addpos_kernel,
        grid=(b, t // tt),
        in_specs=[pl.BlockSpec((1, tt, d), lambda i, j: (i, j, 0)),
                  pl.BlockSpec((tt, d), lambda i, j: (j, 0))],
        out_specs=pl.BlockSpec((1, tt, d), lambda i, j: (i, j, 0)),
        out_shape=jax.ShapeDtypeStruct(x.shape, F32),
        compiler_params=_cparams(("parallel", "parallel")),
        name="add_position",
    )(x, pos)


def _inproj_kernel(x_ref, mod_ref, g_ref, w_ref, b_ref, o_ref, h_ref):
    @pl.when(pl.program_id(1) == 0)
    def _():
        x = x_ref[...]
        inv = lax.rsqrt(jnp.mean(x * x, axis=-1, keepdims=True) + EPS)
        sh = mod_ref[0, :, 0:D_MODEL]
        sc = mod_ref[0, :, D_MODEL:2 * D_MODEL]
        h_ref[...] = (x * inv * g_ref[...] * (1.0 + sc) + sh).astype(BF16)

    acc = jnp.dot(h_ref[...], w_ref[...], preferred_element_type=F32) + b_ref[...]
    o_ref[...] = acc.astype(o_ref.dtype)


def _inproj(x, mod_l, norm_g, w_pack, b_pack, seq_len, layer, tc, out_dtype, name):
    n = x.shape[0]
    width = w_pack.shape[-1]
    tn = 512
    rows = mod_l.shape[0]
    if rows == 1:
        mod_map = lambda i, j: (0, 0, 0)
    else:
        mod_map = lambda i, j: ((i * tn) // seq_len, 0, 0)
    return pl.pallas_call(
        _inproj_kernel,
        grid=(n // tn, width // tc),
        in_specs=[pl.BlockSpec((tn, D_MODEL), lambda i, j: (i, 0)),
                  pl.BlockSpec((1, 1, 6 * D_MODEL), mod_map),
                  pl.BlockSpec((1, D_MODEL), lambda i, j: (0, 0)),
                  pl.BlockSpec((None, D_MODEL, tc), lambda i, j: (layer, 0, j)),
                  pl.BlockSpec((1, tc), lambda i, j: (0, j))],
        out_specs=pl.BlockSpec((tn, tc), lambda i, j: (i, j)),
        out_shape=jax.ShapeDtypeStruct((n, width), out_dtype),
        scratch_shapes=[pltpu.VMEM((tn, D_MODEL), BF16)],
        compiler_params=_cparams(("parallel", "arbitrary")),
        name=name,
    )(x, mod_l, norm_g, w_pack, b_pack)


SCAN_GROUP = 4


def _lockstep(gens):
    alive = list(gens)
    while alive:
        still = []
        for gen in alive:
            try:
                next(gen)
                still.append(gen)
            except StopIteration:
                pass
        alive = still
        if alive:
            yield


def _per_sequence(body, n_group, shared):
    def kern(*refs):
        def run(phase):
            stages = [body(*[r if i in shared else r.at[g] for i, r in enumerate(refs)], phase=phase)
                      for g in range(n_group)]
            for _ in _lockstep([s for s in stages if s is not None]):
                pass

        pl.when(pl.program_id(1) == 0)(lambda: run('init'))
        run('main')
        pl.when(pl.program_id(1) == pl.num_programs(1) - 1)(lambda: run('emit'))
    return kern


def _cumsum_rows(x, rev):
    k = 1
    while k < x.shape[0]:
        x = x + _shift_rows(x, k, rev, 0.0)
        k *= 2
    return x


def _causal_mask(length, rev):
    row = lax.broadcasted_iota(jnp.int32, (length, length), 0)
    col = lax.broadcasted_iota(jnp.int32, (length, length), 1)
    return (col >= row) if rev else (col <= row)


def _mlstm_kernel(*refs, has_init, emit_state, phase):
    L = CHUNK_A
    ins = list(refs)
    dirs = [ins[0:4], ins[4:8]]
    pos = 8
    if has_init:
        c0_ref, n0_ref, m0_ref = ins[pos:pos + 3]
        pos += 3
    outs = ins[pos:pos + 2]
    pos += 2
    if emit_state:
        cout_ref, nout_ref, mout_ref = ins[pos:pos + 3]
        pos += 3
    cs_ref, ns_ref, ms_ref = ins[pos:pos + 3]

    if phase == 'init':
        if has_init:
            cs_ref[...] = c0_ref[...]
            ns_ref[...] = n0_ref[...]
            ms_ref[...] = m0_ref[...]
        else:
            cs_ref[...] = jnp.zeros_like(cs_ref)
            ns_ref[...] = jnp.zeros_like(ns_ref)
            ms_ref[...] = jnp.zeros_like(ms_ref)
        return
    if phase == 'emit':
        if emit_state:
            cout_ref[...] = cs_ref[...]
            nout_ref[...] = ns_ref[...]
            mout_ref[...] = ms_ref[...]
        return

    def direction(d):
        rev = d == 1
        q_ref, k_ref, v_ref, s_ref = dirs[d]
        o_ref = outs[d]
        mask = _causal_mask(L, rev)
        sm = s_ref[:, 0:LANE]
        lf = _log_sigmoid(sm)
        bc = jnp.dot(mask.astype(F32), lf, precision=HIGHEST,
                     preferred_element_type=F32)
        sm_t = sm.T
        bc_t = bc.T
        last = 0 if rev else L - 1
        heads = range(N_HEAD)
        hsl = [slice(h * DH, (h + 1) * DH) for h in heads]
        yield
        li_c, b_c, logw, m_prev, m_t, w_state = [], [], [], [], [], []
        for h in heads:
            ci, cf = d * N_HEAD + h, 2 * N_HEAD + d * N_HEAD + h
            li_c.append(sm[:, ci:ci + 1])
            b_c.append(bc[:, cf:cf + 1])
            li_r, b_r = sm_t[ci:ci + 1, :], bc_t[cf:cf + 1, :]
            m_prev.append(ms_ref[d, h][:, 0:1])
            logw.append(jnp.where(mask, b_c[h] - b_r + li_r, NEG_BIG))
            from_state = b_c[h] + m_prev[h]
            m_t.append(jnp.maximum(from_state, jnp.max(logw[h], axis=-1, keepdims=True)))
            w_state.append(jnp.exp(from_state - m_t[h]))
        q = [q_ref[:, hsl[h]].astype(F32) for h in heads]
        k = [k_ref[:, hsl[h]].astype(F32) * (DH ** -0.5) for h in heads]
        v = [v_ref[:, hsl[h]] for h in heads]
        c_st = [cs_ref[d, h] for h in heads]
        n_st = [ns_ref[d, h] for h in heads]
        qk = [_bdot_nt(q[h], k[h]) for h in heads]
        qc = [_bdot(q[h], c_st[h]) for h in heads]
        yield
        for h in heads:
            scores = qk[h] * jnp.exp(logw[h] - m_t[h])
            num = w_state[h] * qc[h] + _bdot(scores, v[h])
            den = (w_state[h] * jnp.sum(q[h] * n_st[h], axis=-1, keepdims=True)
                   + jnp.sum(scores, axis=-1, keepdims=True))
            floor = jnp.exp(jnp.minimum(-m_t[h], MAX_EXP_ARG))
            o_ref[:, hsl[h]] = num / jnp.maximum(jnp.abs(den), floor)
        yield
        for h in heads:
            m_new = m_t[h][last:last + 1, :]
            b_last = b_c[h][last:last + 1, :]
            kw = k[h] * jnp.exp(b_last - b_c[h] + li_c[h] - m_new)
            decay = jnp.exp(b_last + m_prev[h] - m_new)
            cs_ref[d, h] = decay * c_st[h] + _bdot_tn(kw, v[h])
            ns_ref[d, h] = decay * n_st[h] + jnp.sum(kw, axis=0, keepdims=True)
            ms_ref[d, h] = jnp.broadcast_to(m_new, (1, LANE))

    yield from _lockstep([direction(0), direction(1)])


def _scan_maps(nc):
    fwd = lambda cb: (lambda b, c: (b, c, cb))
    bwd = lambda cb: (lambda b, c: (b, nc - 1 - c, cb))
    return fwd, bwd


def _mlstm(pw, pf, bsz, seq, state, emit_state):
    L = CHUNK_A
    nc = seq // L
    n = bsz * seq
    g = min(bsz, SCAN_GROUP)
    has_init = state is not None
    pw3 = pw.reshape(bsz, seq, N_WIDE)
    pf3 = pf.reshape(bsz, seq, N_FINE)
    fwd, bwd = _scan_maps(nc)

    in_specs, args = [], []
    for mk in (fwd, bwd):
        for col in (COL_AQ, COL_AK, COL_AV):
            in_specs.append(pl.BlockSpec((g, L, MIX_W), mk(col // MIX_W)))
            args.append(pw3)
        in_specs.append(pl.BlockSpec((g, L, SMALL_W), mk(COL_SMALL // SMALL_W)))
        args.append(pf3)
    st_specs = [pl.BlockSpec((g, 2, N_HEAD, DH, DH), lambda b, c: (b, 0, 0, 0, 0)),
                pl.BlockSpec((g, 2, N_HEAD, 1, DH), lambda b, c: (b, 0, 0, 0, 0)),
                pl.BlockSpec((g, 2, N_HEAD, 1, LANE), lambda b, c: (b, 0, 0, 0, 0))]
    st_shapes = [jax.ShapeDtypeStruct((bsz, 2, N_HEAD, DH, DH), F32),
                 jax.ShapeDtypeStruct((bsz, 2, N_HEAD, 1, DH), F32),
                 jax.ShapeDtypeStruct((bsz, 2, N_HEAD, 1, LANE), F32)]
    if has_init:
        c0, n0, m0 = state
        in_specs += st_specs
        args += [c0, n0.reshape(bsz, 2, N_HEAD, 1, DH),
                 jnp.broadcast_to(m0[..., None, None], (bsz, 2, N_HEAD, 1, LANE))]
    out_specs = [pl.BlockSpec((g, L, MIX_W), fwd(0)), pl.BlockSpec((g, L, MIX_W), bwd(0))]
    out_shape = [jax.ShapeDtypeStruct((bsz, seq, MIX_W), F32)] * 2
    if emit_state:
        out_specs += st_specs
        out_shape += st_shapes
    body = functools.partial(_mlstm_kernel, has_init=has_init, emit_state=emit_state)
    res = pl.pallas_call(
        _per_sequence(body, g, ()),
        grid=(bsz // g, nc),
        in_specs=in_specs,
        out_specs=out_specs,
        out_shape=out_shape,
        scratch_shapes=[pltpu.VMEM((g, 2, N_HEAD, DH, DH), F32),
                        pltpu.VMEM((g, 2, N_HEAD, 1, DH), F32),
                        pltpu.VMEM((g, 2, N_HEAD, 1, LANE), F32)],
        compiler_params=_cparams(("parallel", "arbitrary")),
        name="mlstm_scan",
    )(*args)
    o_f, o_b = res[0].reshape(n, MIX_W), res[1].reshape(n, MIX_W)
    new_state = None
    if emit_state:
        new_state = (res[2], res[3].reshape(bsz, 2, N_HEAD, DH), res[4][:, :, :, 0, 0])
    return o_f, o_b, new_state


def _gla_direction(q, k, la, v, s_ref, d, o_ref, *, rev, dk):
    L, S = CHUNK_G, SUB_G
    nsub = L // S
    mask = _causal_mask(L, rev)
    b = _cumsum_rows(la, rev)
    bx = b - la
    ref_rows = [bx[i * S + (S - 1 if rev else 0):i * S + (S - 1 if rev else 0) + 1, :]
                for i in range(nsub)]
    bref = jnp.concatenate([jnp.broadcast_to(r, (S, r.shape[1])) for r in ref_rows], axis=0)
    q_hat = q * jnp.exp(b - bref)
    rowid = lax.broadcasted_iota(jnp.int32, (L, 1), 0)
    k_hat = []
    for i in range(nsub):
        seen = (rowid >= i * S) if rev else (rowid < (i + 1) * S)
        k_hat.append(k * jnp.exp(jnp.where(seen, ref_rows[i] - b, NEG_BIG)))
    last = 0 if rev else L - 1
    b_last = b[last:last + 1, :]
    q_state = q * jnp.exp(b)
    k_state = k * jnp.exp(b_last - b)
    e_last = jnp.exp(b_last)
    yield
    ksl = [slice(h * dk, (h + 1) * dk) for h in range(N_HEAD)]
    vsl = [slice(h * DH, (h + 1) * DH) for h in range(N_HEAD)]
    blocks = [[_bdot_nt(q_hat[i * S:(i + 1) * S, ksl[h]], k_hat[i][:, ksl[h]]) for i in range(nsub)]
              for h in range(N_HEAD)]
    states = [s_ref[d, h] for h in range(N_HEAD)]
    inter = [_bdot_nt(q_state[:, ksl[h]], states[h]) for h in range(N_HEAD)]
    grown = [_bdot_tn(v[:, vsl[h]], k_state[:, ksl[h]]) for h in range(N_HEAD)]
    yield
    for h in range(N_HEAD):
        scores = jnp.where(mask, jnp.concatenate(blocks[h], axis=0), 0.0)
        o_ref[:, vsl[h]] = inter[h] + _bdot(scores, v[:, vsl[h]])
    yield
    for h in range(N_HEAD):
        s_ref[d, h] = states[h] * e_last[:, ksl[h]] + grown[h]


def _scan_state_io(ins, pos, has_init, emit_state, n_out):
    s0_ref = sout_ref = None
    if has_init:
        s0_ref = ins[pos]
        pos += 1
    outs = ins[pos:pos + n_out]
    pos += n_out
    if emit_state:
        sout_ref = ins[pos]
        pos += 1
    return s0_ref, outs, sout_ref, ins[pos]


def _scan_state_phase(phase, s_ref, s0_ref, sout_ref):
    if phase == 'init':
        if s0_ref is not None:
            s_ref[...] = s0_ref[...]
        else:
            s_ref[...] = jnp.zeros_like(s_ref)
    elif phase == 'emit' and sout_ref is not None:
        sout_ref[...] = s_ref[...]
    return phase != 'main'


def _hgrn_kernel(*refs, has_init, emit_state, phase):
    ins = list(refs)
    lb_ref = ins[6]
    s0_ref, outs, sout_ref, s_ref = _scan_state_io(ins, 7, has_init, emit_state, 2)
    if _scan_state_phase(phase, s_ref, s0_ref, sout_ref):
        return
    chains = []
    for d in range(2):
        q_ref, z_ref, v_ref = ins[3 * d:3 * d + 3]
        z = z_ref[...]
        lb = lb_ref[d:d + 1, :]
        la = _log_sigmoid(z) + jnp.log1p(lb * jnp.exp(jnp.minimum(-z, MAX_EXP_ARG)))
        k = (1.0 - lb) * _sigmoid(-z)
        q = q_ref[...].astype(F32) * (DH ** -0.5)
        chains.append(_gla_direction(q, k, la, v_ref[...], s_ref, d, outs[d], rev=d == 1, dk=DH))
    yield from _lockstep(chains)


def _gla_kernel(*refs, has_init, emit_state, phase):
    ins = list(refs)
    up_ref, upb_ref = ins[8], ins[9]
    s0_ref, outs, sout_ref, s_ref = _scan_state_io(ins, 10, has_init, emit_state, 2)
    if _scan_state_phase(phase, s_ref, s0_ref, sout_ref):
        return
    chains = []
    for d in range(2):
        q_ref, k_ref, v_ref, sm_ref = ins[4 * d:4 * d + 4]
        zg = jnp.dot(sm_ref[...], up_ref[d], precision=HIGHEST,
                     preferred_element_type=F32) + upb_ref[d:d + 1, :]
        la = _log_sigmoid(zg) * (1.0 / GLA_TAU)
        q = q_ref[...].astype(F32) * (DK_C ** -0.5)
        k = k_ref[...].astype(F32)
        chains.append(_gla_direction(q, k, la, v_ref[...], s_ref, d, outs[d], rev=d == 1,
                                     dk=DK_C))
    yield from _lockstep(chains)


def _gated_scan(kernel, p, bsz, seq, cols, extra, extra_specs, dk, state_t, emit_state, name):
    L = CHUNK_G
    nc = seq // L
    n = bsz * seq
    g = min(bsz, SCAN_GROUP)
    has_init = state_t is not None
    src = {'wide': p[0].reshape(bsz, seq, N_WIDE), 'fine': p[1].reshape(bsz, seq, N_FINE)}
    fwd, bwd = _scan_maps(nc)

    in_specs, args = [], []
    for d, mk in enumerate((fwd, bwd)):
        for which, off, width in cols[d]:
            in_specs.append(pl.BlockSpec((g, L, width), mk(off // width)))
            args.append(src[which])
    shared = tuple(range(len(args), len(args) + len(extra)))
    in_specs += extra_specs
    args += extra
    st_spec = pl.BlockSpec((g, 2, N_HEAD, DH, dk), lambda b, c: (b, 0, 0, 0, 0))
    if has_init:
        in_specs.append(st_spec)
        args.append(state_t)
    out_specs = [pl.BlockSpec((g, L, MIX_W), fwd(0)), pl.BlockSpec((g, L, MIX_W), bwd(0))]
    out_shape = [jax.ShapeDtypeStruct((bsz, seq, MIX_W), F32)] * 2
    if emit_state:
        out_specs.append(st_spec)
        out_shape.append(jax.ShapeDtypeStruct((bsz, 2, N_HEAD, DH, dk), F32))
    body = functools.partial(kernel, has_init=has_init, emit_state=emit_state)
    res = pl.pallas_call(
        _per_sequence(body, g, shared),
        grid=(bsz // g, nc),
        in_specs=in_specs,
        out_specs=out_specs,
        out_shape=out_shape,
        scratch_shapes=[pltpu.VMEM((g, 2, N_HEAD, DH, dk), F32)],
        compiler_params=_cparams(("parallel", "arbitrary")),
        name=name,
    )(*args)
    return res[0].reshape(n, MIX_W), res[1].reshape(n, MIX_W), (res[2] if emit_state else None)


def _hgrn(p, bsz, seq, lb, state_t, emit_state):
    cols = [[('wide', COL_BQ, MIX_W), ('fine', COL_BF + d * MIX_W, MIX_W), ('wide', COL_BI, MIX_W)]
            for d in range(2)]
    return _gated_scan(_hgrn_kernel, p, bsz, seq, cols, [lb],
                       [pl.BlockSpec((2, MIX_W), lambda b, c: (0, 0))], DH, state_t, emit_state,
                       "hgrn2_scan")


def _gla(p, bsz, seq, up_pad, up_b, state_t, emit_state):
    kw = N_HEAD * DK_C
    cols = [[('wide', COL_CQ, kw), ('wide', COL_CK, kw), ('wide', COL_CV, MIX_W),
             ('fine', COL_SMALL, SMALL_W)] for _ in range(2)]
    return _gated_scan(_gla_kernel, p, bsz, seq, cols, [up_pad, up_b],
                       [pl.BlockSpec((2, SMALL_W, kw), lambda b, c: (0, 0, 0)),
                        pl.BlockSpec((2, kw), lambda b, c: (0, 0))], DK_C, state_t, emit_state,
                       "gla_scan")


def _shift_rows(x, k, rev, fill):
    n = x.shape[0]
    rowid = lax.broadcasted_iota(jnp.int32, x.shape, 0)
    if rev:
        return jnp.where(rowid >= n - k, fill, pltpu.roll(x, n - k, 0))
    return jnp.where(rowid < k, fill, pltpu.roll(x, k, 0))


def _lru_kernel(*refs, seq, has_init, emit_state):
    L = CHUNK_D
    nc = seq // L
    ins = list(refs)
    (dx_ref, dg_ref, cw_ref, cb_ref, wah_ref, wal_ref, ba_ref, wxh_ref, wxl_ref, bx_ref,
     lam_ref) = ins[0:11]
    pos = 11
    h0_ref = hout_ref = None
    if has_init:
        h0_ref = ins[pos]
        pos += 1
    y_ref = ins[pos]
    pos += 1
    if emit_state:
        hout_ref = ins[pos]
        pos += 1
    pad_ref, hb_ref = ins[pos], ins[pos + 1]

    zeros8 = jnp.zeros((8, LANE), F32)
    pad_ref[0:8, :] = zeros8
    pad_ref[8:8 + seq, :] = dx_ref[...]
    pad_ref[8 + seq:16 + seq, :] = zeros8
    lam = lam_ref[...]
    sp = jnp.maximum(-lam, 0.0) + jnp.log1p(jnp.exp(-jnp.abs(lam)))
    cw = cw_ref[...]
    cb = cb_ref[...]

    def chunk_scan(c, carry, d, out):
        rev = d == 1
        start = pl.multiple_of(c * L, L)
        win = pad_ref[pl.ds(start, L + 16), :]
        xd = cb + sum(cw[j:j + 1, :] * win[7 + j:7 + j + L, :] for j in range(CONV_W))
        yield
        za = _dot3(xd, wah_ref[d, 0], wal_ref[d, 0])
        zx = _dot3(xd, wxh_ref[d, 0], wxl_ref[d, 0])
        yield
        r = _sigmoid(za + ba_ref[d:d + 1, :])
        ig = _sigmoid(zx + bx_ref[d:d + 1, :])
        log_a = -LRU_C * r * sp[d:d + 1, :]
        a = jnp.exp(log_a)
        u = jnp.sqrt(jnp.maximum(-_expm1(2.0 * log_a), 0.0)) * (ig * xd)
        yield
        k = 1
        while k < L:
            u = a * _shift_rows(u, k, rev, 0.0) + u
            a = a * _shift_rows(a, k, rev, 1.0)
            k *= 2
            yield
        h = a * carry + u
        last = 0 if rev else L - 1
        out.extend((start, h, h[last:last + 1, :]))

    def scan_body(i, carry):
        res_f, res_b = [], []
        for _ in _lockstep([chunk_scan(i, carry[0], 0, res_f),
                            chunk_scan(nc - 1 - i, carry[1], 1, res_b)]):
            pass
        y_ref[pl.ds(res_f[0], L), :] = res_f[1]
        hb_ref[pl.ds(res_b[0], L), :] = res_b[1]
        return res_f[2], res_b[2]

    def gate_body(c, carry):
        rows = pl.ds(pl.multiple_of(c * L, L), L)
        y_ref[rows, :] = (y_ref[rows, :] + hb_ref[rows, :]) * _gelu(dg_ref[rows, :].astype(F32))
        return carry

    if has_init:
        init = (h0_ref[0, 0:1, :], h0_ref[0, 1:2, :])
    else:
        init = (jnp.zeros((1, LANE), F32), jnp.zeros((1, LANE), F32))
    fin_f, fin_b = lax.fori_loop(0, nc, scan_body, init)
    lax.fori_loop(0, nc, gate_body, 0)
    if emit_state:
        hout_ref[0, 0:1, :] = fin_f
        hout_ref[0, 1:2, :] = fin_b


def _lru(pw, pf, bsz, seq, conv_w, conv_b, wa, ba, wx, bx, lam, h0, emit_state):
    n = bsz * seq
    has_init = h0 is not None
    wah, wal = _split(wa)
    wxh, wxl = _split(wx)
    gate_w = pl.BlockSpec((2, 1, DH, DH), lambda b, h: (0, h, 0, 0))
    vec2 = pl.BlockSpec((2, LANE), lambda b, h: (0, h))
    in_specs = [pl.BlockSpec((seq, LANE), lambda b, h: (b, COL_DX // LANE + h)),
                pl.BlockSpec((seq, LANE), lambda b, h: (b, COL_DG // LANE + h)),
                pl.BlockSpec((CONV_W, LANE), lambda b, h: (0, h)),
                pl.BlockSpec((1, LANE), lambda b, h: (0, h)),
                gate_w, gate_w, vec2, gate_w, gate_w, vec2, vec2]
    args = [pf, pw, conv_w, conv_b.reshape(1, MIX_W), wah, wal, ba, wxh, wxl, bx, lam]
    st_spec = pl.BlockSpec((1, 2, LANE), lambda b, h: (b, 0, h))
    if has_init:
        in_specs.append(st_spec)
        args.append(h0)
    out_specs = [pl.BlockSpec((seq, LANE), lambda b, h: (b, h))]
    out_shape = [jax.ShapeDtypeStruct((n, MIX_W), F32)]
    if emit_state:
        out_specs.append(st_spec)
        out_shape.append(jax.ShapeDtypeStruct((bsz, 2, MIX_W), F32))
    res = pl.pallas_call(
        functools.partial(_lru_kernel, seq=seq, has_init=has_init, emit_state=emit_state),
        grid=(bsz, N_HEAD),
        in_specs=in_specs,
        out_specs=out_specs,
        out_shape=out_shape,
        scratch_shapes=[pltpu.VMEM((seq + 16, LANE), F32), pltpu.VMEM((seq, LANE), F32)],
        compiler_params=_cparams(("parallel", "parallel")),
        name="conv_rglru",
    )(*args)
    return res[0], (res[1] if emit_state else None)


def _head_rms(x):
    parts = []
    for h in range(N_HEAD):
        xh = x[:, h * DH:(h + 1) * DH]
        parts.append(xh * lax.rsqrt(jnp.mean(xh * xh, axis=-1, keepdims=True) + EPS))
    return jnp.concatenate(parts, axis=-1)


def _merge_kernel(af_ref, ab_ref, bf_ref, bb_ref, cf_ref, cb_ref, yd_ref, ao_ref, bg_ref, cg_ref,
                  gt0_ref, gt1_ref, gt2_ref, gt3_ref, x_ref, mod_ref, g2_ref, wbr_ref, wout_ref,
                  xo_ref, h2_ref, h2b_ref):
    ya = _sigmoid(ao_ref[...].astype(F32)) * _head_rms(af_ref[...] + ab_ref[...])
    yb = _silu(bg_ref[...].astype(F32)) * _head_rms(bf_ref[...] + bb_ref[...])
    yc = _silu(cg_ref[...].astype(F32)) * _head_rms(cf_ref[...] + cb_ref[...])
    merged = None
    gate_refs = (gt0_ref, gt1_ref, gt2_ref, gt3_ref)
    for i, y in enumerate((ya, yb, yc, yd_ref[...])):
        proj = jnp.dot(y.astype(BF16), wbr_ref[i], preferred_element_type=F32)
        term = _sigmoid(gate_refs[i][...].astype(F32)) * proj
        merged = term if merged is None else merged + term
    out = jnp.dot(merged.astype(BF16), wout_ref[...], preferred_element_type=F32)
    g1 = mod_ref[0, :, 2 * D_MODEL:3 * D_MODEL]
    sh2 = mod_ref[0, :, 3 * D_MODEL:4 * D_MODEL]
    sc2 = mod_ref[0, :, 4 * D_MODEL:5 * D_MODEL]
    x = x_ref[...] + g1 * out
    xo_ref[...] = x
    inv = lax.rsqrt(jnp.mean(x * x, axis=-1, keepdims=True) + EPS)
    h2 = x * inv * g2_ref[...] * (1.0 + sc2) + sh2
    h2_ref[...] = h2
    h2b_ref[...] = h2.astype(BF16)


def _merge(mix_outs, yd, p, x, mod_l, norm2_g, w_branch, w_out, seq_len, layer):
    n = x.shape[0]
    tn = 256
    rows = mod_l.shape[0]
    mod_map = (lambda i: (0, 0, 0)) if rows == 1 else (lambda i: ((i * tn) // seq_len, 0, 0))
    tok = lambda cb: (lambda i: (i, cb))
    in_specs = [pl.BlockSpec((tn, MIX_W), tok(0))] * 7
    in_specs += [pl.BlockSpec((tn, MIX_W), tok(COL_AO // MIX_W)),
                 pl.BlockSpec((tn, MIX_W), tok(COL_BG // MIX_W)),
                 pl.BlockSpec((tn, MIX_W), tok(COL_CG // MIX_W)),
                 *[pl.BlockSpec((tn, D_MODEL), tok(COL_GATES // D_MODEL + i)) for i in range(N_BRANCH)],
                 pl.BlockSpec((tn, D_MODEL), tok(0)),
                 pl.BlockSpec((1, 1, 6 * D_MODEL), mod_map),
                 pl.BlockSpec((1, D_MODEL), lambda i: (0, 0)),
                 pl.BlockSpec((None, N_BRANCH, MIX_W, D_MODEL), lambda i: (layer, 0, 0, 0)),
                 pl.BlockSpec((None, D_MODEL, D_MODEL), lambda i: (layer, 0, 0))]
    out_spec = pl.BlockSpec((tn, D_MODEL), tok(0))
    return pl.pallas_call(
        _merge_kernel,
        grid=(n // tn,),
        in_specs=in_specs,
        out_specs=[out_spec, out_spec, out_spec],
        out_shape=[jax.ShapeDtypeStruct((n, D_MODEL), F32), jax.ShapeDtypeStruct((n, D_MODEL), F32),
                   jax.ShapeDtypeStruct((n, D_MODEL), BF16)],
        compiler_params=_cparams(("parallel",)),
        name="branch_merge",
    )(*mix_outs, yd, p, p, p, p, p, p, p, x, mod_l, norm2_g, w_branch, w_out)


ROUTE_TN = 512
ROUTE_LG = LANE


def _top16_exact(vals):
    n_rows = vals.shape[0]
    rowid = lax.broadcasted_iota(jnp.int32, vals.shape, 0).astype(F32)
    rank = jnp.full(vals.shape, float(PEER_TOPK), F32)
    tops = []
    for r in range(PEER_TOPK):
        m = jnp.max(vals, axis=0, keepdims=True)
        idx = jnp.min(jnp.where(vals == m, rowid, float(n_rows)), axis=0, keepdims=True)
        sel = rowid == idx
        rank = jnp.where(sel, float(r), rank)
        vals = jnp.where(sel, -jnp.inf, vals)
        tops.append(m)
    return jnp.concatenate(tops, axis=0), rank


_TAKEN_BASE = -3.0e38
_TAKEN_STEP = 2.0e36
_TAKEN_BELOW = _TAKEN_BASE + 0.5 * _TAKEN_STEP


def _top16_quick(vals):
    return _top16_quick_many([vals])[0]


def _top16_quick_many(vals_list):
    work = list(vals_list)
    tops = [[] for _ in work]
    for r in range(PEER_TOPK):
        for i in range(len(work)):
            m = jnp.max(work[i], axis=0, keepdims=True)
            work[i] = jnp.where(work[i] == m, _TAKEN_BASE - r * _TAKEN_STEP, work[i])
            tops[i].append(m)
    out = []
    for i in range(len(work)):
        taken = work[i] <= _TAKEN_BELOW
        rank = jnp.where(taken, jnp.round((_TAKEN_BASE - work[i]) * (1.0 / _TAKEN_STEP)),
                         float(PEER_TOPK))
        n_taken = jnp.sum(jnp.where(taken, 1.0, 0.0), axis=0, keepdims=True)
        out.append((jnp.concatenate(tops[i], axis=0), rank, n_taken))
    return out


_CAND_PIECES = (('a', 0, 0, 16), ('a', 0, 8, 16), ('a', 1, 0, 8), ('a', 2, 0, 5), ('a', 3, 0, 4),
                ('b', 0, 0, (4, 8)), ('b', 0, 8, (8, 16)), ('b', 1, 0, (4, 8)), ('b', 2, 0, (4, 5)))


def _pair_pieces(x1, x2, combine):
    out = []
    for kind, fixed, off, _ in _CAND_PIECES:
        if kind == 'a':
            out.append(combine(x1[fixed:fixed + 1, :], x2[off:off + 8, :]))
        else:
            out.append(combine(x1[off:off + 8, :], x2[fixed:fixed + 1, :]))
    return out


def _candidate_sums(s1, s2):
    K = PEER_TOPK
    n = s1.shape[1]
    iota8 = lax.broadcasted_iota(jnp.int32, (8, n), 0)
    sums = _pair_pieces(s1, s2, lambda x, y: x + y)
    vals, poss = [], []
    for (kind, fixed, off, lim), sm in zip(_CAND_PIECES, sums):
        idx = iota8 + off
        if kind == 'a':
            valid = idx < lim
            pos = fixed * K + idx
        else:
            valid = (idx >= lim[0]) & (idx < lim[1])
            pos = idx * K + fixed
        vals.append(jnp.where(valid, sm, -jnp.inf))
        poss.append(jnp.where(valid, pos, K * K).astype(F32))
    return jnp.concatenate(vals, axis=0), jnp.concatenate(poss, axis=0)


def _choose_exact(vals, posid):
    K = PEER_TOPK
    chosen = jnp.zeros(vals.shape, F32)
    for _ in range(K):
        m = jnp.max(vals, axis=0, keepdims=True)
        idx = jnp.min(jnp.where(vals == m, posid, float(K * K)), axis=0, keepdims=True)
        sel = posid == idx
        chosen = jnp.where(sel, 1.0, chosen)
        vals = jnp.where(sel, -jnp.inf, vals)
    return chosen


def _choose_quick(vals):
    return _choose_quick_many([vals])[0]


def _choose_quick_many(vals_list):
    work = list(vals_list)
    for _ in range(PEER_TOPK):
        for i in range(len(work)):
            m = jnp.max(work[i], axis=0, keepdims=True)
            work[i] = jnp.where(work[i] == m, _TAKEN_BASE, work[i])
    out = []
    for w in work:
        chosen = jnp.where(w == _TAKEN_BASE, 1.0, 0.0)
        out.append((chosen, jnp.sum(chosen, axis=0, keepdims=True)))
    return out


def _route_kernel(h_ref, wqh_ref, wql_ref, kh_ref, kl_ref, a_ref, b_ref, r2_ref, c_ref,
                  qh_ref, ql_ref, s_ref, top_ref, rank_ref, ch_ref):
    K = PEER_TOPK
    n_lg = ROUTE_TN // ROUTE_LG
    q = _dot3(h_ref[...], wqh_ref[...], wql_ref[...])
    q_hi, q_lo = _split(q)
    for i in range(2 * PEER_HEADS):
        qh_ref[i] = q_hi[:, i * N_KEYS:(i + 1) * N_KEYS]
        ql_ref[i] = q_lo[:, i * N_KEYS:(i + 1) * N_KEYS]
    nt = (((1,), (1,)), ((), ()))
    lanes = [slice(lg * ROUTE_LG, (lg + 1) * ROUTE_LG) for lg in range(n_lg)]

    def miscount(counts):
        worst = None
        for cnt in counts:
            dev = jnp.abs(cnt - float(K))
            worst = dev if worst is None else jnp.maximum(worst, dev)
        return jnp.max(worst) > 0.0

    def head_body(h, carry):
        for half in range(2):
            kh, kl = kh_ref[h, half], kl_ref[h, half]
            qh, ql = qh_ref[2 * h + half], ql_ref[2 * h + half]
            s_ref[half] = (lax.dot_general(kh, qh, nt, preferred_element_type=F32)
                           + lax.dot_general(kl, qh, nt, preferred_element_type=F32)
                           + lax.dot_general(kh, ql, nt, preferred_element_type=F32))

        problems = [(half, lg) for half in range(2) for lg in range(n_lg)]
        quick = _top16_quick_many([s_ref[half, :, lanes[lg]] for half, lg in problems])
        counts = []
        for (half, lg), (top, rank, n_taken) in zip(problems, quick):
            top_ref[half, lg] = top
            rank_ref[half, lg] = rank
            counts.append(n_taken)

        @pl.when(miscount(counts))
        def _():
            for half in range(2):
                for lg in range(n_lg):
                    top, rank = _top16_exact(s_ref[half, :, lanes[lg]])
                    top_ref[half, lg] = top
                    rank_ref[half, lg] = rank

        cands = [_candidate_sums(top_ref[0, lg], top_ref[1, lg])[0] for lg in range(n_lg)]
        counts = []
        for lg, (chosen, n_taken) in enumerate(_choose_quick_many(cands)):
            ch_ref[lg] = chosen
            counts.append(n_taken)

        @pl.when(miscount(counts))
        def _():
            for lg in range(n_lg):
                vals, posid = _candidate_sums(top_ref[0, lg], top_ref[1, lg])
                ch_ref[lg] = _choose_exact(vals, posid)

        for lg in range(n_lg):
            ls = lanes[lg]
            st1, st2 = s_ref[0, :, ls], s_ref[1, :, ls]
            s1, s2 = top_ref[0, lg], top_ref[1, lg]
            rank1, rank2 = rank_ref[0, lg], rank_ref[1, lg]
            chosen = ch_ref[lg]
            ch = [chosen[8 * i:8 * i + 8, :] for i in range(len(_CAND_PIECES))]
            e1 = jnp.exp(s1 - s1[0:1, :])
            e2 = jnp.exp(s2 - s2[0:1, :])
            pair = _pair_pieces(e1, e2, lambda x, y: x * y)
            z = sum(jnp.sum(c * p, axis=0, keepdims=True) for c, p in zip(ch, pair))
            iota8 = lax.broadcasted_iota(jnp.int32, (8, ROUTE_LG), 0)
            low = ch[5] + ch[7] + ch[8]
            for a, cnt in ((3, ch[4]), (2, ch[3]), (1, ch[2]), (0, ch[0] + ch[1])):
                low = jnp.where(iota8 == a, jnp.sum(cnt, axis=0, keepdims=True), low)
            counts = jnp.concatenate([low, ch[6]], axis=0)
            c_dense = jnp.zeros((N_KEYS, ROUTE_LG), F32)
            for a in range(K):
                c_dense = jnp.where(rank1 == float(a), counts[a:a + 1, :], c_dense)
            a_ref[h, :, ls] = jnp.where(rank1 < float(K), jnp.exp(st1 - s1[0:1, :]), 0.0) / z
            b_ref[h, :, ls] = jnp.where(rank2 < float(K), jnp.exp(st2 - s2[0:1, :]), 0.0).astype(BF16)
            r2_ref[h, :, ls] = rank2.astype(BF16)
            c_ref[h, :, ls] = c_dense
        return carry

    lax.fori_loop(0, PEER_HEADS, head_body, 0)


def _route(h2, wq_hi, wq_lo, keys_hi, keys_lo, layer):
    n = h2.shape[0]
    tn = ROUTE_TN
    dense = pl.BlockSpec((PEER_HEADS, N_KEYS, tn), lambda i: (0, 0, i))
    wspec = pl.BlockSpec((None, D_MODEL, PEER_HEADS * 2 * N_KEYS), lambda i: (layer, 0, 0))
    kspec = pl.BlockSpec((None, PEER_HEADS, 2, N_KEYS, N_KEYS), lambda i: (layer, 0, 0, 0, 0))
    return pl.pallas_call(
        _route_kernel,
        grid=(n // tn,),
        in_specs=[pl.BlockSpec((tn, D_MODEL), lambda i: (i, 0)), wspec, wspec, kspec, kspec],
        out_specs=[dense] * 4,
        out_shape=[jax.ShapeDtypeStruct((PEER_HEADS, N_KEYS, n), dt) for dt in (F32, BF16, BF16, F32)],
        scratch_shapes=[pltpu.VMEM((2 * PEER_HEADS, tn, N_KEYS), BF16),
                        pltpu.VMEM((2 * PEER_HEADS, tn, N_KEYS), BF16),
                        pltpu.VMEM((2, N_KEYS, tn), F32),
                        pltpu.VMEM((2, tn // ROUTE_LG, PEER_TOPK, ROUTE_LG), F32),
                        pltpu.VMEM((2, tn // ROUTE_LG, N_KEYS, ROUTE_LG), F32),
                        pltpu.VMEM((tn // ROUTE_LG, 8 * len(_CAND_PIECES), ROUTE_LG), F32)],
        compiler_params=_cparams(("parallel",)),
        name="peer_route",
    )(h2, wq_hi, wq_lo, keys_hi, keys_lo)


PEER_TN = 512
PEER_TE = 1024


PEER_SPLIT = 1


def _gelu_bf16(t):
    x = t.astype(BF16)
    y2 = x * (1.0 + 0.044715 * (x * x)) * (-2.0 * 0.7978845608028654)
    return x / (1.0 + jnp.exp(y2))


BF16_ROWS = 16


def _row_to_packed(row):
    tile = jnp.broadcast_to(row, (BF16_ROWS, row.shape[1])).astype(BF16)
    return jnp.concatenate([tile] * (N_KEYS // BF16_ROWS), axis=0)


def _expert_kernel(hb_ref, u_ref, vt_ref, a_ref, b_ref, r2_ref, c_ref, x_ref, mod_ref, o_ref,
                   acc_ref):
    e = pl.program_id(1)
    rows_per_tile = PEER_TE // N_KEYS
    rows_per_part = rows_per_tile // PEER_SPLIT
    part_w = PEER_TE // PEER_SPLIT

    @pl.when(e == 0)
    def _():
        acc_ref[...] = jnp.zeros_like(acc_ref)

    update = None
    for part in range(PEER_SPLIT):
        t_t = lax.dot_general(u_ref[part * part_w:(part + 1) * part_w, :], hb_ref[...],
                              (((1,), (1,)), ((), ())), preferred_element_type=F32)
        w_parts = []
        for jj in range(rows_per_part):
            j = part * rows_per_part + jj
            e1 = e * rows_per_tile + j
            g = None
            for h in range(PEER_HEADS):
                a_row = _row_to_packed(a_ref[h, pl.ds(e1, 1), :])
                c_row = _row_to_packed(c_ref[h, pl.ds(e1, 1), :])
                term = a_row * jnp.where(r2_ref[h] < c_row, b_ref[h], 0.0)
                g = term if g is None else g + term
            w_parts.append(g * _gelu_bf16(t_t[jj * N_KEYS:(jj + 1) * N_KEYS, :]))
        w_t = jnp.concatenate(w_parts, axis=0)
        upd = jnp.dot(vt_ref[:, part * part_w:(part + 1) * part_w], w_t, preferred_element_type=F32)
        update = upd if update is None else update + upd
    acc_ref[...] += update

    @pl.when(e == pl.num_programs(1) - 1)
    def _():
        g2 = mod_ref[0, :, 5 * D_MODEL:6 * D_MODEL]
        o_ref[...] = x_ref[...] + g2 * acc_ref[...].T


def _experts(h2b, u_b, vt_b, dense, x, mod_l, seq_len, layer):
    n = x.shape[0]
    tn, te = PEER_TN, PEER_TE
    rows = mod_l.shape[0]
    mod_map = (lambda i, e: (0, 0, 0)) if rows == 1 else (lambda i, e: ((i * tn) // seq_len, 0, 0))
    dspec = pl.BlockSpec((PEER_HEADS, N_KEYS, tn), lambda i, e: (0, 0, i))
    return pl.pallas_call(
        _expert_kernel,
        grid=(n // tn, N_EXPERTS // te),
        in_specs=[pl.BlockSpec((tn, D_MODEL), lambda i, e: (i, 0)),
                  pl.BlockSpec((None, te, D_MODEL), lambda i, e: (layer, e, 0)),
                  pl.BlockSpec((None, D_MODEL, te), lambda i, e: (layer, 0, e)),
                  dspec, dspec, dspec, dspec,
                  pl.BlockSpec((tn, D_MODEL), lambda i, e: (i, 0)),
                  pl.BlockSpec((1, 1, 6 * D_MODEL), mod_map)],
        out_specs=pl.BlockSpec((tn, D_MODEL), lambda i, e: (i, 0)),
        out_shape=jax.ShapeDtypeStruct((n, D_MODEL), F32),
        scratch_shapes=[pltpu.VMEM((D_MODEL, tn), F32)],
        compiler_params=_cparams(("parallel", "arbitrary")),
        name="peer_experts",
    )(h2b, u_b, vt_b, *dense, x, mod_l)


def _final_norm_kernel(x_ref, g_ref, o_ref):
    x = x_ref[...]
    o_ref[...] = x * lax.rsqrt(jnp.mean(x * x, axis=-1, keepdims=True) + EPS) * g_ref[...]


def _final_norm(x, g):
    n = x.shape[0]
    tn = 512
    return pl.pallas_call(
        _final_norm_kernel,
        grid=(n // tn,),
        in_specs=[pl.BlockSpec((tn, D_MODEL), lambda i: (i, 0)),
                  pl.BlockSpec((1, D_MODEL), lambda i: (0, 0))],
        out_specs=pl.BlockSpec((tn, D_MODEL), lambda i: (i, 0)),
        out_shape=jax.ShapeDtypeStruct((n, D_MODEL), F32),
        compiler_params=_cparams(("parallel",)),
        name="final_norm",
    )(x, g)


def _pack_columns(a, src, width):
    parts = [a[..., lo:hi] for lo, hi in src]
    used = sum(hi - lo for lo, hi in src)
    if used < width:
        parts.append(jnp.zeros(a.shape[:-1] + (width - used,), a.dtype))
    return jnp.concatenate(parts, axis=-1)


def _position_code(rows):
    quarter = D_MODEL // 4
    omega = 1.0 / (POS_BASE ** (jnp.arange(quarter, dtype=F32) / quarter))
    r, col = jnp.meshgrid(jnp.arange(rows, dtype=F32), jnp.arange(GRID_W, dtype=F32), indexing='ij')

    def enc(pos):
        ang = pos.reshape(-1, 1) * omega
        return jnp.concatenate([jnp.sin(ang), jnp.cos(ang)], axis=-1)
    return jnp.concatenate([enc(r), enc(col)], axis=-1)


def _layer(x, bsz, seq, mod_l, lp, state, emit_state):
    layer = lp['layer']
    pw = _inproj(x, mod_l, lp['norm1_g'], lp['w_wide'], lp['b_wide'], seq, layer, 2432, BF16,
                 "in_projection_wide")
    pf = _inproj(x, mod_l, lp['norm1_g'], lp['w_fine'], lp['b_fine'], seq, layer, 896, F32,
                 "in_projection_fine")
    if state is None:
        st_a = st_b = st_c = st_d = None
    else:
        c0, n0, m0, sb0, sc0, hd0 = state
        st_a = (c0, n0, m0)
        st_b = jnp.swapaxes(sb0, -1, -2)
        st_c = jnp.swapaxes(sc0, -1, -2)
        st_d = hd0
    a_f, a_b, new_a = _mlstm(pw, pf, bsz, seq, st_a, emit_state)
    b_f, b_b, new_b = _hgrn((pw, pf), bsz, seq, lp['hgrn_lb'], st_b, emit_state)
    c_f, c_b, new_c = _gla((pw, pf), bsz, seq, lp['gla_up_pad'], lp['gla_up_b'], st_c, emit_state)
    yd, new_d = _lru(pw, pf, bsz, seq, lp['conv_w'], lp['conv_b'], lp['lru_w_a'], lp['lru_b_a'],
                     lp['lru_w_x'], lp['lru_b_x'], lp['lru_lambda'], st_d, emit_state)
    x1, h2, h2b = _merge((a_f, a_b, b_f, b_b, c_f, c_b), yd, pw, x, mod_l, lp['norm2_g'],
                         lp['w_branch'], lp['w_out'], seq, layer)
    dense = _route(h2, lp['wq_hi'], lp['wq_lo'], lp['keys_hi'], lp['keys_lo'], layer)
    x2 = _experts(h2b, lp['peer_u'], lp['peer_vt'], dense, x1, mod_l, seq, layer)
    new_state = None
    if emit_state:
        new_state = (*new_a, jnp.swapaxes(new_b, -1, -2), jnp.swapaxes(new_c, -1, -2), new_d)
    return x2, new_state


def kernel(x_prompt, x_sample, state_mlstm_C, state_mlstm_n, state_mlstm_m, state_hgrn_S,
           state_gla_S, state_lru_h, c, c_ctx, norm1_g, norm2_g, final_norm_g, w_mod, b_mod,
           w_in, b_in, w_gla_up, b_gla_up, hgrn_lower_bounds, conv_w, conv_b, lru_w_a, lru_b_a,
           lru_w_x, lru_b_x, lru_lambda, w_branch, w_out, peer_w_q, peer_sub_keys, peer_u, peer_v):
    lb_soft = jax.nn.softmax(hgrn_lower_bounds.astype(F32), axis=0)
    lb_all = jnp.cumsum(lb_soft, axis=0) - lb_soft[0:1]
    w_wide = _pack_columns(w_in, _WIDE_SRC, N_WIDE).astype(BF16)
    w_fine = _pack_columns(w_in, _FINE_SRC, N_FINE).astype(BF16)
    b_wide = _pack_columns(b_in, _WIDE_SRC, N_WIDE).reshape(DEPTH, 1, N_WIDE)
    b_fine = _pack_columns(b_in, _FINE_SRC, N_FINE).reshape(DEPTH, 1, N_FINE)
    kw = N_HEAD * DK_C
    up_pad = jnp.zeros((DEPTH, 2, SMALL_W, kw), F32)
    for d in range(2):
        lo = 2 * N_HEAD * 2 + d * R_C
        up_pad = up_pad.at[:, d, lo:lo + R_C, :].set(w_gla_up[:, d].astype(F32))
    wq_hi, wq_lo = _split(peer_w_q)
    keys_hi, keys_lo = _split(peer_sub_keys)
    u_b = peer_u.astype(BF16)
    vt_b = jnp.swapaxes(peer_v, 1, 2).astype(BF16)
    wbr_b = w_branch.astype(BF16)
    wout_b = w_out.astype(BF16)

    def layer_params(l):
        return {
            'norm1_g': norm1_g[l].reshape(1, D_MODEL), 'norm2_g': norm2_g[l].reshape(1, D_MODEL),
            'layer': l, 'w_wide': w_wide, 'b_wide': b_wide[l], 'w_fine': w_fine, 'b_fine': b_fine[l],
            'hgrn_lb': lb_all[l],
            'gla_up_pad': up_pad[l], 'gla_up_b': b_gla_up[l],
            'conv_w': conv_w[l], 'conv_b': conv_b[l], 'lru_w_a': lru_w_a[l], 'lru_b_a': lru_b_a[l],
            'lru_w_x': lru_w_x[l], 'lru_b_x': lru_b_x[l], 'lru_lambda': lru_lambda[l],
            'w_branch': wbr_b, 'w_out': wout_b, 'wq_hi': wq_hi, 'wq_lo': wq_lo,
            'keys_hi': keys_hi, 'keys_lo': keys_lo, 'peer_u': u_b, 'peer_vt': vt_b,
        }

    cond = jnp.concatenate([c, c_ctx[None, :]], axis=0).astype(F32)
    n_dec = c.shape[0]
    mod = _modulation(cond, w_mod, b_mod)
    final_g = final_norm_g.reshape(1, D_MODEL)

    bp, tp, _ = x_prompt.shape
    x = x_prompt.reshape(bp * tp, D_MODEL)
    ctx_states = []
    for l in range(DEPTH):
        mod_l = mod[l, n_dec:n_dec + 1].reshape(1, 1, 6 * D_MODEL)
        x, st = _layer(x, bp, tp, mod_l, layer_params(l), None, True)
        ctx_states.append(st)
    y_prompt = _final_norm(x, final_g).reshape(bp, tp, D_MODEL)
    new_states = tuple(jnp.stack([s[i] for s in ctx_states], axis=1) for i in range(6))

    bd, td, _ = x_sample.shape
    x = _add_position(x_sample, _position_code(td // GRID_W)).reshape(bd * td, D_MODEL)
    for l in range(DEPTH):
        cached = (state_mlstm_C[:, l].astype(F32), state_mlstm_n[:, l].astype(F32),
                  state_mlstm_m[:, l].astype(F32), state_hgrn_S[:, l].astype(F32),
                  state_gla_S[:, l].astype(F32), state_lru_h[:, l].astype(F32))
        mod_l = mod[l, 0:n_dec].reshape(n_dec, 1, 6 * D_MODEL)
        x, _ = _layer(x, bd, td, mod_l, layer_params(l), cached, False)
    y_sample = _final_norm(x, final_g).reshape(bd, td, D_MODEL)
    return (y_prompt, y_sample) + new_states
```

```python
import functools

import jax
import jax.numpy as jnp
from jax import lax
from jax.experimental import pallas as pl
from jax.experimental.pallas import tpu as pltpu

F32 = jnp.float32
BF16 = jnp.bfloat16
HIGHEST = lax.Precision.HIGHEST

D_MODEL = 1024
DEPTH = 4
GRID_W = 64
EPS = 1e-6
NEG_BIG = -1e30
MAX_EXP_ARG = 80.0
MIX_W = 512
N_HEAD = 4
DH = 128
DK_C = 64
R_C = 16
GLA_TAU = 16.0
CONV_W = 4
LRU_C = 8.0
N_BRANCH = 4
N_KEYS = 128
N_EXPERTS = N_KEYS * N_KEYS
PEER_HEADS = 8
PEER_TOPK = 16
POS_BASE = 10000.0

LANE = 128
SUBLANES = 8
VMEM_LIMIT = 56 * 1024 * 1024

COL_GATES = 0
COL_AQ, COL_AK, COL_AV, COL_AO = 4096, 4608, 5120, 5632
COL_BQ, COL_BI, COL_BG = 6144, 6656, 7168
COL_CQ, COL_CK, COL_CV, COL_CG = 7680, 7936, 8192, 8704
COL_DG = 9216
N_WIDE = 9728
COL_BF, COL_DX = 0, 1024
COL_SMALL = 1536
SMALL_W = 256
N_FINE = COL_SMALL + SMALL_W

_WIDE_SRC = ((7216, 11312), (0, 2048), (2064, 2576), (3600, 4624), (4624, 6160), (6704, 7216))
_FINE_SRC = ((2576, 3600), (6192, 6704), (2048, 2064), (6160, 6192))

CHUNK_A = 128
CHUNK_G = 64
SUB_G = 16
CHUNK_D = 128


def _cparams(sem):
    return pltpu.CompilerParams(dimension_semantics=sem, vmem_limit_bytes=VMEM_LIMIT)


def _bdot(a, b):
    return jnp.dot(a.astype(BF16), b.astype(BF16), preferred_element_type=F32)


def _bdot_nt(a, b):
    return lax.dot_general(a.astype(BF16), b.astype(BF16), (((1,), (1,)), ((), ())),
                           preferred_element_type=F32)


def _bdot_tn(a, b):
    return lax.dot_general(a.astype(BF16), b.astype(BF16), (((0,), (0,)), ((), ())),
                           preferred_element_type=F32)


def _split(a):
    hi = a.astype(BF16)
    lo = (a - hi.astype(F32)).astype(BF16)
    return hi, lo


def _dot3(a, b_hi, b_lo):
    a_hi, a_lo = _split(a)
    return (jnp.dot(a_hi, b_hi, preferred_element_type=F32)
            + jnp.dot(a_lo, b_hi, preferred_element_type=F32)
            + jnp.dot(a_hi, b_lo, preferred_element_type=F32))


def _log_sigmoid(z):
    return jnp.minimum(z, 0.0) - jnp.log1p(jnp.exp(-jnp.abs(z)))


def _sigmoid(z):
    return 0.5 * jnp.tanh(0.5 * z) + 0.5


def _gelu(x):
    return 0.5 * x * (1.0 + jnp.tanh(0.7978845608028654 * (x + 0.044715 * x * x * x)))


def _silu(x):
    return x * _sigmoid(x)


def _expm1(y):
    u = jnp.exp(y)
    near = (u - 1.0) * y / jnp.log(u)
    return jnp.where(u == 1.0, y, jnp.where(u < 0.5, u - 1.0, near))


def _mod_kernel(c_ref, w_ref, b_ref, o_ref):
    a = _silu(c_ref[...])
    o_ref[0] = jnp.dot(a, w_ref[0], precision=HIGHEST, preferred_element_type=F32) + b_ref[0]


def _modulation(cond, w_mod, b_mod):
    r = cond.shape[0]
    tc = 1536
    return pl.pallas_call(
        _mod_kernel,
        grid=(DEPTH, 6 * D_MODEL // tc),
        in_specs=[pl.BlockSpec((r, D_MODEL), lambda l, j: (0, 0)),
                  pl.BlockSpec((1, D_MODEL, tc), lambda l, j: (l, 0, j)),
                  pl.BlockSpec((1, 1, tc), lambda l, j: (l, 0, j))],
        out_specs=pl.BlockSpec((1, r, tc), lambda l, j: (l, 0, j)),
        out_shape=jax.ShapeDtypeStruct((DEPTH, r, 6 * D_MODEL), F32),
        compiler_params=_cparams(("parallel", "parallel")),
        name="modulation",
    )(cond, w_mod, b_mod.reshape(DEPTH, 1, 6 * D_MODEL))


def _addpos_kernel(x_ref, p_ref, o_ref):
    o_ref[0] = x_ref[0] + p_ref[...]


def _add_position(x, pos):
    b, t, d = x.shape
    tt = 512
    return pl.pallas_call(
        _addpos_kernel,
        grid=(b, t // tt),
        in_specs=[pl.BlockSpec((1, tt, d), lambda i, j: (i, j, 0)),
                  pl.BlockSpec((tt, d), lambda i, j: (j, 0))],
        out_specs=pl.BlockSpec((1, tt, d), lambda i, j: (i, j, 0)),
        out_shape=jax.ShapeDtypeStruct(x.shape, F32),
        compiler_params=_cparams(("parallel", "parallel")),
        name="add_position",
    )(x, pos)


def _inproj_kernel(x_ref, mod_ref, g_ref, w_ref, b_ref, o_ref, h_ref):
    @pl.when(pl.program_id(1) == 0)
    def _():
        x = x_ref[...]
        inv = lax.rsqrt(jnp.mean(x * x, axis=-1, keepdims=True) + EPS)
        sh = mod_ref[0, :, 0:D_MODEL]
        sc = mod_ref[0, :, D_MODEL:2 * D_MODEL]
        h_ref[...] = (x * inv * g_ref[...] * (1.0 + sc) + sh).astype(BF16)

    acc = jnp.dot(h_ref[...], w_ref[...], preferred_element_type=F32) + b_ref[...]
    o_ref[...] = acc.astype(o_ref.dtype)


def _inproj(x, mod_l, norm_g, w_pack, b_pack, seq_len, layer, tc, out_dtype, name):
    n = x.shape[0]
    width = w_pack.shape[-1]
    tn = 1024
    rows = mod_l.shape[0]
    if rows == 1:
        mod_map = lambda i, j: (0, 0, 0)
    else:
        mod_map = lambda i, j: ((i * tn) // seq_len, 0, 0)
    return pl.pallas_call(
        _inproj_kernel,
        grid=(n // tn, width // tc),
        in_specs=[pl.BlockSpec((tn, D_MODEL), lambda i, j: (i, 0)),
                  pl.BlockSpec((1, 1, 6 * D_MODEL), mod_map),
                  pl.BlockSpec((1, D_MODEL), lambda i, j: (0, 0)),
                  pl.BlockSpec((None, D_MODEL, tc), lambda i, j: (layer, 0, j)),
                  pl.BlockSpec((1, tc), lambda i, j: (0, j))],
        out_specs=pl.BlockSpec((tn, tc), lambda i, j: (i, j)),
        out_shape=jax.ShapeDtypeStruct((n, width), out_dtype),
        scratch_shapes=[pltpu.VMEM((tn, D_MODEL), BF16)],
        compiler_params=_cparams(("parallel", "arbitrary")),
        name=name,
    )(x, mod_l, norm_g, w_pack, b_pack)


SCAN_GROUP = 4


def _lockstep(gens):
    alive = list(gens)
    while alive:
        still = []
        for gen in alive:
            try:
                next(gen)
                still.append(gen)
            except StopIteration:
                pass
        alive = still
        if alive:
            yield


def _per_sequence(body, n_group, shared):
    def kern(*refs):
        def run(phase):
            stages = [body(*[r if i in shared else r.at[g] for i, r in enumerate(refs)], phase=phase)
                      for g in range(n_group)]
            for _ in _lockstep([s for s in stages if s is not None]):
                pass

        pl.when(pl.program_id(1) == 0)(lambda: run('init'))
        run('main')
        pl.when(pl.program_id(1) == pl.num_programs(1) - 1)(lambda: run('emit'))
    return kern


def _cumsum_rows(x, rev):
    k = 1
    while k < x.shape[0]:
        x = x + _shift_rows(x, k, rev, 0.0)
        k *= 2
    return x


def _causal_mask(length, rev):
    row = lax.broadcasted_iota(jnp.int32, (length, length), 0)
    col = lax.broadcasted_iota(jnp.int32, (length, length), 1)
    return (col >= row) if rev else (col <= row)


def _mlstm_kernel(*refs, has_init, emit_state, phase):
    L = CHUNK_A
    ins = list(refs)
    dirs = [ins[0:4], ins[4:8]]
    pos = 8
    if has_init:
        c0_ref, n0_ref, m0_ref = ins[pos:pos + 3]
        pos += 3
    outs = ins[pos:pos + 2]
    pos += 2
    if emit_state:
        cout_ref, nout_ref, mout_ref = ins[pos:pos + 3]
        pos += 3
    cs_ref, ns_ref, ms_ref = ins[pos:pos + 3]

    if phase == 'init':
        if has_init:
            cs_ref[...] = c0_ref[...]
            ns_ref[...] = n0_ref[...]
            ms_ref[...] = m0_ref[...]
        else:
            cs_ref[...] = jnp.zeros_like(cs_ref)
            ns_ref[...] = jnp.zeros_like(ns_ref)
            ms_ref[...] = jnp.zeros_like(ms_ref)
        return
    if phase == 'emit':
        if emit_state:
            cout_ref[...] = cs_ref[...]
            nout_ref[...] = ns_ref[...]
            mout_ref[...] = ms_ref[...]
        return

    def direction(d):
        rev = d == 1
        q_ref, k_ref, v_ref, s_ref = dirs[d]
        o_ref = outs[d]
        mask = _causal_mask(L, rev)
        sm = s_ref[:, 0:LANE]
        lf = _log_sigmoid(sm)
        bc = jnp.dot(mask.astype(F32), lf, precision=HIGHEST,
                     preferred_element_type=F32)
        sm_t = sm.T
        bc_t = bc.T
        last = 0 if rev else L - 1
        heads = range(N_HEAD)
        hsl = [slice(h * DH, (h + 1) * DH) for h in heads]
        yield
        li_c, b_c, logw, m_prev, m_t, w_state = [], [], [], [], [], []
        for h in heads:
            ci, cf = d * N_HEAD + h, 2 * N_HEAD + d * N_HEAD + h
            li_c.append(sm[:, ci:ci + 1])
            b_c.append(bc[:, cf:cf + 1])
            li_r, b_r = sm_t[ci:ci + 1, :], bc_t[cf:cf + 1, :]
            m_prev.append(ms_ref[d, h][:, 0:1])
            logw.append(jnp.where(mask, b_c[h] - b_r + li_r, NEG_BIG))
            from_state = b_c[h] + m_prev[h]
            m_t.append(jnp.maximum(from_state, jnp.max(logw[h], axis=-1, keepdims=True)))
            w_state.append(jnp.exp(from_state - m_t[h]))
        q = [q_ref[:, hsl[h]].astype(F32) for h in heads]
        k = [k_ref[:, hsl[h]].astype(F32) * (DH ** -0.5) for h in heads]
        v = [v_ref[:, hsl[h]] for h in heads]
        c_st = [cs_ref[d, h] for h in heads]
        n_st = [ns_ref[d, h] for h in heads]
        qk = [_bdot_nt(q[h], k[h]) for h in heads]
        qc = [_bdot(q[h], c_st[h]) for h in heads]
        yield
        for h in heads:
            scores = qk[h] * jnp.exp(logw[h] - m_t[h])
            num = w_state[h] * qc[h] + _bdot(scores, v[h])
            den = (w_state[h] * jnp.sum(q[h] * n_st[h], axis=-1, keepdims=True)
                   + jnp.sum(scores, axis=-1, keepdims=True))
            floor = jnp.exp(jnp.minimum(-m_t[h], MAX_EXP_ARG))
            o_ref[:, hsl[h]] = num / jnp.maximum(jnp.abs(den), floor)
        yield
        for h in heads:
            m_new = m_t[h][last:last + 1, :]
            b_last = b_c[h][last:last + 1, :]
            kw = k[h] * jnp.exp(b_last - b_c[h] + li_c[h] - m_new)
            decay = jnp.exp(b_last + m_prev[h] - m_new)
            cs_ref[d, h] = decay * c_st[h] + _bdot_tn(kw, v[h])
            ns_ref[d, h] = decay * n_st[h] + jnp.sum(kw, axis=0, keepdims=True)
            ms_ref[d, h] = jnp.broadcast_to(m_new, (1, LANE))

    yield from _lockstep([direction(0), direction(1)])


def _scan_maps(nc):
    fwd = lambda cb: (lambda b, c: (b, c, cb))
    bwd = lambda cb: (lambda b, c: (b, nc - 1 - c, cb))
    return fwd, bwd


def _mlstm(pw, pf, bsz, seq, state, emit_state):
    L = CHUNK_A
    nc = seq // L
    n = bsz * seq
    g = min(bsz, SCAN_GROUP)
    has_init = state is not None
    pw3 = pw.reshape(bsz, seq, N_WIDE)
    pf3 = pf.reshape(bsz, seq, N_FINE)
    fwd, bwd = _scan_maps(nc)

    in_specs, args = [], []
    for mk in (fwd, bwd):
        for col in (COL_AQ, COL_AK, COL_AV):
            in_specs.append(pl.BlockSpec((g, L, MIX_W), mk(col // MIX_W)))
            args.append(pw3)
        in_specs.append(pl.BlockSpec((g, L, SMALL_W), mk(COL_SMALL // SMALL_W)))
        args.append(pf3)
    st_specs = [pl.BlockSpec((g, 2, N_HEAD, DH, DH), lambda b, c: (b, 0, 0, 0, 0)),
                pl.BlockSpec((g, 2, N_HEAD, 1, DH), lambda b, c: (b, 0, 0, 0, 0)),
                pl.BlockSpec((g, 2, N_HEAD, 1, LANE), lambda b, c: (b, 0, 0, 0, 0))]
    st_shapes = [jax.ShapeDtypeStruct((bsz, 2, N_HEAD, DH, DH), F32),
                 jax.ShapeDtypeStruct((bsz, 2, N_HEAD, 1, DH), F32),
                 jax.ShapeDtypeStruct((bsz, 2, N_HEAD, 1, LANE), F32)]
    if has_init:
        c0, n0, m0 = state
        in_specs += st_specs
        args += [c0, n0.reshape(bsz, 2, N_HEAD, 1, DH),
                 jnp.broadcast_to(m0[..., None, None], (bsz, 2, N_HEAD, 1, LANE))]
    out_specs = [pl.BlockSpec((g, L, MIX_W), fwd(0)), pl.BlockSpec((g, L, MIX_W), bwd(0))]
    out_shape = [jax.ShapeDtypeStruct((bsz, seq, MIX_W), F32)] * 2
    if emit_state:
        out_specs += st_specs
        out_shape += st_shapes
    body = functools.partial(_mlstm_kernel, has_init=has_init, emit_state=emit_state)
    res = pl.pallas_call(
        _per_sequence(body, g, ()),
        grid=(bsz // g, nc),
        in_specs=in_specs,
        out_specs=out_specs,
        out_shape=out_shape,
        scratch_shapes=[pltpu.VMEM((g, 2, N_HEAD, DH, DH), F32),
                        pltpu.VMEM((g, 2, N_HEAD, 1, DH), F32),
                        pltpu.VMEM((g, 2, N_HEAD, 1, LANE), F32)],
        compiler_params=_cparams(("parallel", "arbitrary")),
        name="mlstm_scan",
    )(*args)
    o_f, o_b = res[0].reshape(n, MIX_W), res[1].reshape(n, MIX_W)
    new_state = None
    if emit_state:
        new_state = (res[2], res[3].reshape(bsz, 2, N_HEAD, DH), res[4][:, :, :, 0, 0])
    return o_f, o_b, new_state


def _gla_direction(q, k, la, v, s_ref, d, o_ref, *, rev, dk):
    L, S = CHUNK_G, SUB_G
    nsub = L // S
    mask = _causal_mask(L, rev)
    b = _cumsum_rows(la, rev)
    bx = b - la
    ref_rows = [bx[i * S + (S - 1 if rev else 0):i * S + (S - 1 if rev else 0) + 1, :]
                for i in range(nsub)]
    bref = jnp.concatenate([jnp.broadcast_to(r, (S, r.shape[1])) for r in ref_rows], axis=0)
    q_hat = q * jnp.exp(b - bref)
    rowid = lax.broadcasted_iota(jnp.int32, (L, 1), 0)
    k_hat = []
    for i in range(nsub):
        seen = (rowid >= i * S) if rev else (rowid < (i + 1) * S)
        k_hat.append(k * jnp.exp(jnp.where(seen, ref_rows[i] - b, NEG_BIG)))
    last = 0 if rev else L - 1
    b_last = b[last:last + 1, :]
    q_state = q * jnp.exp(b)
    k_state = k * jnp.exp(b_last - b)
    e_last = jnp.exp(b_last)
    yield
    ksl = [slice(h * dk, (h + 1) * dk) for h in range(N_HEAD)]
    vsl = [slice(h * DH, (h + 1) * DH) for h in range(N_HEAD)]
    blocks = [[_bdot_nt(q_hat[i * S:(i + 1) * S, ksl[h]], k_hat[i][:, ksl[h]]) for i in range(nsub)]
              for h in range(N_HEAD)]
    states = [s_ref[d, h] for h in range(N_HEAD)]
    inter = [_bdot_nt(q_state[:, ksl[h]], states[h]) for h in range(N_HEAD)]
    grown = [_bdot_tn(v[:, vsl[h]], k_state[:, ksl[h]]) for h in range(N_HEAD)]
    yield
    for h in range(N_HEAD):
        scores = jnp.where(mask, jnp.concatenate(blocks[h], axis=0), 0.0)
        o_ref[:, vsl[h]] = inter[h] + _bdot(scores, v[:, vsl[h]])
    yield
    for h in range(N_HEAD):
        s_ref[d, h] = states[h] * e_last[:, ksl[h]] + grown[h]


def _scan_state_io(ins, pos, has_init, emit_state, n_out):
    s0_ref = sout_ref = None
    if has_init:
        s0_ref = ins[pos]
        pos += 1
    outs = ins[pos:pos + n_out]
    pos += n_out
    if emit_state:
        sout_ref = ins[pos]
        pos += 1
    return s0_ref, outs, sout_ref, ins[pos]


def _scan_state_phase(phase, s_ref, s0_ref, sout_ref):
    if phase == 'init':
        if s0_ref is not None:
            s_ref[...] = s0_ref[...]
        else:
            s_ref[...] = jnp.zeros_like(s_ref)
    elif phase == 'emit' and sout_ref is not None:
        sout_ref[...] = s_ref[...]
    return phase != 'main'


def _hgrn_kernel(*refs, has_init, emit_state, phase):
    ins = list(refs)
    lb_ref = ins[6]
    s0_ref, outs, sout_ref, s_ref = _scan_state_io(ins, 7, has_init, emit_state, 2)
    if _scan_state_phase(phase, s_ref, s0_ref, sout_ref):
        return
    chains = []
    for d in range(2):
        q_ref, z_ref, v_ref = ins[3 * d:3 * d + 3]
        z = z_ref[...]
        lb = lb_ref[d:d + 1, :]
        la = _log_sigmoid(z) + jnp.log1p(lb * jnp.exp(jnp.minimum(-z, MAX_EXP_ARG)))
        k = (1.0 - lb) * _sigmoid(-z)
        q = q_ref[...].astype(F32) * (DH ** -0.5)
        chains.append(_gla_direction(q, k, la, v_ref[...], s_ref, d, outs[d], rev=d == 1, dk=DH))
    yield from _lockstep(chains)


def _gla_kernel(*refs, has_init, emit_state, phase):
    ins = list(refs)
    up_ref, upb_ref = ins[8], ins[9]
    s0_ref, outs, sout_ref, s_ref = _scan_state_io(ins, 10, has_init, emit_state, 2)
    if _scan_state_phase(phase, s_ref, s0_ref, sout_ref):
        return
    chains = []
    for d in range(2):
        q_ref, k_ref, v_ref, sm_ref = ins[4 * d:4 * d + 4]
        zg = jnp.dot(sm_ref[...], up_ref[d], precision=HIGHEST,
                     preferred_element_type=F32) + upb_ref[d:d + 1, :]
        la = _log_sigmoid(zg) * (1.0 / GLA_TAU)
        q = q_ref[...].astype(F32) * (DK_C ** -0.5)
        k = k_ref[...].astype(F32)
        chains.append(_gla_direction(q, k, la, v_ref[...], s_ref, d, outs[d], rev=d == 1,
                                     dk=DK_C))
    yield from _lockstep(chains)


def _gated_scan(kernel, p, bsz, seq, cols, extra, extra_specs, dk, state_t, emit_state, name):
    L = CHUNK_G
    nc = seq // L
    n = bsz * seq
    g = min(bsz, SCAN_GROUP)
    has_init = state_t is not None
    src = {'wide': p[0].reshape(bsz, seq, N_WIDE), 'fine': p[1].reshape(bsz, seq, N_FINE)}
    fwd, bwd = _scan_maps(nc)

    in_specs, args = [], []
    for d, mk in enumerate((fwd, bwd)):
        for which, off, width in cols[d]:
            in_specs.append(pl.BlockSpec((g, L, width), mk(off // width)))
            args.append(src[which])
    shared = tuple(range(len(args), len(args) + len(extra)))
    in_specs += extra_specs
    args += extra
    st_spec = pl.BlockSpec((g, 2, N_HEAD, DH, dk), lambda b, c: (b, 0, 0, 0, 0))
    if has_init:
        in_specs.append(st_spec)
        args.append(state_t)
    out_specs = [pl.BlockSpec((g, L, MIX_W), fwd(0)), pl.BlockSpec((g, L, MIX_W), bwd(0))]
    out_shape = [jax.ShapeDtypeStruct((bsz, seq, MIX_W), F32)] * 2
    if emit_state:
        out_specs.append(st_spec)
        out_shape.append(jax.ShapeDtypeStruct((bsz, 2, N_HEAD, DH, dk), F32))
    body = functools.partial(kernel, has_init=has_init, emit_state=emit_state)
    res = pl.pallas_call(
        _per_sequence(body, g, shared),
        grid=(bsz // g, nc),
        in_specs=in_specs,
        out_specs=out_specs,
        out_shape=out_shape,
        scratch_shapes=[pltpu.VMEM((g, 2, N_HEAD, DH, dk), F32)],
        compiler_params=_cparams(("parallel", "arbitrary")),
        name=name,
    )(*args)
    return res[0].reshape(n, MIX_W), res[1].reshape(n, MIX_W), (res[2] if emit_state else None)


def _hgrn(p, bsz, seq, lb, state_t, emit_state):
    cols = [[('wide', COL_BQ, MIX_W), ('fine', COL_BF + d * MIX_W, MIX_W), ('wide', COL_BI, MIX_W)]
            for d in range(2)]
    return _gated_scan(_hgrn_kernel, p, bsz, seq, cols, [lb],
                       [pl.BlockSpec((2, MIX_W), lambda b, c: (0, 0))], DH, state_t, emit_state,
                       "hgrn2_scan")


def _gla(p, bsz, seq, up_pad, up_b, state_t, emit_state):
    kw = N_HEAD * DK_C
    cols = [[('wide', COL_CQ, kw), ('wide', COL_CK, kw), ('wide', COL_CV, MIX_W),
             ('fine', COL_SMALL, SMALL_W)] for _ in range(2)]
    return _gated_scan(_gla_kernel, p, bsz, seq, cols, [up_pad, up_b],
                       [pl.BlockSpec((2, SMALL_W, kw), lambda b, c: (0, 0, 0)),
                        pl.BlockSpec((2, kw), lambda b, c: (0, 0))], DK_C, state_t, emit_state,
                       "gla_scan")


def _shift_rows(x, k, rev, fill):
    n = x.shape[0]
    rowid = lax.broadcasted_iota(jnp.int32, x.shape, 0)
    if rev:
        return jnp.where(rowid >= n - k, fill, pltpu.roll(x, n - k, 0))
    return jnp.where(rowid < k, fill, pltpu.roll(x, k, 0))


def _lru_kernel(*refs, seq, has_init, emit_state):
    L = CHUNK_D
    nc = seq // L
    ins = list(refs)
    (dx_ref, dg_ref, cw_ref, cb_ref, wah_ref, wal_ref, ba_ref, wxh_ref, wxl_ref, bx_ref,
     lam_ref) = ins[0:11]
    pos = 11
    h0_ref = hout_ref = None
    if has_init:
        h0_ref = ins[pos]
        pos += 1
    y_ref = ins[pos]
    pos += 1
    if emit_state:
        hout_ref = ins[pos]
        pos += 1
    pad_ref, hb_ref = ins[pos], ins[pos + 1]

    zeros8 = jnp.zeros((8, LANE), F32)
    pad_ref[0:8, :] = zeros8
    pad_ref[8:8 + seq, :] = dx_ref[...]
    pad_ref[8 + seq:16 + seq, :] = zeros8
    lam = lam_ref[...]
    sp = jnp.maximum(-lam, 0.0) + jnp.log1p(jnp.exp(-jnp.abs(lam)))
    cw = cw_ref[...]
    cb = cb_ref[...]

    def chunk_scan(c, carry, d, out):
        rev = d == 1
        start = pl.multiple_of(c * L, L)
        win = pad_ref[pl.ds(start, L + 16), :]
        xd = cb + sum(cw[j:j + 1, :] * win[7 + j:7 + j + L, :] for j in range(CONV_W))
        yield
        za = _dot3(xd, wah_ref[d, 0], wal_ref[d, 0])
        zx = _dot3(xd, wxh_ref[d, 0], wxl_ref[d, 0])
        yield
        r = _sigmoid(za + ba_ref[d:d + 1, :])
        ig = _sigmoid(zx + bx_ref[d:d + 1, :])
        log_a = -LRU_C * r * sp[d:d + 1, :]
        a = jnp.exp(log_a)
        u = jnp.sqrt(jnp.maximum(-_expm1(2.0 * log_a), 0.0)) * (ig * xd)
        yield
        k = 1
        while k < L:
            u = a * _shift_rows(u, k, rev, 0.0) + u
            a = a * _shift_rows(a, k, rev, 1.0)
            k *= 2
            yield
        h = a * carry + u
        last = 0 if rev else L - 1
        out.extend((start, h, h[last:last + 1, :]))

    def scan_body(i, carry):
        res_f, res_b = [], []
        for _ in _lockstep([chunk_scan(i, carry[0], 0, res_f),
                            chunk_scan(nc - 1 - i, carry[1], 1, res_b)]):
            pass
        y_ref[pl.ds(res_f[0], L), :] = res_f[1]
        hb_ref[pl.ds(res_b[0], L), :] = res_b[1]
        return res_f[2], res_b[2]

    def gate_body(c, carry):
        rows = pl.ds(pl.multiple_of(c * L, L), L)
        y_ref[rows, :] = (y_ref[rows, :] + hb_ref[rows, :]) * _gelu(dg_ref[rows, :].astype(F32))
        return carry

    if has_init:
        init = (h0_ref[0, 0:1, :], h0_ref[0, 1:2, :])
    else:
        init = (jnp.zeros((1, LANE), F32), jnp.zeros((1, LANE), F32))
    fin_f, fin_b = lax.fori_loop(0, nc, scan_body, init)
    lax.fori_loop(0, nc, gate_body, 0)
    if emit_state:
        hout_ref[0, 0:1, :] = fin_f
        hout_ref[0, 1:2, :] = fin_b


def _lru(pw, pf, bsz, seq, conv_w, conv_b, wa, ba, wx, bx, lam, h0, emit_state):
    n = bsz * seq
    has_init = h0 is not None
    wah, wal = _split(wa)
    wxh, wxl = _split(wx)
    gate_w = pl.BlockSpec((2, 1, DH, DH), lambda b, h: (0, h, 0, 0))
    vec2 = pl.BlockSpec((2, LANE), lambda b, h: (0, h))
    in_specs = [pl.BlockSpec((seq, LANE), lambda b, h: (b, COL_DX // LANE + h)),
                pl.BlockSpec((seq, LANE), lambda b, h: (b, COL_DG // LANE + h)),
                pl.BlockSpec((CONV_W, LANE), lambda b, h: (0, h)),
                pl.BlockSpec((1, LANE), lambda b, h: (0, h)),
                gate_w, gate_w, vec2, gate_w, gate_w, vec2, vec2]
    args = [pf, pw, conv_w, conv_b.reshape(1, MIX_W), wah, wal, ba, wxh, wxl, bx, lam]
    st_spec = pl.BlockSpec((1, 2, LANE), lambda b, h: (b, 0, h))
    if has_init:
        in_specs.append(st_spec)
        args.append(h0)
    out_specs = [pl.BlockSpec((seq, LANE), lambda b, h: (b, h))]
    out_shape = [jax.ShapeDtypeStruct((n, MIX_W), F32)]
    if emit_state:
        out_specs.append(st_spec)
        out_shape.append(jax.ShapeDtypeStruct((bsz, 2, MIX_W), F32))
    res = pl.pallas_call(
        functools.partial(_lru_kernel, seq=seq, has_init=has_init, emit_state=emit_state),
        grid=(bsz, N_HEAD),
        in_specs=in_specs,
        out_specs=out_specs,
        out_shape=out_shape,
        scratch_shapes=[pltpu.VMEM((seq + 16, LANE), F32), pltpu.VMEM((seq, LANE), F32)],
        compiler_params=_cparams(("parallel", "parallel")),
        name="conv_rglru",
    )(*args)
    return res[0], (res[1] if emit_state else None)


def _head_rms(x):
    parts = []
    for h in range(N_HEAD):
        xh = x[:, h * DH:(h + 1) * DH]
        parts.append(xh * lax.rsqrt(jnp.mean(xh * xh, axis=-1, keepdims=True) + EPS))
    return jnp.concatenate(parts, axis=-1)


def _merge_kernel(af_ref, ab_ref, bf_ref, bb_ref, cf_ref, cb_ref, yd_ref, ao_ref, bg_ref, cg_ref,
                  gt0_ref, gt1_ref, gt2_ref, gt3_ref, x_ref, mod_ref, g2_ref, wbr_ref, wout_ref,
                  xo_ref, h2_ref, h2b_ref):
    ya = _sigmoid(ao_ref[...].astype(F32)) * _head_rms(af_ref[...] + ab_ref[...])
    yb = _silu(bg_ref[...].astype(F32)) * _head_rms(bf_ref[...] + bb_ref[...])
    yc = _silu(cg_ref[...].astype(F32)) * _head_rms(cf_ref[...] + cb_ref[...])
    merged = None
    gate_refs = (gt0_ref, gt1_ref, gt2_ref, gt3_ref)
    for i, y in enumerate((ya, yb, yc, yd_ref[...])):
        proj = jnp.dot(y.astype(BF16), wbr_ref[i], preferred_element_type=F32)
        term = _sigmoid(gate_refs[i][...].astype(F32)) * proj
        merged = term if merged is None else merged + term
    out = jnp.dot(merged.astype(BF16), wout_ref[...], preferred_element_type=F32)
    g1 = mod_ref[0, :, 2 * D_MODEL:3 * D_MODEL]
    sh2 = mod_ref[0, :, 3 * D_MODEL:4 * D_MODEL]
    sc2 = mod_ref[0, :, 4 * D_MODEL:5 * D_MODEL]
    x = x_ref[...] + g1 * out
    xo_ref[...] = x
    inv = lax.rsqrt(jnp.mean(x * x, axis=-1, keepdims=True) + EPS)
    h2 = x * inv * g2_ref[...] * (1.0 + sc2) + sh2
    h2_ref[...] = h2
    h2b_ref[...] = h2.astype(BF16)


def _merge(mix_outs, yd, p, x, mod_l, norm2_g, w_branch, w_out, seq_len, layer):
    n = x.shape[0]
    tn = 256
    rows = mod_l.shape[0]
    mod_map = (lambda i: (0, 0, 0)) if rows == 1 else (lambda i: ((i * tn) // seq_len, 0, 0))
    tok = lambda cb: (lambda i: (i, cb))
    in_specs = [pl.BlockSpec((tn, MIX_W), tok(0))] * 7
    in_specs += [pl.BlockSpec((tn, MIX_W), tok(COL_AO // MIX_W)),
                 pl.BlockSpec((tn, MIX_W), tok(COL_BG // MIX_W)),
                 pl.BlockSpec((tn, MIX_W), tok(COL_CG // MIX_W)),
                 *[pl.BlockSpec((tn, D_MODEL), tok(COL_GATES // D_MODEL + i)) for i in range(N_BRANCH)],
                 pl.BlockSpec((tn, D_MODEL), tok(0)),
                 pl.BlockSpec((1, 1, 6 * D_MODEL), mod_map),
                 pl.BlockSpec((1, D_MODEL), lambda i: (0, 0)),
                 pl.BlockSpec((None, N_BRANCH, MIX_W, D_MODEL), lambda i: (layer, 0, 0, 0)),
                 pl.BlockSpec((None, D_MODEL, D_MODEL), lambda i: (layer, 0, 0))]
    out_spec = pl.BlockSpec((tn, D_MODEL), tok(0))
    return pl.pallas_call(
        _merge_kernel,
        grid=(n // tn,),
        in_specs=in_specs,
        out_specs=[out_spec, out_spec, out_spec],
        out_shape=[jax.ShapeDtypeStruct((n, D_MODEL), F32), jax.ShapeDtypeStruct((n, D_MODEL), F32),
                   jax.ShapeDtypeStruct((n, D_MODEL), BF16)],
        compiler_params=_cparams(("parallel",)),
        name="branch_merge",
    )(*mix_outs, yd, p, p, p, p, p, p, p, x, mod_l, norm2_g, w_branch, w_out)


ROUTE_TN = 512
ROUTE_LG = LANE


def _top16_exact(vals):
    n_rows = vals.shape[0]
    rowid = lax.broadcasted_iota(jnp.int32, vals.shape, 0).astype(F32)
    rank = jnp.full(vals.shape, float(PEER_TOPK), F32)
    tops = []
    for r in range(PEER_TOPK):
        m = jnp.max(vals, axis=0, keepdims=True)
        idx = jnp.min(jnp.where(vals == m, rowid, float(n_rows)), axis=0, keepdims=True)
        sel = rowid == idx
        rank = jnp.where(sel, float(r), rank)
        vals = jnp.where(sel, -jnp.inf, vals)
        tops.append(m)
    return jnp.concatenate(tops, axis=0), rank


_TAKEN_BASE = -3.0e38
_TAKEN_STEP = 2.0e36
_TAKEN_BELOW = _TAKEN_BASE + 0.5 * _TAKEN_STEP


def _top16_quick(vals):
    return _top16_quick_many([vals])[0]


def _top16_quick_many(vals_list):
    work = list(vals_list)
    tops = [[] for _ in work]
    for r in range(PEER_TOPK):
        for i in range(len(work)):
            m = jnp.max(work[i], axis=0, keepdims=True)
            work[i] = jnp.where(work[i] == m, _TAKEN_BASE - r * _TAKEN_STEP, work[i])
            tops[i].append(m)
    out = []
    for i in range(len(work)):
        taken = work[i] <= _TAKEN_BELOW
        rank = jnp.where(taken, jnp.round((_TAKEN_BASE - work[i]) * (1.0 / _TAKEN_STEP)),
                         float(PEER_TOPK))
        n_taken = jnp.sum(jnp.where(taken, 1.0, 0.0), axis=0, keepdims=True)
        out.append((jnp.concatenate(tops[i], axis=0), rank, n_taken))
    return out


_CAND_PIECES = (('a', 0, 0, 16), ('a', 0, 8, 16), ('a', 1, 0, 8), ('a', 2, 0, 5), ('a', 3, 0, 4),
                ('b', 0, 0, (4, 8)), ('b', 0, 8, (8, 16)), ('b', 1, 0, (4, 8)), ('b', 2, 0, (4, 5)))


def _pair_pieces(x1, x2, combine):
    out = []
    for kind, fixed, off, _ in _CAND_PIECES:
        if kind == 'a':
            out.append(combine(x1[fixed:fixed + 1, :], x2[off:off + 8, :]))
        else:
            out.append(combine(x1[off:off + 8, :], x2[fixed:fixed + 1, :]))
    return out


def _candidate_sums(s1, s2):
    K = PEER_TOPK
    n = s1.shape[1]
    iota8 = lax.broadcasted_iota(jnp.int32, (8, n), 0)
    sums = _pair_pieces(s1, s2, lambda x, y: x + y)
    vals, poss = [], []
    for (kind, fixed, off, lim), sm in zip(_CAND_PIECES, sums):
        idx = iota8 + off
        if kind == 'a':
            valid = idx < lim
            pos = fixed * K + idx
        else:
            valid = (idx >= lim[0]) & (idx < lim[1])
            pos = idx * K + fixed
        vals.append(jnp.where(valid, sm, -jnp.inf))
        poss.append(jnp.where(valid, pos, K * K).astype(F32))
    return jnp.concatenate(vals, axis=0), jnp.concatenate(poss, axis=0)


def _choose_exact(vals, posid):
    K = PEER_TOPK
    chosen = jnp.zeros(vals.shape, F32)
    for _ in range(K):
        m = jnp.max(vals, axis=0, keepdims=True)
        idx = jnp.min(jnp.where(vals == m, posid, float(K * K)), axis=0, keepdims=True)
        sel = posid == idx
        chosen = jnp.where(sel, 1.0, chosen)
        vals = jnp.where(sel, -jnp.inf, vals)
    return chosen


def _choose_quick(vals):
    return _choose_quick_many([vals])[0]


def _choose_quick_many(vals_list):
    work = list(vals_list)
    for _ in range(PEER_TOPK):
        for i in range(len(work)):
            m = jnp.max(work[i], axis=0, keepdims=True)
            work[i] = jnp.where(work[i] == m, _TAKEN_BASE, work[i])
    out = []
    for w in work:
        chosen = jnp.where(w == _TAKEN_BASE, 1.0, 0.0)
        out.append((chosen, jnp.sum(chosen, axis=0, keepdims=True)))
    return out


def _route_kernel(h_ref, wqh_ref, wql_ref, kh_ref, kl_ref, a_ref, b_ref, r2_ref, c_ref,
                  qh_ref, ql_ref, s_ref, top_ref, rank_ref, ch_ref):
    K = PEER_TOPK
    n_lg = ROUTE_TN // ROUTE_LG
    q = _dot3(h_ref[...], wqh_ref[...], wql_ref[...])
    q_hi, q_lo = _split(q)
    for i in range(2 * PEER_HEADS):
        qh_ref[i] = q_hi[:, i * N_KEYS:(i + 1) * N_KEYS]
        ql_ref[i] = q_lo[:, i * N_KEYS:(i + 1) * N_KEYS]
    nt = (((1,), (1,)), ((), ()))
    lanes = [slice(lg * ROUTE_LG, (lg + 1) * ROUTE_LG) for lg in range(n_lg)]

    def miscount(counts):
        worst = None
        for cnt in counts:
            dev = jnp.abs(cnt - float(K))
            worst = dev if worst is None else jnp.maximum(worst, dev)
        return jnp.max(worst) > 0.0

    def head_body(h, carry):
        for half in range(2):
            kh, kl = kh_ref[h, half], kl_ref[h, half]
            qh, ql = qh_ref[2 * h + half], ql_ref[2 * h + half]
            s_ref[half] = (lax.dot_general(kh, qh, nt, preferred_element_type=F32)
                           + lax.dot_general(kl, qh, nt, preferred_element_type=F32)
                           + lax.dot_general(kh, ql, nt, preferred_element_type=F32))

        problems = [(half, lg) for half in range(2) for lg in range(n_lg)]
        quick = _top16_quick_many([s_ref[half, :, lanes[lg]] for half, lg in problems])
        counts = []
        for (half, lg), (top, rank, n_taken) in zip(problems, quick):
            top_ref[half, lg] = top
            rank_ref[half, lg] = rank
            counts.append(n_taken)

        @pl.when(miscount(counts))
        def _():
            for half in range(2):
                for lg in range(n_lg):
                    top, rank = _top16_exact(s_ref[half, :, lanes[lg]])
                    top_ref[half, lg] = top
                    rank_ref[half, lg] = rank

        cands = [_candidate_sums(top_ref[0, lg], top_ref[1, lg])[0] for lg in range(n_lg)]
        counts = []
        for lg, (chosen, n_taken) in enumerate(_choose_quick_many(cands)):
            ch_ref[lg] = chosen
            counts.append(n_taken)

        @pl.when(miscount(counts))
        def _():
            for lg in range(n_lg):
                vals, posid = _candidate_sums(top_ref[0, lg], top_ref[1, lg])
                ch_ref[lg] = _choose_exact(vals, posid)

        for lg in range(n_lg):
            ls = lanes[lg]
            st1, st2 = s_ref[0, :, ls], s_ref[1, :, ls]
            s1, s2 = top_ref[0, lg], top_ref[1, lg]
            rank1, rank2 = rank_ref[0, lg], rank_ref[1, lg]
            chosen = ch_ref[lg]
            ch = [chosen[8 * i:8 * i + 8, :] for i in range(len(_CAND_PIECES))]
            e1 = jnp.exp(s1 - s1[0:1, :])
            e2 = jnp.exp(s2 - s2[0:1, :])
            pair = _pair_pieces(e1, e2, lambda x, y: x * y)
            z = sum(jnp.sum(c * p, axis=0, keepdims=True) for c, p in zip(ch, pair))
            iota8 = lax.broadcasted_iota(jnp.int32, (8, ROUTE_LG), 0)
            low = ch[5] + ch[7] + ch[8]
            for a, cnt in ((3, ch[4]), (2, ch[3]), (1, ch[2]), (0, ch[0] + ch[1])):
                low = jnp.where(iota8 == a, jnp.sum(cnt, axis=0, keepdims=True), low)
            counts = jnp.concatenate([low, ch[6]], axis=0)
            c_dense = jnp.zeros((N_KEYS, ROUTE_LG), F32)
            for a in range(K):
                c_dense = jnp.where(rank1 == float(a), counts[a:a + 1, :], c_dense)
            a_ref[h, :, ls] = jnp.where(rank1 < float(K), jnp.exp(st1 - s1[0:1, :]), 0.0) / z
            b_ref[h, :, ls] = jnp.where(rank2 < float(K), jnp.exp(st2 - s2[0:1, :]), 0.0).astype(BF16)
            r2_ref[h, :, ls] = rank2.astype(BF16)
            c_ref[h, :, ls] = c_dense
        return carry

    lax.fori_loop(0, PEER_HEADS, head_body, 0)


def _route(h2, wq_hi, wq_lo, keys_hi, keys_lo, layer):
    n = h2.shape[0]
    tn = ROUTE_TN
    dense = pl.BlockSpec((PEER_HEADS, N_KEYS, tn), lambda i: (0, 0, i))
    wspec = pl.BlockSpec((None, D_MODEL, PEER_HEADS * 2 * N_KEYS), lambda i: (layer, 0, 0))
    kspec = pl.BlockSpec((None, PEER_HEADS, 2, N_KEYS, N_KEYS), lambda i: (layer, 0, 0, 0, 0))
    return pl.pallas_call(
        _route_kernel,
        grid=(n // tn,),
        in_specs=[pl.BlockSpec((tn, D_MODEL), lambda i: (i, 0)), wspec, wspec, kspec, kspec],
        out_specs=[dense] * 4,
        out_shape=[jax.ShapeDtypeStruct((PEER_HEADS, N_KEYS, n), dt) for dt in (F32, BF16, BF16, F32)],
        scratch_shapes=[pltpu.VMEM((2 * PEER_HEADS, tn, N_KEYS), BF16),
                        pltpu.VMEM((2 * PEER_HEADS, tn, N_KEYS), BF16),
                        pltpu.VMEM((2, N_KEYS, tn), F32),
                        pltpu.VMEM((2, tn // ROUTE_LG, PEER_TOPK, ROUTE_LG), F32),
                        pltpu.VMEM((2, tn // ROUTE_LG, N_KEYS, ROUTE_LG), F32),
                        pltpu.VMEM((tn // ROUTE_LG, 8 * len(_CAND_PIECES), ROUTE_LG), F32)],
        compiler_params=_cparams(("parallel",)),
        name="peer_route",
    )(h2, wq_hi, wq_lo, keys_hi, keys_lo)


PEER_TN = 512
PEER_TE = 1024


PEER_SPLIT = 2


def _gelu_bf16(t):
    x = t.astype(BF16)
    y2 = x * (1.0 + 0.044715 * (x * x)) * (-2.0 * 0.7978845608028654)
    return x / (1.0 + jnp.exp(y2))


BF16_ROWS = 16


def _row_to_packed(row):
    tile = jnp.broadcast_to(row, (BF16_ROWS, row.shape[1])).astype(BF16)
    return jnp.concatenate([tile] * (N_KEYS // BF16_ROWS), axis=0)


def _expert_kernel(hb_ref, u_ref, vt_ref, a_ref, b_ref, r2_ref, c_ref, x_ref, mod_ref, o_ref,
                   acc_ref, g_ref):
    e = pl.program_id(1)
    rows_per_tile = PEER_TE // N_KEYS
    sub_w = PEER_TN // PEER_SPLIT

    @pl.when(e == 0)
    def _():
        acc_ref[...] = jnp.zeros_like(acc_ref)

    def lanes(sub):
        return slice(sub * sub_w, (sub + 1) * sub_w)

    def score(sub):
        return lax.dot_general(u_ref[...], hb_ref[lanes(sub), :], (((1,), (1,)), ((), ())),
                               preferred_element_type=F32)

    def gate(sub):
        for j in range(rows_per_tile):
            e1 = e * rows_per_tile + j
            g = None
            for h in range(PEER_HEADS):
                a_row = _row_to_packed(a_ref[h, pl.ds(e1, 1), lanes(sub)])
                c_row = _row_to_packed(c_ref[h, pl.ds(e1, 1), lanes(sub)])
                term = a_row * jnp.where(r2_ref[h, :, lanes(sub)] < c_row, b_ref[h, :, lanes(sub)], 0.0)
                g = term if g is None else g + term
            g_ref[sub, j * N_KEYS:(j + 1) * N_KEYS, :] = g

    def weigh(sub, t_t):
        return jnp.concatenate([g_ref[sub, j * N_KEYS:(j + 1) * N_KEYS, :]
                                * _gelu_bf16(t_t[j * N_KEYS:(j + 1) * N_KEYS, :])
                                for j in range(rows_per_tile)], axis=0)

    def project(sub, w_t):
        acc_ref[:, lanes(sub)] += jnp.dot(vt_ref[...], w_t, preferred_element_type=F32)

    gate(0)
    t_t = {0: score(0)}
    for sub in range(PEER_SPLIT):
        if sub + 1 < PEER_SPLIT:
            t_t[sub + 1] = score(sub + 1)
        project(sub, weigh(sub, t_t[sub]))
        if sub + 1 < PEER_SPLIT:
            gate(sub + 1)

    @pl.when(e == pl.num_programs(1) - 1)
    def _():
        g2 = mod_ref[0, :, 5 * D_MODEL:6 * D_MODEL]
        o_ref[...] = x_ref[...] + g2 * acc_ref[...].T


def _experts(h2b, u_b, vt_b, dense, x, mod_l, seq_len, layer):
    n = x.shape[0]
    tn, te = PEER_TN, PEER_TE
    rows = mod_l.shape[0]
    mod_map = (lambda i, e: (0, 0, 0)) if rows == 1 else (lambda i, e: ((i * tn) // seq_len, 0, 0))
    dspec = pl.BlockSpec((PEER_HEADS, N_KEYS, tn), lambda i, e: (0, 0, i))
    return pl.pallas_call(
        _expert_kernel,
        grid=(n // tn, N_EXPERTS // te),
        in_specs=[pl.BlockSpec((tn, D_MODEL), lambda i, e: (i, 0)),
                  pl.BlockSpec((None, te, D_MODEL), lambda i, e: (layer, e, 0)),
                  pl.BlockSpec((None, D_MODEL, te), lambda i, e: (layer, 0, e)),
                  dspec, dspec, dspec, dspec,
                  pl.BlockSpec((tn, D_MODEL), lambda i, e: (i, 0)),
                  pl.BlockSpec((1, 1, 6 * D_MODEL), mod_map)],
        out_specs=pl.BlockSpec((tn, D_MODEL), lambda i, e: (i, 0)),
        out_shape=jax.ShapeDtypeStruct((n, D_MODEL), F32),
        scratch_shapes=[pltpu.VMEM((D_MODEL, tn), F32),
                        pltpu.VMEM((PEER_SPLIT, te, tn // PEER_SPLIT), BF16)],
        compiler_params=_cparams(("parallel", "arbitrary")),
        name="peer_experts",
    )(h2b, u_b, vt_b, *dense, x, mod_l)


def _final_norm_kernel(x_ref, g_ref, o_ref):
    x = x_ref[...]
    o_ref[...] = x * lax.rsqrt(jnp.mean(x * x, axis=-1, keepdims=True) + EPS) * g_ref[...]


def _final_norm(x, g):
    n = x.shape[0]
    tn = 512
    return pl.pallas_call(
        _final_norm_kernel,
        grid=(n // tn,),
        in_specs=[pl.BlockSpec((tn, D_MODEL), lambda i: (i, 0)),
                  pl.BlockSpec((1, D_MODEL), lambda i: (0, 0))],
        out_specs=pl.BlockSpec((tn, D_MODEL), lambda i: (i, 0)),
        out_shape=jax.ShapeDtypeStruct((n, D_MODEL), F32),
        compiler_params=_cparams(("parallel",)),
        name="final_norm",
    )(x, g)


def _pack_columns(a, src, width):
    parts = [a[..., lo:hi] for lo, hi in src]
    used = sum(hi - lo for lo, hi in src)
    if used < width:
        parts.append(jnp.zeros(a.shape[:-1] + (width - used,), a.dtype))
    return jnp.concatenate(parts, axis=-1)


def _position_code(rows):
    quarter = D_MODEL // 4
    omega = 1.0 / (POS_BASE ** (jnp.arange(quarter, dtype=F32) / quarter))
    r, col = jnp.meshgrid(jnp.arange(rows, dtype=F32), jnp.arange(GRID_W, dtype=F32), indexing='ij')

    def enc(pos):
        ang = pos.reshape(-1, 1) * omega
        return jnp.concatenate([jnp.sin(ang), jnp.cos(ang)], axis=-1)
    return jnp.concatenate([enc(r), enc(col)], axis=-1)


def _layer(x, bsz, seq, mod_l, lp, state, emit_state):
    layer = lp['layer']
    pw = _inproj(x, mod_l, lp['norm1_g'], lp['w_wide'], lp['b_wide'], seq, layer, 2432, BF16,
                 "in_projection_wide")
    pf = _inproj(x, mod_l, lp['norm1_g'], lp['w_fine'], lp['b_fine'], seq, layer, N_FINE, F32,
                 "in_projection_fine")
    if state is None:
        st_a = st_b = st_c = st_d = None
    else:
        c0, n0, m0, sb0, sc0, hd0 = state
        st_a = (c0, n0, m0)
        st_b = jnp.swapaxes(sb0, -1, -2)
        st_c = jnp.swapaxes(sc0, -1, -2)
        st_d = hd0
    a_f, a_b, new_a = _mlstm(pw, pf, bsz, seq, st_a, emit_state)
    b_f, b_b, new_b = _hgrn((pw, pf), bsz, seq, lp['hgrn_lb'], st_b, emit_state)
    c_f, c_b, new_c = _gla((pw, pf), bsz, seq, lp['gla_up_pad'], lp['gla_up_b'], st_c, emit_state)
    yd, new_d = _lru(pw, pf, bsz, seq, lp['conv_w'], lp['conv_b'], lp['lru_w_a'], lp['lru_b_a'],
                     lp['lru_w_x'], lp['lru_b_x'], lp['lru_lambda'], st_d, emit_state)
    x1, h2, h2b = _merge((a_f, a_b, b_f, b_b, c_f, c_b), yd, pw, x, mod_l, lp['norm2_g'],
                         lp['w_branch'], lp['w_out'], seq, layer)
    dense = _route(h2, lp['wq_hi'], lp['wq_lo'], lp['keys_hi'], lp['keys_lo'], layer)
    x2 = _experts(h2b, lp['peer_u'], lp['peer_vt'], dense, x1, mod_l, seq, layer)
    new_state = None
    if emit_state:
        new_state = (*new_a, jnp.swapaxes(new_b, -1, -2), jnp.swapaxes(new_c, -1, -2), new_d)
    return x2, new_state


def kernel(x_prompt, x_sample, state_mlstm_C, state_mlstm_n, state_mlstm_m, state_hgrn_S,
           state_gla_S, state_lru_h, c, c_ctx, norm1_g, norm2_g, final_norm_g, w_mod, b_mod,
           w_in, b_in, w_gla_up, b_gla_up, hgrn_lower_bounds, conv_w, conv_b, lru_w_a, lru_b_a,
           lru_w_x, lru_b_x, lru_lambda, w_branch, w_out, peer_w_q, peer_sub_keys, peer_u, peer_v):
    lb_soft = jax.nn.softmax(hgrn_lower_bounds.astype(F32), axis=0)
    lb_all = jnp.cumsum(lb_soft, axis=0) - lb_soft[0:1]
    w_wide = _pack_columns(w_in, _WIDE_SRC, N_WIDE).astype(BF16)
    w_fine = _pack_columns(w_in, _FINE_SRC, N_FINE).astype(BF16)
    b_wide = _pack_columns(b_in, _WIDE_SRC, N_WIDE).reshape(DEPTH, 1, N_WIDE)
    b_fine = _pack_columns(b_in, _FINE_SRC, N_FINE).reshape(DEPTH, 1, N_FINE)
    kw = N_HEAD * DK_C
    up_pad = jnp.zeros((DEPTH, 2, SMALL_W, kw), F32)
    for d in range(2):
        lo = 2 * N_HEAD * 2 + d * R_C
        up_pad = up_pad.at[:, d, lo:lo + R_C, :].set(w_gla_up[:, d].astype(F32))
    wq_hi, wq_lo = _split(peer_w_q)
    keys_hi, keys_lo = _split(peer_sub_keys)
    u_b = peer_u.astype(BF16)
    vt_b = jnp.swapaxes(peer_v, 1, 2).astype(BF16)
    wbr_b = w_branch.astype(BF16)
    wout_b = w_out.astype(BF16)

    def layer_params(l):
        return {
            'norm1_g': norm1_g[l].reshape(1, D_MODEL), 'norm2_g': norm2_g[l].reshape(1, D_MODEL),
            'layer': l, 'w_wide': w_wide, 'b_wide': b_wide[l], 'w_fine': w_fine, 'b_fine': b_fine[l],
            'hgrn_lb': lb_all[l],
            'gla_up_pad': up_pad[l], 'gla_up_b': b_gla_up[l],
            'conv_w': conv_w[l], 'conv_b': conv_b[l], 'lru_w_a': lru_w_a[l], 'lru_b_a': lru_b_a[l],
            'lru_w_x': lru_w_x[l], 'lru_b_x': lru_b_x[l], 'lru_lambda': lru_lambda[l],
            'w_branch': wbr_b, 'w_out': wout_b, 'wq_hi': wq_hi, 'wq_lo': wq_lo,
            'keys_hi': keys_hi, 'keys_lo': keys_lo, 'peer_u': u_b, 'peer_vt': vt_b,
        }

    cond = jnp.concatenate([c, c_ctx[None, :]], axis=0).astype(F32)
    n_dec = c.shape[0]
    mod = _modulation(cond, w_mod, b_mod)
    final_g = final_norm_g.reshape(1, D_MODEL)

    bp, tp, _ = x_prompt.shape
    x = x_prompt.reshape(bp * tp, D_MODEL)
    ctx_states = []
    for l in range(DEPTH):
        mod_l = mod[l, n_dec:n_dec + 1].reshape(1, 1, 6 * D_MODEL)
        x, st = _layer(x, bp, tp, mod_l, layer_params(l), None, True)
        ctx_states.append(st)
    y_prompt = _final_norm(x, final_g).reshape(bp, tp, D_MODEL)
    new_states = tuple(jnp.stack([s[i] for s in ctx_states], axis=1) for i in range(6))

    bd, td, _ = x_sample.shape
    x = _add_position(x_sample, _position_code(td // GRID_W)).reshape(bd * td, D_MODEL)
    for l in range(DEPTH):
        cached = (state_mlstm_C[:, l].astype(F32), state_mlstm_n[:, l].astype(F32),
                  state_mlstm_m[:, l].astype(F32), state_hgrn_S[:, l].astype(F32),
                  state_gla_S[:, l].astype(F32), state_lru_h[:, l].astype(F32))
        mod_l = mod[l, 0:n_dec].reshape(n_dec, 1, 6 * D_MODEL)
        x, _ = _layer(x, bd, td, mod_l, layer_params(l), cached, False)
    y_sample = _final_norm(x, final_g).reshape(bd, td, D_MODEL)
    return (y_prompt, y_sample) + new_states
```

```python
import functools

import jax
import jax.numpy as jnp
from jax import lax
from jax.experimental import pallas as pl
from jax.experimental.pallas import tpu as pltpu

F32 = jnp.float32
BF16 = jnp.bfloat16
HIGHEST = lax.Precision.HIGHEST

D_MODEL = 1024
DEPTH = 4
GRID_W = 64
EPS = 1e-6
NEG_BIG = -1e30
MAX_EXP_ARG = 80.0
MIX_W = 512
N_HEAD = 4
DH = 128
DK_C = 64
R_C = 16
GLA_TAU = 16.0
CONV_W = 4
LRU_C = 8.0
N_BRANCH = 4
N_KEYS = 128
N_EXPERTS = N_KEYS * N_KEYS
PEER_HEADS = 8
PEER_TOPK = 16
POS_BASE = 10000.0

LANE = 128
SUBLANES = 8
VMEM_LIMIT = 56 * 1024 * 1024

COL_GATES = 0
COL_AQ, COL_AK, COL_AV, COL_AO = 4096, 4608, 5120, 5632
COL_BQ, COL_BI, COL_BG = 6144, 6656, 7168
COL_CQ, COL_CK, COL_CV, COL_CG = 7680, 7936, 8192, 8704
COL_DG = 9216
N_WIDE = 9728
COL_BF, COL_DX = 0, 1024
COL_SMALL = 1536
SMALL_W = 256
N_FINE = COL_SMALL + SMALL_W

_WIDE_SRC = ((7216, 11312), (0, 2048), (2064, 2576), (3600, 4624), (4624, 6160), (6704, 7216))
_FINE_SRC = ((2576, 3600), (6192, 6704), (2048, 2064), (6160, 6192))

CHUNK_A = 128
CHUNK_G = 64
SUB_G = 16
CHUNK_D = 128


def _cparams(sem):
    return pltpu.CompilerParams(dimension_semantics=sem, vmem_limit_bytes=VMEM_LIMIT)


def _bdot(a, b):
    return jnp.dot(a.astype(BF16), b.astype(BF16), preferred_element_type=F32)


def _bdot_nt(a, b):
    return lax.dot_general(a.astype(BF16), b.astype(BF16), (((1,), (1,)), ((), ())),
                           preferred_element_type=F32)


def _bdot_tn(a, b):
    return lax.dot_general(a.astype(BF16), b.astype(BF16), (((0,), (0,)), ((), ())),
                           preferred_element_type=F32)


def _split(a):
    hi = a.astype(BF16)
    lo = (a - hi.astype(F32)).astype(BF16)
    return hi, lo


def _dot3(a, b_hi, b_lo):
    a_hi, a_lo = _split(a)
    return (jnp.dot(a_hi, b_hi, preferred_element_type=F32)
            + jnp.dot(a_lo, b_hi, preferred_element_type=F32)
            + jnp.dot(a_hi, b_lo, preferred_element_type=F32))


def _log_sigmoid(z):
    return jnp.minimum(z, 0.0) - jnp.log1p(jnp.exp(-jnp.abs(z)))


def _sigmoid(z):
    return 0.5 * jnp.tanh(0.5 * z) + 0.5


def _gelu(x):
    return 0.5 * x * (1.0 + jnp.tanh(0.7978845608028654 * (x + 0.044715 * x * x * x)))


def _silu(x):
    return x * _sigmoid(x)


def _expm1(y):
    u = jnp.exp(y)
    near = (u - 1.0) * y / jnp.log(u)
    return jnp.where(u == 1.0, y, jnp.where(u < 0.5, u - 1.0, near))


def _mod_kernel(c_ref, w_ref, b_ref, o_ref):
    a = _silu(c_ref[...])
    o_ref[0] = jnp.dot(a, w_ref[0], precision=HIGHEST, preferred_element_type=F32) + b_ref[0]


def _modulation(cond, w_mod, b_mod):
    r = cond.shape[0]
    tc = 1536
    return pl.pallas_call(
        _mod_kernel,
        grid=(DEPTH, 6 * D_MODEL // tc),
        in_specs=[pl.BlockSpec((r, D_MODEL), lambda l, j: (0, 0)),
                  pl.BlockSpec((1, D_MODEL, tc), lambda l, j: (l, 0, j)),
                  pl.BlockSpec((1, 1, tc), lambda l, j: (l, 0, j))],
        out_specs=pl.BlockSpec((1, r, tc), lambda l, j: (l, 0, j)),
        out_shape=jax.ShapeDtypeStruct((DEPTH, r, 6 * D_MODEL), F32),
        compiler_params=_cparams(("parallel", "parallel")),
        name="modulation",
    )(cond, w_mod, b_mod.reshape(DEPTH, 1, 6 * D_MODEL))


def _addpos_kernel(x_ref, p_ref, o_ref):
    o_ref[0] = x_ref[0] + p_ref[...]


def _add_position(x, pos):
    b, t, d = x.shape
    tt = 512
    return pl.pallas_call(
        _addpos_kernel,
        grid=(b, t // tt),
        in_specs=[pl.BlockSpec((1, tt, d), lambda i, j: (i, j, 0)),
                  pl.BlockSpec((tt, d), lambda i, j: (j, 0))],
        out_specs=pl.BlockSpec((1, tt, d), lambda i, j: (i, j, 0)),
        out_shape=jax.ShapeDtypeStruct(x.shape, F32),
        compiler_params=_cparams(("parallel", "parallel")),
        name="add_position",
    )(x, pos)


def _inproj_kernel(x_ref, mod_ref, g_ref, w_ref, b_ref, o_ref, h_ref):
    @pl.when(pl.program_id(1) == 0)
    def _():
        x = x_ref[...]
        inv = lax.rsqrt(jnp.mean(x * x, axis=-1, keepdims=True) + EPS)
        sh = mod_ref[0, :, 0:D_MODEL]
        sc = mod_ref[0, :, D_MODEL:2 * D_MODEL]
        h_ref[...] = (x * inv * g_ref[...] * (1.0 + sc) + sh).astype(BF16)

    acc = jnp.dot(h_ref[...], w_ref[...], preferred_element_type=F32) + b_ref[...]
    o_ref[...] = acc.astype(o_ref.dtype)


def _inproj(x, mod_l, norm_g, w_pack, b_pack, seq_len, layer, tc, out_dtype, name):
    n = x.shape[0]
    width = w_pack.shape[-1]
    tn = 1024
    rows = mod_l.shape[0]
    if rows == 1:
        mod_map = lambda i, j: (0, 0, 0)
    else:
        mod_map = lambda i, j: ((i * tn) // seq_len, 0, 0)
    return pl.pallas_call(
        _inproj_kernel,
        grid=(n // tn, width // tc),
        in_specs=[pl.BlockSpec((tn, D_MODEL), lambda i, j: (i, 0)),
                  pl.BlockSpec((1, 1, 6 * D_MODEL), mod_map),
                  pl.BlockSpec((1, D_MODEL), lambda i, j: (0, 0)),
                  pl.BlockSpec((None, D_MODEL, tc), lambda i, j: (layer, 0, j)),
                  pl.BlockSpec((1, tc), lambda i, j: (0, j))],
        out_specs=pl.BlockSpec((tn, tc), lambda i, j: (i, j)),
        out_shape=jax.ShapeDtypeStruct((n, width), out_dtype),
        scratch_shapes=[pltpu.VMEM((tn, D_MODEL), BF16)],
        compiler_params=_cparams(("parallel", "arbitrary")),
        name=name,
    )(x, mod_l, norm_g, w_pack, b_pack)


SCAN_GROUP = 4


def _lockstep(gens):
    alive = list(gens)
    while alive:
        still = []
        for gen in alive:
            try:
                next(gen)
                still.append(gen)
            except StopIteration:
                pass
        alive = still
        if alive:
            yield


def _per_sequence(body, n_group, shared):
    def kern(*refs):
        def run(phase):
            stages = [body(*[r if i in shared else r.at[g] for i, r in enumerate(refs)], phase=phase)
                      for g in range(n_group)]
            for _ in _lockstep([s for s in stages if s is not None]):
                pass

        pl.when(pl.program_id(1) == 0)(lambda: run('init'))
        run('main')
        pl.when(pl.program_id(1) == pl.num_programs(1) - 1)(lambda: run('emit'))
    return kern


def _cumsum_rows(x, rev):
    k = 1
    while k < x.shape[0]:
        x = x + _shift_rows(x, k, rev, 0.0)
        k *= 2
    return x


def _causal_mask(length, rev):
    row = lax.broadcasted_iota(jnp.int32, (length, length), 0)
    col = lax.broadcasted_iota(jnp.int32, (length, length), 1)
    return (col >= row) if rev else (col <= row)


def _mlstm_kernel(*refs, has_init, emit_state, phase):
    L = CHUNK_A
    ins = list(refs)
    dirs = [ins[0:4], ins[4:8]]
    pos = 8
    if has_init:
        c0_ref, n0_ref, m0_ref = ins[pos:pos + 3]
        pos += 3
    outs = ins[pos:pos + 2]
    pos += 2
    if emit_state:
        cout_ref, nout_ref, mout_ref = ins[pos:pos + 3]
        pos += 3
    cs_ref, ns_ref, ms_ref = ins[pos:pos + 3]

    if phase == 'init':
        if has_init:
            cs_ref[...] = c0_ref[...]
            ns_ref[...] = n0_ref[...]
            ms_ref[...] = m0_ref[...]
        else:
            cs_ref[...] = jnp.zeros_like(cs_ref)
            ns_ref[...] = jnp.zeros_like(ns_ref)
            ms_ref[...] = jnp.zeros_like(ms_ref)
        return
    if phase == 'emit':
        if emit_state:
            cout_ref[...] = cs_ref[...]
            nout_ref[...] = ns_ref[...]
            mout_ref[...] = ms_ref[...]
        return

    def direction(d):
        rev = d == 1
        q_ref, k_ref, v_ref, s_ref = dirs[d]
        o_ref = outs[d]
        mask = _causal_mask(L, rev)
        sm = s_ref[:, 0:LANE]
        lf = _log_sigmoid(sm)
        bc = jnp.dot(mask.astype(F32), lf, precision=HIGHEST,
                     preferred_element_type=F32)
        sm_t = sm.T
        bc_t = bc.T
        last = 0 if rev else L - 1
        heads = range(N_HEAD)
        hsl = [slice(h * DH, (h + 1) * DH) for h in heads]
        yield
        li_c, b_c, logw, m_prev, m_t, w_state = [], [], [], [], [], []
        for h in heads:
            ci, cf = d * N_HEAD + h, 2 * N_HEAD + d * N_HEAD + h
            li_c.append(sm[:, ci:ci + 1])
            b_c.append(bc[:, cf:cf + 1])
            li_r, b_r = sm_t[ci:ci + 1, :], bc_t[cf:cf + 1, :]
            m_prev.append(ms_ref[d, h][:, 0:1])
            logw.append(jnp.where(mask, b_c[h] - b_r + li_r, NEG_BIG))
            from_state = b_c[h] + m_prev[h]
            m_t.append(jnp.maximum(from_state, jnp.max(logw[h], axis=-1, keepdims=True)))
            w_state.append(jnp.exp(from_state - m_t[h]))
        q = [q_ref[:, hsl[h]].astype(F32) for h in heads]
        k = [k_ref[:, hsl[h]].astype(F32) * (DH ** -0.5) for h in heads]
        v = [v_ref[:, hsl[h]] for h in heads]
        c_st = [cs_ref[d, h] for h in heads]
        n_st = [ns_ref[d, h] for h in heads]
        qk = [_bdot_nt(q[h], k[h]) for h in heads]
        qc = [_bdot(q[h], c_st[h]) for h in heads]
        yield
        for h in heads:
            scores = qk[h] * jnp.exp(logw[h] - m_t[h])
            num = w_state[h] * qc[h] + _bdot(scores, v[h])
            den = (w_state[h] * jnp.sum(q[h] * n_st[h], axis=-1, keepdims=True)
                   + jnp.sum(scores, axis=-1, keepdims=True))
            floor = jnp.exp(jnp.minimum(-m_t[h], MAX_EXP_ARG))
            o_ref[:, hsl[h]] = num / jnp.maximum(jnp.abs(den), floor)
        yield
        for h in heads:
            m_new = m_t[h][last:last + 1, :]
            b_last = b_c[h][last:last + 1, :]
            kw = k[h] * jnp.exp(b_last - b_c[h] + li_c[h] - m_new)
            decay = jnp.exp(b_last + m_prev[h] - m_new)
            cs_ref[d, h] = decay * c_st[h] + _bdot_tn(kw, v[h])
            ns_ref[d, h] = decay * n_st[h] + jnp.sum(kw, axis=0, keepdims=True)
            ms_ref[d, h] = jnp.broadcast_to(m_new, (1, LANE))

    yield from _lockstep([direction(0), direction(1)])


def _scan_maps(nc):
    fwd = lambda cb: (lambda b, c: (b, c, cb))
    bwd = lambda cb: (lambda b, c: (b, nc - 1 - c, cb))
    return fwd, bwd


def _mlstm(pw, pf, bsz, seq, state, emit_state):
    L = CHUNK_A
    nc = seq // L
    n = bsz * seq
    g = min(bsz, SCAN_GROUP)
    has_init = state is not None
    pw3 = pw.reshape(bsz, seq, N_WIDE)
    pf3 = pf.reshape(bsz, seq, N_FINE)
    fwd, bwd = _scan_maps(nc)

    in_specs, args = [], []
    for mk in (fwd, bwd):
        for col in (COL_AQ, COL_AK, COL_AV):
            in_specs.append(pl.BlockSpec((g, L, MIX_W), mk(col // MIX_W)))
            args.append(pw3)
        in_specs.append(pl.BlockSpec((g, L, SMALL_W), mk(COL_SMALL // SMALL_W)))
        args.append(pf3)
    st_specs = [pl.BlockSpec((g, 2, N_HEAD, DH, DH), lambda b, c: (b, 0, 0, 0, 0)),
                pl.BlockSpec((g, 2, N_HEAD, 1, DH), lambda b, c: (b, 0, 0, 0, 0)),
                pl.BlockSpec((g, 2, N_HEAD, 1, LANE), lambda b, c: (b, 0, 0, 0, 0))]
    st_shapes = [jax.ShapeDtypeStruct((bsz, 2, N_HEAD, DH, DH), F32),
                 jax.ShapeDtypeStruct((bsz, 2, N_HEAD, 1, DH), F32),
                 jax.ShapeDtypeStruct((bsz, 2, N_HEAD, 1, LANE), F32)]
    if has_init:
        c0, n0, m0 = state
        in_specs += st_specs
        args += [c0, n0.reshape(bsz, 2, N_HEAD, 1, DH),
                 jnp.broadcast_to(m0[..., None, None], (bsz, 2, N_HEAD, 1, LANE))]
    out_specs = [pl.BlockSpec((g, L, MIX_W), fwd(0)), pl.BlockSpec((g, L, MIX_W), bwd(0))]
    out_shape = [jax.ShapeDtypeStruct((bsz, seq, MIX_W), F32)] * 2
    if emit_state:
        out_specs += st_specs
        out_shape += st_shapes
    body = functools.partial(_mlstm_kernel, has_init=has_init, emit_state=emit_state)
    res = pl.pallas_call(
        _per_sequence(body, g, ()),
        grid=(bsz // g, nc),
        in_specs=in_specs,
        out_specs=out_specs,
        out_shape=out_shape,
        scratch_shapes=[pltpu.VMEM((g, 2, N_HEAD, DH, DH), F32),
                        pltpu.VMEM((g, 2, N_HEAD, 1, DH), F32),
                        pltpu.VMEM((g, 2, N_HEAD, 1, LANE), F32)],
        compiler_params=_cparams(("parallel", "arbitrary")),
        name="mlstm_scan",
    )(*args)
    o_f, o_b = res[0].reshape(n, MIX_W), res[1].reshape(n, MIX_W)
    new_state = None
    if emit_state:
        new_state = (res[2], res[3].reshape(bsz, 2, N_HEAD, DH), res[4][:, :, :, 0, 0])
    return o_f, o_b, new_state


def _gla_direction(q, k, la, v, s_ref, d, o_ref, *, rev, dk):
    L, S = CHUNK_G, SUB_G
    nsub = L // S
    mask = _causal_mask(L, rev)
    b = _cumsum_rows(la, rev)
    bx = b - la
    ref_rows = [bx[i * S + (S - 1 if rev else 0):i * S + (S - 1 if rev else 0) + 1, :]
                for i in range(nsub)]
    bref = jnp.concatenate([jnp.broadcast_to(r, (S, r.shape[1])) for r in ref_rows], axis=0)
    q_hat = q * jnp.exp(b - bref)
    rowid = lax.broadcasted_iota(jnp.int32, (L, 1), 0)
    k_hat = []
    for i in range(nsub):
        seen = (rowid >= i * S) if rev else (rowid < (i + 1) * S)
        k_hat.append(k * jnp.exp(jnp.where(seen, ref_rows[i] - b, NEG_BIG)))
    last = 0 if rev else L - 1
    b_last = b[last:last + 1, :]
    q_state = q * jnp.exp(b)
    k_state = k * jnp.exp(b_last - b)
    e_last = jnp.exp(b_last)
    yield
    ksl = [slice(h * dk, (h + 1) * dk) for h in range(N_HEAD)]
    vsl = [slice(h * DH, (h + 1) * DH) for h in range(N_HEAD)]
    blocks = [[_bdot_nt(q_hat[i * S:(i + 1) * S, ksl[h]], k_hat[i][:, ksl[h]]) for i in range(nsub)]
              for h in range(N_HEAD)]
    states = [s_ref[d, h] for h in range(N_HEAD)]
    inter = [_bdot_nt(q_state[:, ksl[h]], states[h]) for h in range(N_HEAD)]
    grown = [_bdot_tn(v[:, vsl[h]], k_state[:, ksl[h]]) for h in range(N_HEAD)]
    yield
    for h in range(N_HEAD):
        scores = jnp.where(mask, jnp.concatenate(blocks[h], axis=0), 0.0)
        o_ref[:, vsl[h]] = inter[h] + _bdot(scores, v[:, vsl[h]])
    yield
    for h in range(N_HEAD):
        s_ref[d, h] = states[h] * e_last[:, ksl[h]] + grown[h]


def _scan_state_io(ins, pos, has_init, emit_state, n_out):
    s0_ref = sout_ref = None
    if has_init:
        s0_ref = ins[pos]
        pos += 1
    outs = ins[pos:pos + n_out]
    pos += n_out
    if emit_state:
        sout_ref = ins[pos]
        pos += 1
    return s0_ref, outs, sout_ref, ins[pos]


def _scan_state_phase(phase, s_ref, s0_ref, sout_ref):
    if phase == 'init':
        if s0_ref is not None:
            s_ref[...] = s0_ref[...]
        else:
            s_ref[...] = jnp.zeros_like(s_ref)
    elif phase == 'emit' and sout_ref is not None:
        sout_ref[...] = s_ref[...]
    return phase != 'main'


def _hgrn_kernel(*refs, has_init, emit_state, phase):
    ins = list(refs)
    lb_ref = ins[6]
    s0_ref, outs, sout_ref, s_ref = _scan_state_io(ins, 7, has_init, emit_state, 2)
    if _scan_state_phase(phase, s_ref, s0_ref, sout_ref):
        return
    chains = []
    for d in range(2):
        q_ref, z_ref, v_ref = ins[3 * d:3 * d + 3]
        z = z_ref[...]
        lb = lb_ref[d:d + 1, :]
        la = _log_sigmoid(z) + jnp.log1p(lb * jnp.exp(jnp.minimum(-z, MAX_EXP_ARG)))
        k = (1.0 - lb) * _sigmoid(-z)
        q = q_ref[...].astype(F32) * (DH ** -0.5)
        chains.append(_gla_direction(q, k, la, v_ref[...], s_ref, d, outs[d], rev=d == 1, dk=DH))
    yield from _lockstep(chains)


def _gla_kernel(*refs, has_init, emit_state, phase):
    ins = list(refs)
    up_ref, upb_ref = ins[8], ins[9]
    s0_ref, outs, sout_ref, s_ref = _scan_state_io(ins, 10, has_init, emit_state, 2)
    if _scan_state_phase(phase, s_ref, s0_ref, sout_ref):
        return
    chains = []
    for d in range(2):
        q_ref, k_ref, v_ref, sm_ref = ins[4 * d:4 * d + 4]
        zg = jnp.dot(sm_ref[...], up_ref[d], precision=HIGHEST,
                     preferred_element_type=F32) + upb_ref[d:d + 1, :]
        la = _log_sigmoid(zg) * (1.0 / GLA_TAU)
        q = q_ref[...].astype(F32) * (DK_C ** -0.5)
        k = k_ref[...].astype(F32)
        chains.append(_gla_direction(q, k, la, v_ref[...], s_ref, d, outs[d], rev=d == 1,
                                     dk=DK_C))
    yield from _lockstep(chains)


def _gated_scan(kernel, p, bsz, seq, cols, extra, extra_specs, dk, state_t, emit_state, name):
    L = CHUNK_G
    nc = seq // L
    n = bsz * seq
    g = min(bsz, SCAN_GROUP)
    has_init = state_t is not None
    src = {'wide': p[0].reshape(bsz, seq, N_WIDE), 'fine': p[1].reshape(bsz, seq, N_FINE)}
    fwd, bwd = _scan_maps(nc)

    in_specs, args = [], []
    for d, mk in enumerate((fwd, bwd)):
        for which, off, width in cols[d]:
            in_specs.append(pl.BlockSpec((g, L, width), mk(off // width)))
            args.append(src[which])
    shared = tuple(range(len(args), len(args) + len(extra)))
    in_specs += extra_specs
    args += extra
    st_spec = pl.BlockSpec((g, 2, N_HEAD, DH, dk), lambda b, c: (b, 0, 0, 0, 0))
    if has_init:
        in_specs.append(st_spec)
        args.append(state_t)
    out_specs = [pl.BlockSpec((g, L, MIX_W), fwd(0)), pl.BlockSpec((g, L, MIX_W), bwd(0))]
    out_shape = [jax.ShapeDtypeStruct((bsz, seq, MIX_W), F32)] * 2
    if emit_state:
        out_specs.append(st_spec)
        out_shape.append(jax.ShapeDtypeStruct((bsz, 2, N_HEAD, DH, dk), F32))
    body = functools.partial(kernel, has_init=has_init, emit_state=emit_state)
    res = pl.pallas_call(
        _per_sequence(body, g, shared),
        grid=(bsz // g, nc),
        in_specs=in_specs,
        out_specs=out_specs,
        out_shape=out_shape,
        scratch_shapes=[pltpu.VMEM((g, 2, N_HEAD, DH, dk), F32)],
        compiler_params=_cparams(("parallel", "arbitrary")),
        name=name,
    )(*args)
    return res[0].reshape(n, MIX_W), res[1].reshape(n, MIX_W), (res[2] if emit_state else None)


def _hgrn(p, bsz, seq, lb, state_t, emit_state):
    cols = [[('wide', COL_BQ, MIX_W), ('fine', COL_BF + d * MIX_W, MIX_W), ('wide', COL_BI, MIX_W)]
            for d in range(2)]
    return _gated_scan(_hgrn_kernel, p, bsz, seq, cols, [lb],
                       [pl.BlockSpec((2, MIX_W), lambda b, c: (0, 0))], DH, state_t, emit_state,
                       "hgrn2_scan")


def _gla(p, bsz, seq, up_pad, up_b, state_t, emit_state):
    kw = N_HEAD * DK_C
    cols = [[('wide', COL_CQ, kw), ('wide', COL_CK, kw), ('wide', COL_CV, MIX_W),
             ('fine', COL_SMALL, SMALL_W)] for _ in range(2)]
    return _gated_scan(_gla_kernel, p, bsz, seq, cols, [up_pad, up_b],
                       [pl.BlockSpec((2, SMALL_W, kw), lambda b, c: (0, 0, 0)),
                        pl.BlockSpec((2, kw), lambda b, c: (0, 0))], DK_C, state_t, emit_state,
                       "gla_scan")


def _shift_rows(x, k, rev, fill):
    n = x.shape[0]
    rowid = lax.broadcasted_iota(jnp.int32, x.shape, 0)
    if rev:
        return jnp.where(rowid >= n - k, fill, pltpu.roll(x, n - k, 0))
    return jnp.where(rowid < k, fill, pltpu.roll(x, k, 0))


def _lru_kernel(*refs, seq, has_init, emit_state):
    L = CHUNK_D
    nc = seq // L
    ins = list(refs)
    (dx_ref, dg_ref, cw_ref, cb_ref, wah_ref, wal_ref, ba_ref, wxh_ref, wxl_ref, bx_ref,
     lam_ref) = ins[0:11]
    pos = 11
    h0_ref = hout_ref = None
    if has_init:
        h0_ref = ins[pos]
        pos += 1
    y_ref = ins[pos]
    pos += 1
    if emit_state:
        hout_ref = ins[pos]
        pos += 1
    pad_ref, hb_ref = ins[pos], ins[pos + 1]

    zeros8 = jnp.zeros((8, LANE), F32)
    pad_ref[0:8, :] = zeros8
    pad_ref[8:8 + seq, :] = dx_ref[...]
    pad_ref[8 + seq:16 + seq, :] = zeros8
    lam = lam_ref[...]
    sp = jnp.maximum(-lam, 0.0) + jnp.log1p(jnp.exp(-jnp.abs(lam)))
    cw = cw_ref[...]
    cb = cb_ref[...]

    def chunk_scan(c, carry, d, out):
        rev = d == 1
        start = pl.multiple_of(c * L, L)
        win = pad_ref[pl.ds(start, L + 16), :]
        xd = cb + sum(cw[j:j + 1, :] * win[7 + j:7 + j + L, :] for j in range(CONV_W))
        yield
        za = _dot3(xd, wah_ref[d, 0], wal_ref[d, 0])
        zx = _dot3(xd, wxh_ref[d, 0], wxl_ref[d, 0])
        yield
        r = _sigmoid(za + ba_ref[d:d + 1, :])
        ig = _sigmoid(zx + bx_ref[d:d + 1, :])
        log_a = -LRU_C * r * sp[d:d + 1, :]
        a = jnp.exp(log_a)
        u = jnp.sqrt(jnp.maximum(-_expm1(2.0 * log_a), 0.0)) * (ig * xd)
        yield
        k = 1
        while k < L:
            u = a * _shift_rows(u, k, rev, 0.0) + u
            a = a * _shift_rows(a, k, rev, 1.0)
            k *= 2
            yield
        h = a * carry + u
        last = 0 if rev else L - 1
        out.extend((start, h, h[last:last + 1, :]))

    def scan_body(i, carry):
        res_f, res_b = [], []
        for _ in _lockstep([chunk_scan(i, carry[0], 0, res_f),
                            chunk_scan(nc - 1 - i, carry[1], 1, res_b)]):
            pass
        y_ref[pl.ds(res_f[0], L), :] = res_f[1]
        hb_ref[pl.ds(res_b[0], L), :] = res_b[1]
        return res_f[2], res_b[2]

    def gate_body(c, carry):
        rows = pl.ds(pl.multiple_of(c * L, L), L)
        y_ref[rows, :] = (y_ref[rows, :] + hb_ref[rows, :]) * _gelu(dg_ref[rows, :].astype(F32))
        return carry

    if has_init:
        init = (h0_ref[0, 0:1, :], h0_ref[0, 1:2, :])
    else:
        init = (jnp.zeros((1, LANE), F32), jnp.zeros((1, LANE), F32))
    fin_f, fin_b = lax.fori_loop(0, nc, scan_body, init)
    lax.fori_loop(0, nc, gate_body, 0)
    if emit_state:
        hout_ref[0, 0:1, :] = fin_f
        hout_ref[0, 1:2, :] = fin_b


def _lru(pw, pf, bsz, seq, conv_w, conv_b, wa, ba, wx, bx, lam, h0, emit_state):
    n = bsz * seq
    has_init = h0 is not None
    wah, wal = _split(wa)
    wxh, wxl = _split(wx)
    gate_w = pl.BlockSpec((2, 1, DH, DH), lambda b, h: (0, h, 0, 0))
    vec2 = pl.BlockSpec((2, LANE), lambda b, h: (0, h))
    in_specs = [pl.BlockSpec((seq, LANE), lambda b, h: (b, COL_DX // LANE + h)),
                pl.BlockSpec((seq, LANE), lambda b, h: (b, COL_DG // LANE + h)),
                pl.BlockSpec((CONV_W, LANE), lambda b, h: (0, h)),
                pl.BlockSpec((1, LANE), lambda b, h: (0, h)),
                gate_w, gate_w, vec2, gate_w, gate_w, vec2, vec2]
    args = [pf, pw, conv_w, conv_b.reshape(1, MIX_W), wah, wal, ba, wxh, wxl, bx, lam]
    st_spec = pl.BlockSpec((1, 2, LANE), lambda b, h: (b, 0, h))
    if has_init:
        in_specs.append(st_spec)
        args.append(h0)
    out_specs = [pl.BlockSpec((seq, LANE), lambda b, h: (b, h))]
    out_shape = [jax.ShapeDtypeStruct((n, MIX_W), F32)]
    if emit_state:
        out_specs.append(st_spec)
        out_shape.append(jax.ShapeDtypeStruct((bsz, 2, MIX_W), F32))
    res = pl.pallas_call(
        functools.partial(_lru_kernel, seq=seq, has_init=has_init, emit_state=emit_state),
        grid=(bsz, N_HEAD),
        in_specs=in_specs,
        out_specs=out_specs,
        out_shape=out_shape,
        scratch_shapes=[pltpu.VMEM((seq + 16, LANE), F32), pltpu.VMEM((seq, LANE), F32)],
        compiler_params=_cparams(("parallel", "parallel")),
        name="conv_rglru",
    )(*args)
    return res[0], (res[1] if emit_state else None)


def _head_rms(x):
    parts = []
    for h in range(N_HEAD):
        xh = x[:, h * DH:(h + 1) * DH]
        parts.append(xh * lax.rsqrt(jnp.mean(xh * xh, axis=-1, keepdims=True) + EPS))
    return jnp.concatenate(parts, axis=-1)


def _merge_kernel(af_ref, ab_ref, bf_ref, bb_ref, cf_ref, cb_ref, yd_ref, ao_ref, bg_ref, cg_ref,
                  gt0_ref, gt1_ref, gt2_ref, gt3_ref, x_ref, mod_ref, g2_ref, wbr_ref, wout_ref,
                  xo_ref, h2_ref, h2b_ref):
    ya = _sigmoid(ao_ref[...].astype(F32)) * _head_rms(af_ref[...] + ab_ref[...])
    yb = _silu(bg_ref[...].astype(F32)) * _head_rms(bf_ref[...] + bb_ref[...])
    yc = _silu(cg_ref[...].astype(F32)) * _head_rms(cf_ref[...] + cb_ref[...])
    merged = None
    gate_refs = (gt0_ref, gt1_ref, gt2_ref, gt3_ref)
    for i, y in enumerate((ya, yb, yc, yd_ref[...])):
        proj = jnp.dot(y.astype(BF16), wbr_ref[i], preferred_element_type=F32)
        term = _sigmoid(gate_refs[i][...].astype(F32)) * proj
        merged = term if merged is None else merged + term
    out = jnp.dot(merged.astype(BF16), wout_ref[...], preferred_element_type=F32)
    g1 = mod_ref[0, :, 2 * D_MODEL:3 * D_MODEL]
    sh2 = mod_ref[0, :, 3 * D_MODEL:4 * D_MODEL]
    sc2 = mod_ref[0, :, 4 * D_MODEL:5 * D_MODEL]
    x = x_ref[...] + g1 * out
    xo_ref[...] = x
    inv = lax.rsqrt(jnp.mean(x * x, axis=-1, keepdims=True) + EPS)
    h2 = x * inv * g2_ref[...] * (1.0 + sc2) + sh2
    h2_ref[...] = h2
    h2b_ref[...] = h2.astype(BF16)


def _merge(mix_outs, yd, p, x, mod_l, norm2_g, w_branch, w_out, seq_len, layer):
    n = x.shape[0]
    tn = 256
    rows = mod_l.shape[0]
    mod_map = (lambda i: (0, 0, 0)) if rows == 1 else (lambda i: ((i * tn) // seq_len, 0, 0))
    tok = lambda cb: (lambda i: (i, cb))
    in_specs = [pl.BlockSpec((tn, MIX_W), tok(0))] * 7
    in_specs += [pl.BlockSpec((tn, MIX_W), tok(COL_AO // MIX_W)),
                 pl.BlockSpec((tn, MIX_W), tok(COL_BG // MIX_W)),
                 pl.BlockSpec((tn, MIX_W), tok(COL_CG // MIX_W)),
                 *[pl.BlockSpec((tn, D_MODEL), tok(COL_GATES // D_MODEL + i)) for i in range(N_BRANCH)],
                 pl.BlockSpec((tn, D_MODEL), tok(0)),
                 pl.BlockSpec((1, 1, 6 * D_MODEL), mod_map),
                 pl.BlockSpec((1, D_MODEL), lambda i: (0, 0)),
                 pl.BlockSpec((None, N_BRANCH, MIX_W, D_MODEL), lambda i: (layer, 0, 0, 0)),
                 pl.BlockSpec((None, D_MODEL, D_MODEL), lambda i: (layer, 0, 0))]
    out_spec = pl.BlockSpec((tn, D_MODEL), tok(0))
    return pl.pallas_call(
        _merge_kernel,
        grid=(n // tn,),
        in_specs=in_specs,
        out_specs=[out_spec, out_spec, out_spec],
        out_shape=[jax.ShapeDtypeStruct((n, D_MODEL), F32), jax.ShapeDtypeStruct((n, D_MODEL), F32),
                   jax.ShapeDtypeStruct((n, D_MODEL), BF16)],
        compiler_params=_cparams(("parallel",)),
        name="branch_merge",
    )(*mix_outs, yd, p, p, p, p, p, p, p, x, mod_l, norm2_g, w_branch, w_out)


ROUTE_TN = 512
ROUTE_LG = LANE


def _top16_exact(vals):
    n_rows = vals.shape[0]
    rowid = lax.broadcasted_iota(jnp.int32, vals.shape, 0).astype(F32)
    rank = jnp.full(vals.shape, float(PEER_TOPK), F32)
    tops = []
    for r in range(PEER_TOPK):
        m = jnp.max(vals, axis=0, keepdims=True)
        idx = jnp.min(jnp.where(vals == m, rowid, float(n_rows)), axis=0, keepdims=True)
        sel = rowid == idx
        rank = jnp.where(sel, float(r), rank)
        vals = jnp.where(sel, -jnp.inf, vals)
        tops.append(m)
    return jnp.concatenate(tops, axis=0), rank


_TAKEN_BASE = -3.0e38
_TAKEN_STEP = 2.0e36
_TAKEN_BELOW = _TAKEN_BASE + 0.5 * _TAKEN_STEP


def _top16_quick_many(vals_list):
    work = list(vals_list)
    tops = [[] for _ in work]
    for r in range(PEER_TOPK):
        for i in range(len(work)):
            m = jnp.max(work[i], axis=0, keepdims=True)
            work[i] = jnp.where(work[i] == m, _TAKEN_BASE - r * _TAKEN_STEP, work[i])
            tops[i].append(m)
    out = []
    for i in range(len(work)):
        taken = work[i] <= _TAKEN_BELOW
        rank = jnp.where(taken, jnp.round((_TAKEN_BASE - work[i]) * (1.0 / _TAKEN_STEP)),
                         float(PEER_TOPK))
        n_taken = jnp.sum(jnp.where(taken, 1.0, 0.0), axis=0, keepdims=True)
        out.append((jnp.concatenate(tops[i], axis=0), rank, n_taken))
    return out


_CAND_PIECES = (('a', 0, 0, 16), ('a', 0, 8, 16), ('a', 1, 0, 8), ('a', 2, 0, 5), ('a', 3, 0, 4),
                ('b', 0, 0, (4, 8)), ('b', 0, 8, (8, 16)), ('b', 1, 0, (4, 8)), ('b', 2, 0, (4, 5)))


def _pair_pieces(x1, x2, combine):
    out = []
    for kind, fixed, off, _ in _CAND_PIECES:
        if kind == 'a':
            out.append(combine(x1[fixed:fixed + 1, :], x2[off:off + 8, :]))
        else:
            out.append(combine(x1[off:off + 8, :], x2[fixed:fixed + 1, :]))
    return out


def _candidate_sums(s1, s2):
    K = PEER_TOPK
    n = s1.shape[1]
    iota8 = lax.broadcasted_iota(jnp.int32, (8, n), 0)
    sums = _pair_pieces(s1, s2, lambda x, y: x + y)
    vals, poss = [], []
    for (kind, fixed, off, lim), sm in zip(_CAND_PIECES, sums):
        idx = iota8 + off
        if kind == 'a':
            valid = idx < lim
            pos = fixed * K + idx
        else:
            valid = (idx >= lim[0]) & (idx < lim[1])
            pos = idx * K + fixed
        vals.append(jnp.where(valid, sm, -jnp.inf))
        poss.append(jnp.where(valid, pos, K * K).astype(F32))
    return jnp.concatenate(vals, axis=0), jnp.concatenate(poss, axis=0)


def _choose_exact(vals, posid):
    K = PEER_TOPK
    chosen = jnp.zeros(vals.shape, F32)
    for _ in range(K):
        m = jnp.max(vals, axis=0, keepdims=True)
        idx = jnp.min(jnp.where(vals == m, posid, float(K * K)), axis=0, keepdims=True)
        sel = posid == idx
        chosen = jnp.where(sel, 1.0, chosen)
        vals = jnp.where(sel, -jnp.inf, vals)
    return chosen


def _choose_quick_many(vals_list):
    work = list(vals_list)
    for _ in range(PEER_TOPK):
        for i in range(len(work)):
            m = jnp.max(work[i], axis=0, keepdims=True)
            work[i] = jnp.where(work[i] == m, _TAKEN_BASE, work[i])
    out = []
    for w in work:
        chosen = jnp.where(w == _TAKEN_BASE, 1.0, 0.0)
        out.append((chosen, jnp.sum(chosen, axis=0, keepdims=True)))
    return out


def _route_kernel(h_ref, wqh_ref, wql_ref, kh_ref, kl_ref, a_ref, b_ref, r2_ref, c_ref,
                  qh_ref, ql_ref, s_ref, top_ref, rank_ref, ch_ref):
    K = PEER_TOPK
    n_lg = ROUTE_TN // ROUTE_LG
    q = _dot3(h_ref[...], wqh_ref[...], wql_ref[...])
    q_hi, q_lo = _split(q)
    for i in range(2 * PEER_HEADS):
        qh_ref[i] = q_hi[:, i * N_KEYS:(i + 1) * N_KEYS]
        ql_ref[i] = q_lo[:, i * N_KEYS:(i + 1) * N_KEYS]
    nt = (((1,), (1,)), ((), ()))
    lanes = [slice(lg * ROUTE_LG, (lg + 1) * ROUTE_LG) for lg in range(n_lg)]

    def miscount(counts):
        worst = None
        for cnt in counts:
            dev = jnp.abs(cnt - float(K))
            worst = dev if worst is None else jnp.maximum(worst, dev)
        return jnp.max(worst) > 0.0

    def head_body(h, carry):
        for half in range(2):
            kh, kl = kh_ref[h, half], kl_ref[h, half]
            qh, ql = qh_ref[2 * h + half], ql_ref[2 * h + half]
            s_ref[half] = (lax.dot_general(kh, qh, nt, preferred_element_type=F32)
                           + lax.dot_general(kl, qh, nt, preferred_element_type=F32)
                           + lax.dot_general(kh, ql, nt, preferred_element_type=F32))

        problems = [(half, lg) for half in range(2) for lg in range(n_lg)]
        quick = _top16_quick_many([s_ref[half, :, lanes[lg]] for half, lg in problems])
        counts = []
        for (half, lg), (top, rank, n_taken) in zip(problems, quick):
            top_ref[half, lg] = top
            rank_ref[half, lg] = rank
            counts.append(n_taken)

        @pl.when(miscount(counts))
        def _():
            for half in range(2):
                for lg in range(n_lg):
                    top, rank = _top16_exact(s_ref[half, :, lanes[lg]])
                    top_ref[half, lg] = top
                    rank_ref[half, lg] = rank

        cands = [_candidate_sums(top_ref[0, lg], top_ref[1, lg])[0] for lg in range(n_lg)]
        counts = []
        for lg, (chosen, n_taken) in enumerate(_choose_quick_many(cands)):
            ch_ref[lg] = chosen
            counts.append(n_taken)

        @pl.when(miscount(counts))
        def _():
            for lg in range(n_lg):
                vals, posid = _candidate_sums(top_ref[0, lg], top_ref[1, lg])
                ch_ref[lg] = _choose_exact(vals, posid)

        for lg in range(n_lg):
            ls = lanes[lg]
            st1, st2 = s_ref[0, :, ls], s_ref[1, :, ls]
            s1, s2 = top_ref[0, lg], top_ref[1, lg]
            rank1, rank2 = rank_ref[0, lg], rank_ref[1, lg]
            chosen = ch_ref[lg]
            ch = [chosen[8 * i:8 * i + 8, :] for i in range(len(_CAND_PIECES))]
            e1 = jnp.exp(s1 - s1[0:1, :])
            e2 = jnp.exp(s2 - s2[0:1, :])
            pair = _pair_pieces(e1, e2, lambda x, y: x * y)
            z = sum(jnp.sum(c * p, axis=0, keepdims=True) for c, p in zip(ch, pair))
            iota8 = lax.broadcasted_iota(jnp.int32, (8, ROUTE_LG), 0)
            low = ch[5] + ch[7] + ch[8]
            for a, cnt in ((3, ch[4]), (2, ch[3]), (1, ch[2]), (0, ch[0] + ch[1])):
                low = jnp.where(iota8 == a, jnp.sum(cnt, axis=0, keepdims=True), low)
            counts = jnp.concatenate([low, ch[6]], axis=0)
            c_dense = jnp.zeros((N_KEYS, ROUTE_LG), F32)
            for a in range(K):
                c_dense = jnp.where(rank1 == float(a), counts[a:a + 1, :], c_dense)
            a_ref[h, :, ls] = jnp.where(rank1 < float(K), jnp.exp(st1 - s1[0:1, :]), 0.0) / z
            b_ref[h, :, ls] = jnp.where(rank2 < float(K), jnp.exp(st2 - s2[0:1, :]), 0.0).astype(BF16)
            r2_ref[h, :, ls] = rank2.astype(BF16)
            c_ref[h, :, ls] = c_dense
        return carry

    lax.fori_loop(0, PEER_HEADS, head_body, 0)


def _route(h2, wq_hi, wq_lo, keys_hi, keys_lo, layer):
    n = h2.shape[0]
    tn = ROUTE_TN
    dense = pl.BlockSpec((PEER_HEADS, N_KEYS, tn), lambda i: (0, 0, i))
    wspec = pl.BlockSpec((None, D_MODEL, PEER_HEADS * 2 * N_KEYS), lambda i: (layer, 0, 0))
    kspec = pl.BlockSpec((None, PEER_HEADS, 2, N_KEYS, N_KEYS), lambda i: (layer, 0, 0, 0, 0))
    return pl.pallas_call(
        _route_kernel,
        grid=(n // tn,),
        in_specs=[pl.BlockSpec((tn, D_MODEL), lambda i: (i, 0)), wspec, wspec, kspec, kspec],
        out_specs=[dense] * 4,
        out_shape=[jax.ShapeDtypeStruct((PEER_HEADS, N_KEYS, n), dt) for dt in (F32, BF16, BF16, F32)],
        scratch_shapes=[pltpu.VMEM((2 * PEER_HEADS, tn, N_KEYS), BF16),
                        pltpu.VMEM((2 * PEER_HEADS, tn, N_KEYS), BF16),
                        pltpu.VMEM((2, N_KEYS, tn), F32),
                        pltpu.VMEM((2, tn // ROUTE_LG, PEER_TOPK, ROUTE_LG), F32),
                        pltpu.VMEM((2, tn // ROUTE_LG, N_KEYS, ROUTE_LG), F32),
                        pltpu.VMEM((tn // ROUTE_LG, 8 * len(_CAND_PIECES), ROUTE_LG), F32)],
        compiler_params=_cparams(("parallel",)),
        name="peer_route",
    )(h2, wq_hi, wq_lo, keys_hi, keys_lo)


PEER_TN = 512
PEER_TE = 1024


def _gelu_bf16(t):
    x = t.astype(BF16)
    y2 = x * (1.0 + 0.044715 * (x * x)) * (-2.0 * 0.7978845608028654)
    return x / (1.0 + jnp.exp(y2))


BF16_ROWS = 16


def _row_to_packed(row):
    tile = jnp.broadcast_to(row, (BF16_ROWS, row.shape[1])).astype(BF16)
    return jnp.concatenate([tile] * (N_KEYS // BF16_ROWS), axis=0)


def _expert_kernel(hb_ref, u_ref, vt_ref, a_ref, b_ref, r2_ref, c_ref, x_ref, mod_ref, o_ref,
                   acc_ref):
    e = pl.program_id(1)
    rows_per_tile = PEER_TE // N_KEYS

    @pl.when(e == 0)
    def _():
        acc_ref[...] = jnp.zeros_like(acc_ref)

    t_t = lax.dot_general(u_ref[...], hb_ref[...], (((1,), (1,)), ((), ())),
                          preferred_element_type=F32)
    w_parts = []
    for j in range(rows_per_tile):
        e1 = e * rows_per_tile + j
        g = None
        for h in range(PEER_HEADS):
            a_row = _row_to_packed(a_ref[h, pl.ds(e1, 1), :])
            c_row = _row_to_packed(c_ref[h, pl.ds(e1, 1), :])
            term = a_row * jnp.where(r2_ref[h] < c_row, b_ref[h], 0.0)
            g = term if g is None else g + term
        w_parts.append(g * _gelu_bf16(t_t[j * N_KEYS:(j + 1) * N_KEYS, :]))
    w_t = jnp.concatenate(w_parts, axis=0)
    acc_ref[...] += jnp.dot(vt_ref[...], w_t, preferred_element_type=F32)

    @pl.when(e == pl.num_programs(1) - 1)
    def _():
        g2 = mod_ref[0, :, 5 * D_MODEL:6 * D_MODEL]
        o_ref[...] = x_ref[...] + g2 * acc_ref[...].T


def _experts(h2b, u_b, vt_b, dense, x, mod_l, seq_len, layer):
    n = x.shape[0]
    tn, te = PEER_TN, PEER_TE
    rows = mod_l.shape[0]
    mod_map = (lambda i, e: (0, 0, 0)) if rows == 1 else (lambda i, e: ((i * tn) // seq_len, 0, 0))
    dspec = pl.BlockSpec((PEER_HEADS, N_KEYS, tn), lambda i, e: (0, 0, i))
    return pl.pallas_call(
        _expert_kernel,
        grid=(n // tn, N_EXPERTS // te),
        in_specs=[pl.BlockSpec((tn, D_MODEL), lambda i, e: (i, 0)),
                  pl.BlockSpec((None, te, D_MODEL), lambda i, e: (layer, e, 0)),
                  pl.BlockSpec((None, D_MODEL, te), lambda i, e: (layer, 0, e)),
                  dspec, dspec, dspec, dspec,
                  pl.BlockSpec((tn, D_MODEL), lambda i, e: (i, 0)),
                  pl.BlockSpec((1, 1, 6 * D_MODEL), mod_map)],
        out_specs=pl.BlockSpec((tn, D_MODEL), lambda i, e: (i, 0)),
        out_shape=jax.ShapeDtypeStruct((n, D_MODEL), F32),
        scratch_shapes=[pltpu.VMEM((D_MODEL, tn), F32)],
        compiler_params=_cparams(("parallel", "arbitrary")),
        name="peer_experts",
    )(h2b, u_b, vt_b, *dense, x, mod_l)


def _final_norm_kernel(x_ref, g_ref, o_ref):
    x = x_ref[...]
    o_ref[...] = x * lax.rsqrt(jnp.mean(x * x, axis=-1, keepdims=True) + EPS) * g_ref[...]


def _final_norm(x, g):
    n = x.shape[0]
    tn = 512
    return pl.pallas_call(
        _final_norm_kernel,
        grid=(n // tn,),
        in_specs=[pl.BlockSpec((tn, D_MODEL), lambda i: (i, 0)),
                  pl.BlockSpec((1, D_MODEL), lambda i: (0, 0))],
        out_specs=pl.BlockSpec((tn, D_MODEL), lambda i: (i, 0)),
        out_shape=jax.ShapeDtypeStruct((n, D_MODEL), F32),
        compiler_params=_cparams(("parallel",)),
        name="final_norm",
    )(x, g)


def _pack_columns(a, src, width):
    parts = [a[..., lo:hi] for lo, hi in src]
    used = sum(hi - lo for lo, hi in src)
    if used < width:
        parts.append(jnp.zeros(a.shape[:-1] + (width - used,), a.dtype))
    return jnp.concatenate(parts, axis=-1)


def _position_code(rows):
    quarter = D_MODEL // 4
    omega = 1.0 / (POS_BASE ** (jnp.arange(quarter, dtype=F32) / quarter))
    r, col = jnp.meshgrid(jnp.arange(rows, dtype=F32), jnp.arange(GRID_W, dtype=F32), indexing='ij')

    def enc(pos):
        ang = pos.reshape(-1, 1) * omega
        return jnp.concatenate([jnp.sin(ang), jnp.cos(ang)], axis=-1)
    return jnp.concatenate([enc(r), enc(col)], axis=-1)


def _layer(x, bsz, seq, mod_l, lp, state, emit_state):
    layer = lp['layer']
    pw = _inproj(x, mod_l, lp['norm1_g'], lp['w_wide'], lp['b_wide'], seq, layer, 2432, BF16,
                 "in_projection_wide")
    pf = _inproj(x, mod_l, lp['norm1_g'], lp['w_fine'], lp['b_fine'], seq, layer, N_FINE, F32,
                 "in_projection_fine")
    if state is None:
        st_a = st_b = st_c = st_d = None
    else:
        c0, n0, m0, sb0, sc0, hd0 = state
        st_a = (c0, n0, m0)
        st_b = jnp.swapaxes(sb0, -1, -2)
        st_c = jnp.swapaxes(sc0, -1, -2)
        st_d = hd0
    a_f, a_b, new_a = _mlstm(pw, pf, bsz, seq, st_a, emit_state)
    b_f, b_b, new_b = _hgrn((pw, pf), bsz, seq, lp['hgrn_lb'], st_b, emit_state)
    c_f, c_b, new_c = _gla((pw, pf), bsz, seq, lp['gla_up_pad'], lp['gla_up_b'], st_c, emit_state)
    yd, new_d = _lru(pw, pf, bsz, seq, lp['conv_w'], lp['conv_b'], lp['lru_w_a'], lp['lru_b_a'],
                     lp['lru_w_x'], lp['lru_b_x'], lp['lru_lambda'], st_d, emit_state)
    x1, h2, h2b = _merge((a_f, a_b, b_f, b_b, c_f, c_b), yd, pw, x, mod_l, lp['norm2_g'],
                         lp['w_branch'], lp['w_out'], seq, layer)
    dense = _route(h2, lp['wq_hi'], lp['wq_lo'], lp['keys_hi'], lp['keys_lo'], layer)
    x2 = _experts(h2b, lp['peer_u'], lp['peer_vt'], dense, x1, mod_l, seq, layer)
    new_state = None
    if emit_state:
        new_state = (*new_a, jnp.swapaxes(new_b, -1, -2), jnp.swapaxes(new_c, -1, -2), new_d)
    return x2, new_state


def kernel(x_prompt, x_sample, state_mlstm_C, state_mlstm_n, state_mlstm_m, state_hgrn_S,
           state_gla_S, state_lru_h, c, c_ctx, norm1_g, norm2_g, final_norm_g, w_mod, b_mod,
           w_in, b_in, w_gla_up, b_gla_up, hgrn_lower_bounds, conv_w, conv_b, lru_w_a, lru_b_a,
           lru_w_x, lru_b_x, lru_lambda, w_branch, w_out, peer_w_q, peer_sub_keys, peer_u, peer_v):
    lb_soft = jax.nn.softmax(hgrn_lower_bounds.astype(F32), axis=0)
    lb_all = jnp.cumsum(lb_soft, axis=0) - lb_soft[0:1]
    w_wide = _pack_columns(w_in, _WIDE_SRC, N_WIDE).astype(BF16)
    w_fine = _pack_columns(w_in, _FINE_SRC, N_FINE).astype(BF16)
    b_wide = _pack_columns(b_in, _WIDE_SRC, N_WIDE).reshape(DEPTH, 1, N_WIDE)
    b_fine = _pack_columns(b_in, _FINE_SRC, N_FINE).reshape(DEPTH, 1, N_FINE)
    kw = N_HEAD * DK_C
    up_pad = jnp.zeros((DEPTH, 2, SMALL_W, kw), F32)
    for d in range(2):
        lo = 2 * N_HEAD * 2 + d * R_C
        up_pad = up_pad.at[:, d, lo:lo + R_C, :].set(w_gla_up[:, d].astype(F32))
    wq_hi, wq_lo = _split(peer_w_q)
    keys_hi, keys_lo = _split(peer_sub_keys)
    u_b = peer_u.astype(BF16)
    vt_b = jnp.swapaxes(peer_v, 1, 2).astype(BF16)
    wbr_b = w_branch.astype(BF16)
    wout_b = w_out.astype(BF16)

    def layer_params(l):
        return {
            'norm1_g': norm1_g[l].reshape(1, D_MODEL), 'norm2_g': norm2_g[l].reshape(1, D_MODEL),
            'layer': l, 'w_wide': w_wide, 'b_wide': b_wide[l], 'w_fine': w_fine, 'b_fine': b_fine[l],
            'hgrn_lb': lb_all[l],
            'gla_up_pad': up_pad[l], 'gla_up_b': b_gla_up[l],
            'conv_w': conv_w[l], 'conv_b': conv_b[l], 'lru_w_a': lru_w_a[l], 'lru_b_a': lru_b_a[l],
            'lru_w_x': lru_w_x[l], 'lru_b_x': lru_b_x[l], 'lru_lambda': lru_lambda[l],
            'w_branch': wbr_b, 'w_out': wout_b, 'wq_hi': wq_hi, 'wq_lo': wq_lo,
            'keys_hi': keys_hi, 'keys_lo': keys_lo, 'peer_u': u_b, 'peer_vt': vt_b,
        }

    cond = jnp.concatenate([c, c_ctx[None, :]], axis=0).astype(F32)
    n_dec = c.shape[0]
    mod = _modulation(cond, w_mod, b_mod)
    final_g = final_norm_g.reshape(1, D_MODEL)

    bp, tp, _ = x_prompt.shape
    x = x_prompt.reshape(bp * tp, D_MODEL)
    ctx_states = []
    for l in range(DEPTH):
        mod_l = mod[l, n_dec:n_dec + 1].reshape(1, 1, 6 * D_MODEL)
        x, st = _layer(x, bp, tp, mod_l, layer_params(l), None, True)
        ctx_states.append(st)
    y_prompt = _final_norm(x, final_g).reshape(bp, tp, D_MODEL)
    new_states = tuple(jnp.stack([s[i] for s in ctx_states], axis=1) for i in range(6))

    bd, td, _ = x_sample.shape
    x = _add_position(x_sample, _position_code(td // GRID_W)).reshape(bd * td, D_MODEL)
    for l in range(DEPTH):
        cached = (state_mlstm_C[:, l].astype(F32), state_mlstm_n[:, l].astype(F32),
                  state_mlstm_m[:, l].astype(F32), state_hgrn_S[:, l].astype(F32),
                  state_gla_S[:, l].astype(F32), state_lru_h[:, l].astype(F32))
        mod_l = mod[l, 0:n_dec].reshape(n_dec, 1, 6 * D_MODEL)
        x, _ = _layer(x, bd, td, mod_l, layer_params(l), cached, False)
    y_sample = _final_norm(x, final_g).reshape(bd, td, D_MODEL)
    return (y_prompt, y_sample) + new_states
```

```python
import functools

import jax
import jax.numpy as jnp
from jax import lax
from jax.experimental import pallas as pl
from jax.experimental.pallas import tpu as pltpu

F32 = jnp.float32
BF16 = jnp.bfloat16
HIGHEST = lax.Precision.HIGHEST

D_MODEL = 1024
DEPTH = 4
GRID_W = 64
EPS = 1e-6
NEG_BIG = -1e30
MAX_EXP_ARG = 80.0
MIX_W = 512
N_HEAD = 4
DH = 128
DK_C = 64
R_C = 16
GLA_TAU = 16.0
CONV_W = 4
LRU_C = 8.0
N_BRANCH = 4
N_KEYS = 128
N_EXPERTS = N_KEYS * N_KEYS
PEER_HEADS = 8
PEER_TOPK = 16
POS_BASE = 10000.0

LANE = 128
SUBLANES = 8
VMEM_LIMIT = 56 * 1024 * 1024

COL_GATES = 0
COL_AQ, COL_AK, COL_AV, COL_AO = 4096, 4608, 5120, 5632
COL_BQ, COL_BI, COL_BG = 6144, 6656, 7168
COL_CQ, COL_CK, COL_CV, COL_CG = 7680, 7936, 8192, 8704
COL_DG = 9216
N_WIDE = 9728
COL_BF, COL_DX = 0, 1024
COL_SMALL = 1536
SMALL_W = 256
N_FINE = COL_SMALL + SMALL_W

_WIDE_SRC = ((7216, 11312), (0, 2048), (2064, 2576), (3600, 4624), (4624, 6160), (6704, 7216))
_FINE_SRC = ((2576, 3600), (6192, 6704), (2048, 2064), (6160, 6192))

CHUNK_A = 128
CHUNK_G = 64
SUB_G = 16
CHUNK_D = 128
LRU_HEADS = 2


def _cparams(sem):
    return pltpu.CompilerParams(dimension_semantics=sem, vmem_limit_bytes=VMEM_LIMIT)


def _bdot(a, b):
    return jnp.dot(a.astype(BF16), b.astype(BF16), preferred_element_type=F32)


def _bdot_nt(a, b):
    return lax.dot_general(a.astype(BF16), b.astype(BF16), (((1,), (1,)), ((), ())),
                           preferred_element_type=F32)


def _bdot_tn(a, b):
    return lax.dot_general(a.astype(BF16), b.astype(BF16), (((0,), (0,)), ((), ())),
                           preferred_element_type=F32)


def _split(a):
    hi = a.astype(BF16)
    lo = (a - hi.astype(F32)).astype(BF16)
    return hi, lo


def _dot3(a, b_hi, b_lo):
    a_hi, a_lo = _split(a)
    return (jnp.dot(a_hi, b_hi, preferred_element_type=F32)
            + jnp.dot(a_lo, b_hi, preferred_element_type=F32)
            + jnp.dot(a_hi, b_lo, preferred_element_type=F32))


def _log_sigmoid(z):
    return jnp.minimum(z, 0.0) - jnp.log1p(jnp.exp(-jnp.abs(z)))


def _sigmoid(z):
    return 0.5 * jnp.tanh(0.5 * z) + 0.5


def _gelu(x):
    return 0.5 * x * (1.0 + jnp.tanh(0.7978845608028654 * (x + 0.044715 * x * x * x)))


def _silu(x):
    return x * _sigmoid(x)


def _expm1(y):
    u = jnp.exp(y)
    near = (u - 1.0) * y / jnp.log(u)
    return jnp.where(u == 1.0, y, jnp.where(u < 0.5, u - 1.0, near))


def _mod_kernel(c_ref, w_ref, b_ref, o_ref):
    a = _silu(c_ref[...])
    o_ref[0] = jnp.dot(a, w_ref[0], precision=HIGHEST, preferred_element_type=F32) + b_ref[0]


def _modulation(cond, w_mod, b_mod):
    r = cond.shape[0]
    tc = 1536
    return pl.pallas_call(
        _mod_kernel,
        grid=(DEPTH, 6 * D_MODEL // tc),
        in_specs=[pl.BlockSpec((r, D_MODEL), lambda l, j: (0, 0)),
                  pl.BlockSpec((1, D_MODEL, tc), lambda l, j: (l, 0, j)),
                  pl.BlockSpec((1, 1, tc), lambda l, j: (l, 0, j))],
        out_specs=pl.BlockSpec((1, r, tc), lambda l, j: (l, 0, j)),
        out_shape=jax.ShapeDtypeStruct((DEPTH, r, 6 * D_MODEL), F32),
        compiler_params=_cparams(("parallel", "parallel")),
        name="modulation",
    )(cond, w_mod, b_mod.reshape(DEPTH, 1, 6 * D_MODEL))


def _addpos_kernel(x_ref, p_ref, o_ref):
    o_ref[0] = x_ref[0] + p_ref[...]


def _add_position(x, pos):
    b, t, d = x.shape
    tt = 512
    return pl.pallas_call(
        _addpos_kernel,
        grid=(b, t // tt),
        in_specs=[pl.BlockSpec((1, tt, d), lambda i, j: (i, j, 0)),
                  pl.BlockSpec((tt, d), lambda i, j: (j, 0))],
        out_specs=pl.BlockSpec((1, tt, d), lambda i, j: (i, j, 0)),
        out_shape=jax.ShapeDtypeStruct(x.shape, F32),
        compiler_params=_cparams(("parallel", "parallel")),
        name="add_position",
    )(x, pos)


def _inproj_kernel(x_ref, mod_ref, g_ref, w_ref, b_ref, o_ref, h_ref):
    @pl.when(pl.program_id(1) == 0)
    def _():
        x = x_ref[...]
        inv = lax.rsqrt(jnp.mean(x * x, axis=-1, keepdims=True) + EPS)
        sh = mod_ref[0, :, 0:D_MODEL]
        sc = mod_ref[0, :, D_MODEL:2 * D_MODEL]
        h_ref[...] = (x * inv * g_ref[...] * (1.0 + sc) + sh).astype(BF16)

    acc = jnp.dot(h_ref[...], w_ref[...], preferred_element_type=F32) + b_ref[...]
    o_ref[...] = acc.astype(o_ref.dtype)


def _inproj(x, mod_l, norm_g, w_pack, b_pack, seq_len, layer, tc, out_dtype, name):
    n = x.shape[0]
    width = w_pack.shape[-1]
    tn = 1024
    rows = mod_l.shape[0]
    if rows == 1:
        mod_map = lambda i, j: (0, 0, 0)
    else:
        mod_map = lambda i, j: ((i * tn) // seq_len, 0, 0)
    return pl.pallas_call(
        _inproj_kernel,
        grid=(n // tn, width // tc),
        in_specs=[pl.BlockSpec((tn, D_MODEL), lambda i, j: (i, 0)),
                  pl.BlockSpec((1, 1, 6 * D_MODEL), mod_map),
                  pl.BlockSpec((1, D_MODEL), lambda i, j: (0, 0)),
                  pl.BlockSpec((None, D_MODEL, tc), lambda i, j: (layer, 0, j)),
                  pl.BlockSpec((1, tc), lambda i, j: (0, j))],
        out_specs=pl.BlockSpec((tn, tc), lambda i, j: (i, j)),
        out_shape=jax.ShapeDtypeStruct((n, width), out_dtype),
        scratch_shapes=[pltpu.VMEM((tn, D_MODEL), BF16)],
        compiler_params=_cparams(("parallel", "arbitrary")),
        name=name,
    )(x, mod_l, norm_g, w_pack, b_pack)


SCAN_GROUP = 4


def _lockstep(gens):
    alive = list(gens)
    while alive:
        still = []
        for gen in alive:
            try:
                next(gen)
                still.append(gen)
            except StopIteration:
                pass
        alive = still
        if alive:
            yield


def _per_sequence(body, n_group, shared):
    def kern(*refs):
        def run(phase):
            stages = [body(*[r if i in shared else r.at[g] for i, r in enumerate(refs)], phase=phase)
                      for g in range(n_group)]
            for _ in _lockstep([s for s in stages if s is not None]):
                pass

        pl.when(pl.program_id(1) == 0)(lambda: run('init'))
        run('main')
        pl.when(pl.program_id(1) == pl.num_programs(1) - 1)(lambda: run('emit'))
    return kern


def _cumsum_rows(x, rev):
    k = 1
    while k < x.shape[0]:
        x = x + _shift_rows(x, k, rev, 0.0)
        k *= 2
    return x


def _causal_mask(length, rev):
    row = lax.broadcasted_iota(jnp.int32, (length, length), 0)
    col = lax.broadcasted_iota(jnp.int32, (length, length), 1)
    return (col >= row) if rev else (col <= row)


def _mlstm_kernel(*refs, has_init, emit_state, phase):
    L = CHUNK_A
    ins = list(refs)
    dirs = [ins[0:4], ins[4:8]]
    pos = 8
    if has_init:
        c0_ref, n0_ref, m0_ref = ins[pos:pos + 3]
        pos += 3
    outs = ins[pos:pos + 2]
    pos += 2
    if emit_state:
        cout_ref, nout_ref, mout_ref = ins[pos:pos + 3]
        pos += 3
    cs_ref, ns_ref, ms_ref = ins[pos:pos + 3]

    if phase == 'init':
        if has_init:
            cs_ref[...] = c0_ref[...]
            ns_ref[...] = n0_ref[...]
            ms_ref[...] = m0_ref[...]
        else:
            cs_ref[...] = jnp.zeros_like(cs_ref)
            ns_ref[...] = jnp.zeros_like(ns_ref)
            ms_ref[...] = jnp.zeros_like(ms_ref)
        return
    if phase == 'emit':
        if emit_state:
            cout_ref[...] = cs_ref[...]
            nout_ref[...] = ns_ref[...]
            mout_ref[...] = ms_ref[...]
        return

    def direction(d):
        rev = d == 1
        q_ref, k_ref, v_ref, s_ref = dirs[d]
        o_ref = outs[d]
        mask = _causal_mask(L, rev)
        sm = s_ref[:, 0:LANE]
        lf = _log_sigmoid(sm)
        bc = jnp.dot(mask.astype(F32), lf, precision=HIGHEST,
                     preferred_element_type=F32)
        sm_t = sm.T
        bc_t = bc.T
        last = 0 if rev else L - 1
        heads = range(N_HEAD)
        hsl = [slice(h * DH, (h + 1) * DH) for h in heads]
        yield
        li_c, b_c, logw, m_prev, m_t, w_state = [], [], [], [], [], []
        for h in heads:
            ci, cf = d * N_HEAD + h, 2 * N_HEAD + d * N_HEAD + h
            li_c.append(sm[:, ci:ci + 1])
            b_c.append(bc[:, cf:cf + 1])
            li_r, b_r = sm_t[ci:ci + 1, :], bc_t[cf:cf + 1, :]
            m_prev.append(ms_ref[d, h][:, 0:1])
            logw.append(jnp.where(mask, b_c[h] - b_r + li_r, NEG_BIG))
            from_state = b_c[h] + m_prev[h]
            m_t.append(jnp.maximum(from_state, jnp.max(logw[h], axis=-1, keepdims=True)))
            w_state.append(jnp.exp(from_state - m_t[h]))
        q = [q_ref[:, hsl[h]].astype(F32) for h in heads]
        k = [k_ref[:, hsl[h]].astype(F32) * (DH ** -0.5) for h in heads]
        v = [v_ref[:, hsl[h]] for h in heads]
        c_st = [cs_ref[d, h] for h in heads]
        n_st = [ns_ref[d, h] for h in heads]
        qk = [_bdot_nt(q[h], k[h]) for h in heads]
        qc = [_bdot(q[h], c_st[h]) for h in heads]
        yield
        for h in heads:
            scores = qk[h] * jnp.exp(logw[h] - m_t[h])
            num = w_state[h] * qc[h] + _bdot(scores, v[h])
            den = (w_state[h] * jnp.sum(q[h] * n_st[h], axis=-1, keepdims=True)
                   + jnp.sum(scores, axis=-1, keepdims=True))
            floor = jnp.exp(jnp.minimum(-m_t[h], MAX_EXP_ARG))
            o_ref[:, hsl[h]] = num / jnp.maximum(jnp.abs(den), floor)
        yield
        for h in heads:
            m_new = m_t[h][last:last + 1, :]
            b_last = b_c[h][last:last + 1, :]
            kw = k[h] * jnp.exp(b_last - b_c[h] + li_c[h] - m_new)
            decay = jnp.exp(b_last + m_prev[h] - m_new)
            cs_ref[d, h] = decay * c_st[h] + _bdot_tn(kw, v[h])
            ns_ref[d, h] = decay * n_st[h] + jnp.sum(kw, axis=0, keepdims=True)
            ms_ref[d, h] = jnp.broadcast_to(m_new, (1, LANE))

    yield from _lockstep([direction(0), direction(1)])


def _scan_maps(nc):
    fwd = lambda cb: (lambda b, c: (b, c, cb))
    bwd = lambda cb: (lambda b, c: (b, nc - 1 - c, cb))
    return fwd, bwd


def _mlstm(pw, pf, bsz, seq, state, emit_state):
    L = CHUNK_A
    nc = seq // L
    n = bsz * seq
    g = min(bsz, SCAN_GROUP)
    has_init = state is not None
    pw3 = pw.reshape(bsz, seq, N_WIDE)
    pf3 = pf.reshape(bsz, seq, N_FINE)
    fwd, bwd = _scan_maps(nc)

    in_specs, args = [], []
    for mk in (fwd, bwd):
        for col in (COL_AQ, COL_AK, COL_AV):
            in_specs.append(pl.BlockSpec((g, L, MIX_W), mk(col // MIX_W)))
            args.append(pw3)
        in_specs.append(pl.BlockSpec((g, L, SMALL_W), mk(COL_SMALL // SMALL_W)))
        args.append(pf3)
    st_specs = [pl.BlockSpec((g, 2, N_HEAD, DH, DH), lambda b, c: (b, 0, 0, 0, 0)),
                pl.BlockSpec((g, 2, N_HEAD, 1, DH), lambda b, c: (b, 0, 0, 0, 0)),
                pl.BlockSpec((g, 2, N_HEAD, 1, LANE), lambda b, c: (b, 0, 0, 0, 0))]
    st_shapes = [jax.ShapeDtypeStruct((bsz, 2, N_HEAD, DH, DH), F32),
                 jax.ShapeDtypeStruct((bsz, 2, N_HEAD, 1, DH), F32),
                 jax.ShapeDtypeStruct((bsz, 2, N_HEAD, 1, LANE), F32)]
    if has_init:
        c0, n0, m0 = state
        in_specs += st_specs
        args += [c0, n0.reshape(bsz, 2, N_HEAD, 1, DH),
                 jnp.broadcast_to(m0[..., None, None], (bsz, 2, N_HEAD, 1, LANE))]
    out_specs = [pl.BlockSpec((g, L, MIX_W), fwd(0)), pl.BlockSpec((g, L, MIX_W), bwd(0))]
    out_shape = [jax.ShapeDtypeStruct((bsz, seq, MIX_W), F32)] * 2
    if emit_state:
        out_specs += st_specs
        out_shape += st_shapes
    body = functools.partial(_mlstm_kernel, has_init=has_init, emit_state=emit_state)
    res = pl.pallas_call(
        _per_sequence(body, g, ()),
        grid=(bsz // g, nc),
        in_specs=in_specs,
        out_specs=out_specs,
        out_shape=out_shape,
        scratch_shapes=[pltpu.VMEM((g, 2, N_HEAD, DH, DH), F32),
                        pltpu.VMEM((g, 2, N_HEAD, 1, DH), F32),
                        pltpu.VMEM((g, 2, N_HEAD, 1, LANE), F32)],
        compiler_params=_cparams(("parallel", "arbitrary")),
        name="mlstm_scan",
    )(*args)
    o_f, o_b = res[0].reshape(n, MIX_W), res[1].reshape(n, MIX_W)
    new_state = None
    if emit_state:
        new_state = (res[2], res[3].reshape(bsz, 2, N_HEAD, DH), res[4][:, :, :, 0, 0])
    return o_f, o_b, new_state


def _gla_direction(q, k, la, v, s_ref, d, o_ref, *, rev, dk):
    L, S = CHUNK_G, SUB_G
    nsub = L // S
    mask = _causal_mask(L, rev)
    b = _cumsum_rows(la, rev)
    bx = b - la
    ref_rows = [bx[i * S + (S - 1 if rev else 0):i * S + (S - 1 if rev else 0) + 1, :]
                for i in range(nsub)]
    bref = jnp.concatenate([jnp.broadcast_to(r, (S, r.shape[1])) for r in ref_rows], axis=0)
    q_hat = q * jnp.exp(b - bref)
    rowid = lax.broadcasted_iota(jnp.int32, (L, 1), 0)
    k_hat = []
    for i in range(nsub):
        seen = (rowid >= i * S) if rev else (rowid < (i + 1) * S)
        k_hat.append(k * jnp.exp(jnp.where(seen, ref_rows[i] - b, NEG_BIG)))
    last = 0 if rev else L - 1
    b_last = b[last:last + 1, :]
    q_state = q * jnp.exp(b)
    k_state = k * jnp.exp(b_last - b)
    e_last = jnp.exp(b_last)
    yield
    ksl = [slice(h * dk, (h + 1) * dk) for h in range(N_HEAD)]
    vsl = [slice(h * DH, (h + 1) * DH) for h in range(N_HEAD)]
    blocks = [[_bdot_nt(q_hat[i * S:(i + 1) * S, ksl[h]], k_hat[i][:, ksl[h]]) for i in range(nsub)]
              for h in range(N_HEAD)]
    states = [s_ref[d, h] for h in range(N_HEAD)]
    inter = [_bdot_nt(q_state[:, ksl[h]], states[h]) for h in range(N_HEAD)]
    grown = [_bdot_tn(v[:, vsl[h]], k_state[:, ksl[h]]) for h in range(N_HEAD)]
    yield
    for h in range(N_HEAD):
        scores = jnp.where(mask, jnp.concatenate(blocks[h], axis=0), 0.0)
        o_ref[:, vsl[h]] = inter[h] + _bdot(scores, v[:, vsl[h]])
    yield
    for h in range(N_HEAD):
        s_ref[d, h] = states[h] * e_last[:, ksl[h]] + grown[h]


def _scan_state_io(ins, pos, has_init, emit_state, n_out):
    s0_ref = sout_ref = None
    if has_init:
        s0_ref = ins[pos]
        pos += 1
    outs = ins[pos:pos + n_out]
    pos += n_out
    if emit_state:
        sout_ref = ins[pos]
        pos += 1
    return s0_ref, outs, sout_ref, ins[pos]


def _scan_state_phase(phase, s_ref, s0_ref, sout_ref):
    if phase == 'init':
        if s0_ref is not None:
            s_ref[...] = s0_ref[...]
        else:
            s_ref[...] = jnp.zeros_like(s_ref)
    elif phase == 'emit' and sout_ref is not None:
        sout_ref[...] = s_ref[...]
    return phase != 'main'


def _hgrn_kernel(*refs, has_init, emit_state, phase):
    ins = list(refs)
    lb_ref = ins[6]
    s0_ref, outs, sout_ref, s_ref = _scan_state_io(ins, 7, has_init, emit_state, 2)
    if _scan_state_phase(phase, s_ref, s0_ref, sout_ref):
        return
    chains = []
    for d in range(2):
        q_ref, z_ref, v_ref = ins[3 * d:3 * d + 3]
        z = z_ref[...]
        lb = lb_ref[d:d + 1, :]
        la = _log_sigmoid(z) + jnp.log1p(lb * jnp.exp(jnp.minimum(-z, MAX_EXP_ARG)))
        k = (1.0 - lb) * _sigmoid(-z)
        q = q_ref[...].astype(F32) * (DH ** -0.5)
        chains.append(_gla_direction(q, k, la, v_ref[...], s_ref, d, outs[d], rev=d == 1, dk=DH))
    yield from _lockstep(chains)


def _gla_kernel(*refs, has_init, emit_state, phase):
    ins = list(refs)
    up_ref, upb_ref = ins[8], ins[9]
    s0_ref, outs, sout_ref, s_ref = _scan_state_io(ins, 10, has_init, emit_state, 2)
    if _scan_state_phase(phase, s_ref, s0_ref, sout_ref):
        return
    chains = []
    for d in range(2):
        q_ref, k_ref, v_ref, sm_ref = ins[4 * d:4 * d + 4]
        zg = jnp.dot(sm_ref[...], up_ref[d], precision=HIGHEST,
                     preferred_element_type=F32) + upb_ref[d:d + 1, :]
        la = _log_sigmoid(zg) * (1.0 / GLA_TAU)
        q = q_ref[...].astype(F32) * (DK_C ** -0.5)
        k = k_ref[...].astype(F32)
        chains.append(_gla_direction(q, k, la, v_ref[...], s_ref, d, outs[d], rev=d == 1,
                                     dk=DK_C))
    yield from _lockstep(chains)


def _gated_scan(kernel, p, bsz, seq, cols, extra, extra_specs, dk, state_t, emit_state, name):
    L = CHUNK_G
    nc = seq // L
    n = bsz * seq
    g = min(bsz, SCAN_GROUP)
    has_init = state_t is not None
    src = {'wide': p[0].reshape(bsz, seq, N_WIDE), 'fine': p[1].reshape(bsz, seq, N_FINE)}
    fwd, bwd = _scan_maps(nc)

    in_specs, args = [], []
    for d, mk in enumerate((fwd, bwd)):
        for which, off, width in cols[d]:
            in_specs.append(pl.BlockSpec((g, L, width), mk(off // width)))
            args.append(src[which])
    shared = tuple(range(len(args), len(args) + len(extra)))
    in_specs += extra_specs
    args += extra
    st_spec = pl.BlockSpec((g, 2, N_HEAD, DH, dk), lambda b, c: (b, 0, 0, 0, 0))
    if has_init:
        in_specs.append(st_spec)
        args.append(state_t)
    out_specs = [pl.BlockSpec((g, L, MIX_W), fwd(0)), pl.BlockSpec((g, L, MIX_W), bwd(0))]
    out_shape = [jax.ShapeDtypeStruct((bsz, seq, MIX_W), F32)] * 2
    if emit_state:
        out_specs.append(st_spec)
        out_shape.append(jax.ShapeDtypeStruct((bsz, 2, N_HEAD, DH, dk), F32))
    body = functools.partial(kernel, has_init=has_init, emit_state=emit_state)
    res = pl.pallas_call(
        _per_sequence(body, g, shared),
        grid=(bsz // g, nc),
        in_specs=in_specs,
        out_specs=out_specs,
        out_shape=out_shape,
        scratch_shapes=[pltpu.VMEM((g, 2, N_HEAD, DH, dk), F32)],
        compiler_params=_cparams(("parallel", "arbitrary")),
        name=name,
    )(*args)
    return res[0].reshape(n, MIX_W), res[1].reshape(n, MIX_W), (res[2] if emit_state else None)


def _hgrn(p, bsz, seq, lb, state_t, emit_state):
    cols = [[('wide', COL_BQ, MIX_W), ('fine', COL_BF + d * MIX_W, MIX_W), ('wide', COL_BI, MIX_W)]
            for d in range(2)]
    return _gated_scan(_hgrn_kernel, p, bsz, seq, cols, [lb],
                       [pl.BlockSpec((2, MIX_W), lambda b, c: (0, 0))], DH, state_t, emit_state,
                       "hgrn2_scan")


def _gla(p, bsz, seq, up_pad, up_b, state_t, emit_state):
    kw = N_HEAD * DK_C
    cols = [[('wide', COL_CQ, kw), ('wide', COL_CK, kw), ('wide', COL_CV, MIX_W),
             ('fine', COL_SMALL, SMALL_W)] for _ in range(2)]
    return _gated_scan(_gla_kernel, p, bsz, seq, cols, [up_pad, up_b],
                       [pl.BlockSpec((2, SMALL_W, kw), lambda b, c: (0, 0, 0)),
                        pl.BlockSpec((2, kw), lambda b, c: (0, 0))], DK_C, state_t, emit_state,
                       "gla_scan")


def _shift_rows(x, k, rev, fill):
    n = x.shape[0]
    rowid = lax.broadcasted_iota(jnp.int32, x.shape, 0)
    if rev:
        return jnp.where(rowid >= n - k, fill, pltpu.roll(x, n - k, 0))
    return jnp.where(rowid < k, fill, pltpu.roll(x, k, 0))


def _lru_kernel(*refs, seq, has_init, emit_state):
    L = CHUNK_D
    nc = seq // L
    ins = list(refs)
    (dx_ref, dg_ref, cw_ref, cb_ref, wah_ref, wal_ref, ba_ref, wxh_ref, wxl_ref, bx_ref,
     lam_ref) = ins[0:11]
    pos = 11
    h0_ref = hout_ref = None
    if has_init:
        h0_ref = ins[pos]
        pos += 1
    y_ref = ins[pos]
    pos += 1
    if emit_state:
        hout_ref = ins[pos]
        pos += 1
    pad_ref, hb_ref, xd_ref, xh_ref, xl_ref = ins[pos:pos + 5]
    width = LRU_HEADS * LANE

    zeros8 = jnp.zeros((8, width), F32)
    pad_ref[0:8, :] = zeros8
    pad_ref[8:8 + seq, :] = dx_ref[...]
    pad_ref[8 + seq:16 + seq, :] = zeros8
    lam = lam_ref[...]
    sp = jnp.maximum(-lam, 0.0) + jnp.log1p(jnp.exp(-jnp.abs(lam)))
    cw = cw_ref[...]
    cb = cb_ref[...]

    def conv_body(c, carry):
        start = pl.multiple_of(c * L, L)
        win = pad_ref[pl.ds(start, L + 16), :]
        xd = cb + sum(cw[j:j + 1, :] * win[7 + j:7 + j + L, :] for j in range(CONV_W))
        x_hi, x_lo = _split(xd)
        xd_ref[pl.ds(start, L), :] = xd
        xh_ref[pl.ds(start, L), :] = x_hi
        xl_ref[pl.ds(start, L), :] = x_lo
        return carry

    lax.fori_loop(0, nc, conv_body, 0)

    def gate_dot(x_hi, x_lo, w_hi, w_lo):
        return (jnp.dot(x_hi, w_hi, preferred_element_type=F32)
                + jnp.dot(x_lo, w_hi, preferred_element_type=F32)
                + jnp.dot(x_hi, w_lo, preferred_element_type=F32))

    def chunk_scan(c, carry, k, d, out):
        rev = d == 1
        cols = slice(k * LANE, (k + 1) * LANE)
        start = pl.multiple_of(c * L, L)
        rows = pl.ds(start, L)
        xd, x_hi, x_lo = xd_ref[rows, cols], xh_ref[rows, cols], xl_ref[rows, cols]
        yield
        za = gate_dot(x_hi, x_lo, wah_ref[d, k], wal_ref[d, k])
        zx = gate_dot(x_hi, x_lo, wxh_ref[d, k], wxl_ref[d, k])
        yield
        r = _sigmoid(za + ba_ref[d:d + 1, cols])
        ig = _sigmoid(zx + bx_ref[d:d + 1, cols])
        log_a = -LRU_C * r * sp[d:d + 1, cols]
        a = jnp.exp(log_a)
        u = jnp.sqrt(jnp.maximum(-_expm1(2.0 * log_a), 0.0)) * (ig * xd)
        yield
        step = 1
        while step < L:
            u = a * _shift_rows(u, step, rev, 0.0) + u
            a = a * _shift_rows(a, step, rev, 1.0)
            step *= 2
            yield
        h = a * carry + u
        last = 0 if rev else L - 1
        out.extend((start, h, h[last:last + 1, :]))

    chains = [(k, d) for k in range(LRU_HEADS) for d in range(2)]

    def scan_body(i, carry):
        res = [[] for _ in chains]
        gens = [chunk_scan(i if d == 0 else nc - 1 - i, carry[n], k, d, res[n])
                for n, (k, d) in enumerate(chains)]
        for _ in _lockstep(gens):
            pass
        for n, (k, d) in enumerate(chains):
            dst = y_ref if d == 0 else hb_ref
            dst[pl.ds(res[n][0], L), k * LANE:(k + 1) * LANE] = res[n][1]
        return tuple(r[2] for r in res)

    def gate_body(c, carry):
        rows = pl.ds(pl.multiple_of(c * L, L), L)
        y_ref[rows, :] = (y_ref[rows, :] + hb_ref[rows, :]) * _gelu(dg_ref[rows, :].astype(F32))
        return carry

    if has_init:
        init = tuple(h0_ref[0, d:d + 1, k * LANE:(k + 1) * LANE] for k, d in chains)
    else:
        init = tuple(jnp.zeros((1, LANE), F32) for _ in chains)
    fin = lax.fori_loop(0, nc, scan_body, init)
    lax.fori_loop(0, nc, gate_body, 0)
    if emit_state:
        for n, (k, d) in enumerate(chains):
            hout_ref[0, d:d + 1, k * LANE:(k + 1) * LANE] = fin[n]


def _lru(pw, pf, bsz, seq, conv_w, conv_b, wa, ba, wx, bx, lam, h0, emit_state):
    n = bsz * seq
    has_init = h0 is not None
    wah, wal = _split(wa)
    wxh, wxl = _split(wx)
    width = LRU_HEADS * LANE
    gate_w = pl.BlockSpec((2, LRU_HEADS, DH, DH), lambda b, h: (0, h, 0, 0))
    vec2 = pl.BlockSpec((2, width), lambda b, h: (0, h))
    in_specs = [pl.BlockSpec((seq, width), lambda b, h: (b, COL_DX // width + h)),
                pl.BlockSpec((seq, width), lambda b, h: (b, COL_DG // width + h)),
                pl.BlockSpec((CONV_W, width), lambda b, h: (0, h)),
                pl.BlockSpec((1, width), lambda b, h: (0, h)),
                gate_w, gate_w, vec2, gate_w, gate_w, vec2, vec2]
    args = [pf, pw, conv_w, conv_b.reshape(1, MIX_W), wah, wal, ba, wxh, wxl, bx, lam]
    st_spec = pl.BlockSpec((1, 2, width), lambda b, h: (b, 0, h))
    if has_init:
        in_specs.append(st_spec)
        args.append(h0)
    out_specs = [pl.BlockSpec((seq, width), lambda b, h: (b, h))]
    out_shape = [jax.ShapeDtypeStruct((n, MIX_W), F32)]
    if emit_state:
        out_specs.append(st_spec)
        out_shape.append(jax.ShapeDtypeStruct((bsz, 2, MIX_W), F32))
    res = pl.pallas_call(
        functools.partial(_lru_kernel, seq=seq, has_init=has_init, emit_state=emit_state),
        grid=(bsz, N_HEAD // LRU_HEADS),
        in_specs=in_specs,
        out_specs=out_specs,
        out_shape=out_shape,
        scratch_shapes=[pltpu.VMEM((seq + 16, width), F32), pltpu.VMEM((seq, width), F32),
                        pltpu.VMEM((seq, width), F32), pltpu.VMEM((seq, width), BF16),
                        pltpu.VMEM((seq, width), BF16)],
        compiler_params=_cparams(("parallel", "parallel")),
        name="conv_rglru",
    )(*args)
    return res[0], (res[1] if emit_state else None)


def _head_rms(x):
    parts = []
    for h in range(N_HEAD):
        xh = x[:, h * DH:(h + 1) * DH]
        parts.append(xh * lax.rsqrt(jnp.mean(xh * xh, axis=-1, keepdims=True) + EPS))
    return jnp.concatenate(parts, axis=-1)


def _merge_kernel(af_ref, ab_ref, bf_ref, bb_ref, cf_ref, cb_ref, yd_ref, ao_ref, bg_ref, cg_ref,
                  gt0_ref, gt1_ref, gt2_ref, gt3_ref, x_ref, mod_ref, g2_ref, wbr_ref, wout_ref,
                  xo_ref, h2_ref, h2b_ref):
    ya = _sigmoid(ao_ref[...].astype(F32)) * _head_rms(af_ref[...] + ab_ref[...])
    yb = _silu(bg_ref[...].astype(F32)) * _head_rms(bf_ref[...] + bb_ref[...])
    yc = _silu(cg_ref[...].astype(F32)) * _head_rms(cf_ref[...] + cb_ref[...])
    merged = None
    gate_refs = (gt0_ref, gt1_ref, gt2_ref, gt3_ref)
    for i, y in enumerate((ya, yb, yc, yd_ref[...])):
        proj = jnp.dot(y.astype(BF16), wbr_ref[i], preferred_element_type=F32)
        term = _sigmoid(gate_refs[i][...].astype(F32)) * proj
        merged = term if merged is None else merged + term
    out = jnp.dot(merged.astype(BF16), wout_ref[...], preferred_element_type=F32)
    g1 = mod_ref[0, :, 2 * D_MODEL:3 * D_MODEL]
    sh2 = mod_ref[0, :, 3 * D_MODEL:4 * D_MODEL]
    sc2 = mod_ref[0, :, 4 * D_MODEL:5 * D_MODEL]
    x = x_ref[...] + g1 * out
    xo_ref[...] = x
    inv = lax.rsqrt(jnp.mean(x * x, axis=-1, keepdims=True) + EPS)
    h2 = x * inv * g2_ref[...] * (1.0 + sc2) + sh2
    h2_ref[...] = h2
    h2b_ref[...] = h2.astype(BF16)


def _merge(mix_outs, yd, p, x, mod_l, norm2_g, w_branch, w_out, seq_len, layer):
    n = x.shape[0]
    tn = 256
    rows = mod_l.shape[0]
    mod_map = (lambda i: (0, 0, 0)) if rows == 1 else (lambda i: ((i * tn) // seq_len, 0, 0))
    tok = lambda cb: (lambda i: (i, cb))
    in_specs = [pl.BlockSpec((tn, MIX_W), tok(0))] * 7
    in_specs += [pl.BlockSpec((tn, MIX_W), tok(COL_AO // MIX_W)),
                 pl.BlockSpec((tn, MIX_W), tok(COL_BG // MIX_W)),
                 pl.BlockSpec((tn, MIX_W), tok(COL_CG // MIX_W)),
                 *[pl.BlockSpec((tn, D_MODEL), tok(COL_GATES // D_MODEL + i)) for i in range(N_BRANCH)],
                 pl.BlockSpec((tn, D_MODEL), tok(0)),
                 pl.BlockSpec((1, 1, 6 * D_MODEL), mod_map),
                 pl.BlockSpec((1, D_MODEL), lambda i: (0, 0)),
                 pl.BlockSpec((None, N_BRANCH, MIX_W, D_MODEL), lambda i: (layer, 0, 0, 0)),
                 pl.BlockSpec((None, D_MODEL, D_MODEL), lambda i: (layer, 0, 0))]
    out_spec = pl.BlockSpec((tn, D_MODEL), tok(0))
    return pl.pallas_call(
        _merge_kernel,
        grid=(n // tn,),
        in_specs=in_specs,
        out_specs=[out_spec, out_spec, out_spec],
        out_shape=[jax.ShapeDtypeStruct((n, D_MODEL), F32), jax.ShapeDtypeStruct((n, D_MODEL), F32),
                   jax.ShapeDtypeStruct((n, D_MODEL), BF16)],
        compiler_params=_cparams(("parallel",)),
        name="branch_merge",
    )(*mix_outs, yd, p, p, p, p, p, p, p, x, mod_l, norm2_g, w_branch, w_out)


ROUTE_TN = 512
ROUTE_LG = LANE


def _top16_exact(vals):
    n_rows = vals.shape[0]
    rowid = lax.broadcasted_iota(jnp.int32, vals.shape, 0).astype(F32)
    rank = jnp.full(vals.shape, float(PEER_TOPK), F32)
    tops = []
    for r in range(PEER_TOPK):
        m = jnp.max(vals, axis=0, keepdims=True)
        idx = jnp.min(jnp.where(vals == m, rowid, float(n_rows)), axis=0, keepdims=True)
        sel = rowid == idx
        rank = jnp.where(sel, float(r), rank)
        vals = jnp.where(sel, -jnp.inf, vals)
        tops.append(m)
    return jnp.concatenate(tops, axis=0), rank


_TAKEN_BASE = -3.0e38
_TAKEN_STEP = 2.0e36
_TAKEN_BELOW = _TAKEN_BASE + 0.5 * _TAKEN_STEP


def _top16_quick_many(vals_list):
    work = list(vals_list)
    tops = [[] for _ in work]
    for r in range(PEER_TOPK):
        for i in range(len(work)):
            m = jnp.max(work[i], axis=0, keepdims=True)
            work[i] = jnp.where(work[i] == m, _TAKEN_BASE - r * _TAKEN_STEP, work[i])
            tops[i].append(m)
    out = []
    for i in range(len(work)):
        taken = work[i] <= _TAKEN_BELOW
        rank = jnp.where(taken, jnp.round((_TAKEN_BASE - work[i]) * (1.0 / _TAKEN_STEP)),
                         float(PEER_TOPK))
        n_taken = jnp.sum(jnp.where(taken, 1.0, 0.0), axis=0, keepdims=True)
        out.append((jnp.concatenate(tops[i], axis=0), rank, n_taken))
    return out


_CAND_PIECES = (('a', 0, 0, 16), ('a', 0, 8, 16), ('a', 1, 0, 8), ('a', 2, 0, 5), ('a', 3, 0, 4),
                ('b', 0, 0, (4, 8)), ('b', 0, 8, (8, 16)), ('b', 1, 0, (4, 8)), ('b', 2, 0, (4, 5)))


def _pair_pieces(x1, x2, combine):
    out = []
    for kind, fixed, off, _ in _CAND_PIECES:
        if kind == 'a':
            out.append(combine(x1[fixed:fixed + 1, :], x2[off:off + 8, :]))
        else:
            out.append(combine(x1[off:off + 8, :], x2[fixed:fixed + 1, :]))
    return out


def _candidate_sums(s1, s2):
    K = PEER_TOPK
    n = s1.shape[1]
    iota8 = lax.broadcasted_iota(jnp.int32, (8, n), 0)
    sums = _pair_pieces(s1, s2, lambda x, y: x + y)
    vals, poss = [], []
    for (kind, fixed, off, lim), sm in zip(_CAND_PIECES, sums):
        idx = iota8 + off
        if kind == 'a':
            valid = idx < lim
            pos = fixed * K + idx
        else:
            valid = (idx >= lim[0]) & (idx < lim[1])
            pos = idx * K + fixed
        vals.append(jnp.where(valid, sm, -jnp.inf))
        poss.append(jnp.where(valid, pos, K * K).astype(F32))
    return jnp.concatenate(vals, axis=0), jnp.concatenate(poss, axis=0)


def _choose_exact(vals, posid):
    K = PEER_TOPK
    chosen = jnp.zeros(vals.shape, F32)
    for _ in range(K):
        m = jnp.max(vals, axis=0, keepdims=True)
        idx = jnp.min(jnp.where(vals == m, posid, float(K * K)), axis=0, keepdims=True)
        sel = posid == idx
        chosen = jnp.where(sel, 1.0, chosen)
        vals = jnp.where(sel, -jnp.inf, vals)
    return chosen


def _choose_quick_many(vals_list):
    work = list(vals_list)
    for _ in range(PEER_TOPK):
        for i in range(len(work)):
            m = jnp.max(work[i], axis=0, keepdims=True)
            work[i] = jnp.where(work[i] == m, _TAKEN_BASE, work[i])
    out = []
    for w in work:
        chosen = jnp.where(w == _TAKEN_BASE, 1.0, 0.0)
        out.append((chosen, jnp.sum(chosen, axis=0, keepdims=True)))
    return out


def _route_kernel(h_ref, wqh_ref, wql_ref, kh_ref, kl_ref, a_ref, b_ref, r2_ref, c_ref,
                  qh_ref, ql_ref, s_ref, top_ref, rank_ref, ch_ref):
    K = PEER_TOPK
    n_lg = ROUTE_TN // ROUTE_LG
    q = _dot3(h_ref[...], wqh_ref[...], wql_ref[...])
    q_hi, q_lo = _split(q)
    for i in range(2 * PEER_HEADS):
        qh_ref[i] = q_hi[:, i * N_KEYS:(i + 1) * N_KEYS]
        ql_ref[i] = q_lo[:, i * N_KEYS:(i + 1) * N_KEYS]
    nt = (((1,), (1,)), ((), ()))
    lanes = [slice(lg * ROUTE_LG, (lg + 1) * ROUTE_LG) for lg in range(n_lg)]

    def miscount(counts):
        worst = None
        for cnt in counts:
            dev = jnp.abs(cnt - float(K))
            worst = dev if worst is None else jnp.maximum(worst, dev)
        return jnp.max(worst) > 0.0

    def head_body(h, carry):
        for half in range(2):
            kh, kl = kh_ref[h, half], kl_ref[h, half]
            qh, ql = qh_ref[2 * h + half], ql_ref[2 * h + half]
            s_ref[half] = (lax.dot_general(kh, qh, nt, preferred_element_type=F32)
                           + lax.dot_general(kl, qh, nt, preferred_element_type=F32)
                           + lax.dot_general(kh, ql, nt, preferred_element_type=F32))

        problems = [(half, lg) for half in range(2) for lg in range(n_lg)]
        quick = _top16_quick_many([s_ref[half, :, lanes[lg]] for half, lg in problems])
        counts = []
        for (half, lg), (top, rank, n_taken) in zip(problems, quick):
            top_ref[half, lg] = top
            rank_ref[half, lg] = rank
            counts.append(n_taken)

        @pl.when(miscount(counts))
        def _():
            for half in range(2):
                for lg in range(n_lg):
                    top, rank = _top16_exact(s_ref[half, :, lanes[lg]])
                    top_ref[half, lg] = top
                    rank_ref[half, lg] = rank

        cands = [_candidate_sums(top_ref[0, lg], top_ref[1, lg])[0] for lg in range(n_lg)]
        counts = []
        for lg, (chosen, n_taken) in enumerate(_choose_quick_many(cands)):
            ch_ref[lg] = chosen
            counts.append(n_taken)

        @pl.when(miscount(counts))
        def _():
            for lg in range(n_lg):
                vals, posid = _candidate_sums(top_ref[0, lg], top_ref[1, lg])
                ch_ref[lg] = _choose_exact(vals, posid)

        for lg in range(n_lg):
            ls = lanes[lg]
            st1, st2 = s_ref[0, :, ls], s_ref[1, :, ls]
            s1, s2 = top_ref[0, lg], top_ref[1, lg]
            rank1, rank2 = rank_ref[0, lg], rank_ref[1, lg]
            chosen = ch_ref[lg]
            ch = [chosen[8 * i:8 * i + 8, :] for i in range(len(_CAND_PIECES))]
            e1 = jnp.exp(s1 - s1[0:1, :])
            e2 = jnp.exp(s2 - s2[0:1, :])
            pair = _pair_pieces(e1, e2, lambda x, y: x * y)
            z = sum(jnp.sum(c * p, axis=0, keepdims=True) for c, p in zip(ch, pair))
            iota8 = lax.broadcasted_iota(jnp.int32, (8, ROUTE_LG), 0)
            low = ch[5] + ch[7] + ch[8]
            for a, cnt in ((3, ch[4]), (2, ch[3]), (1, ch[2]), (0, ch[0] + ch[1])):
                low = jnp.where(iota8 == a, jnp.sum(cnt, axis=0, keepdims=True), low)
            counts = jnp.concatenate([low, ch[6]], axis=0)
            c_dense = jnp.zeros((N_KEYS, ROUTE_LG), F32)
            for a in range(K):
                c_dense = jnp.where(rank1 == float(a), counts[a:a + 1, :], c_dense)
            a_ref[h, :, ls] = jnp.where(rank1 < float(K), jnp.exp(st1 - s1[0:1, :]), 0.0) / z
            b_ref[h, :, ls] = jnp.where(rank2 < float(K), jnp.exp(st2 - s2[0:1, :]), 0.0).astype(BF16)
            r2_ref[h, :, ls] = rank2.astype(BF16)
            c_ref[h, :, ls] = c_dense
        return carry

    lax.fori_loop(0, PEER_HEADS, head_body, 0)


def _route(h2, wq_hi, wq_lo, keys_hi, keys_lo, layer):
    n = h2.shape[0]
    tn = ROUTE_TN
    dense = pl.BlockSpec((PEER_HEADS, N_KEYS, tn), lambda i: (0, 0, i))
    wspec = pl.BlockSpec((None, D_MODEL, PEER_HEADS * 2 * N_KEYS), lambda i: (layer, 0, 0))
    kspec = pl.BlockSpec((None, PEER_HEADS, 2, N_KEYS, N_KEYS), lambda i: (layer, 0, 0, 0, 0))
    return pl.pallas_call(
        _route_kernel,
        grid=(n // tn,),
        in_specs=[pl.BlockSpec((tn, D_MODEL), lambda i: (i, 0)), wspec, wspec, kspec, kspec],
        out_specs=[dense] * 4,
        out_shape=[jax.ShapeDtypeStruct((PEER_HEADS, N_KEYS, n), dt) for dt in (F32, BF16, BF16, F32)],
        scratch_shapes=[pltpu.VMEM((2 * PEER_HEADS, tn, N_KEYS), BF16),
                        pltpu.VMEM((2 * PEER_HEADS, tn, N_KEYS), BF16),
                        pltpu.VMEM((2, N_KEYS, tn), F32),
                        pltpu.VMEM((2, tn // ROUTE_LG, PEER_TOPK, ROUTE_LG), F32),
                        pltpu.VMEM((2, tn // ROUTE_LG, N_KEYS, ROUTE_LG), F32),
                        pltpu.VMEM((tn // ROUTE_LG, 8 * len(_CAND_PIECES), ROUTE_LG), F32)],
        compiler_params=_cparams(("parallel",)),
        name="peer_route",
    )(h2, wq_hi, wq_lo, keys_hi, keys_lo)


PEER_TN = 512
PEER_TE = 1024


def _gelu_bf16(t):
    x = t.astype(BF16)
    y2 = x * (1.0 + 0.044715 * (x * x)) * (-2.0 * 0.7978845608028654)
    return x / (1.0 + jnp.exp(y2))


BF16_ROWS = 16


def _row_to_packed(row):
    tile = jnp.broadcast_to(row, (BF16_ROWS, row.shape[1])).astype(BF16)
    return jnp.concatenate([tile] * (N_KEYS // BF16_ROWS), axis=0)


def _expert_kernel(hb_ref, u_ref, vt_ref, a_ref, b_ref, r2_ref, c_ref, x_ref, mod_ref, o_ref,
                   acc_ref):
    e = pl.program_id(1)
    rows_per_tile = PEER_TE // N_KEYS

    @pl.when(e == 0)
    def _():
        acc_ref[...] = jnp.zeros_like(acc_ref)

    t_t = lax.dot_general(u_ref[...], hb_ref[...], (((1,), (1,)), ((), ())),
                          preferred_element_type=F32)
    w_parts = []
    for j in range(rows_per_tile):
        e1 = e * rows_per_tile + j
        g = None
        for h in range(PEER_HEADS):
            a_row = _row_to_packed(a_ref[h, pl.ds(e1, 1), :])
            c_row = _row_to_packed(c_ref[h, pl.ds(e1, 1), :])
            term = a_row * jnp.where(r2_ref[h] < c_row, b_ref[h], 0.0)
            g = term if g is None else g + term
        w_parts.append(g * _gelu_bf16(t_t[j * N_KEYS:(j + 1) * N_KEYS, :]))
    w_t = jnp.concatenate(w_parts, axis=0)
    acc_ref[...] += jnp.dot(vt_ref[...], w_t, preferred_element_type=F32)

    @pl.when(e == pl.num_programs(1) - 1)
    def _():
        g2 = mod_ref[0, :, 5 * D_MODEL:6 * D_MODEL]
        o_ref[...] = x_ref[...] + g2 * acc_ref[...].T


def _experts(h2b, u_b, vt_b, dense, x, mod_l, seq_len, layer):
    n = x.shape[0]
    tn, te = PEER_TN, PEER_TE
    rows = mod_l.shape[0]
    mod_map = (lambda i, e: (0, 0, 0)) if rows == 1 else (lambda i, e: ((i * tn) // seq_len, 0, 0))
    dspec = pl.BlockSpec((PEER_HEADS, N_KEYS, tn), lambda i, e: (0, 0, i))
    return pl.pallas_call(
        _expert_kernel,
        grid=(n // tn, N_EXPERTS // te),
        in_specs=[pl.BlockSpec((tn, D_MODEL), lambda i, e: (i, 0)),
                  pl.BlockSpec((None, te, D_MODEL), lambda i, e: (layer, e, 0)),
                  pl.BlockSpec((None, D_MODEL, te), lambda i, e: (layer, 0, e)),
                  dspec, dspec, dspec, dspec,
                  pl.BlockSpec((tn, D_MODEL), lambda i, e: (i, 0)),
                  pl.BlockSpec((1, 1, 6 * D_MODEL), mod_map)],
        out_specs=pl.BlockSpec((tn, D_MODEL), lambda i, e: (i, 0)),
        out_shape=jax.ShapeDtypeStruct((n, D_MODEL), F32),
        scratch_shapes=[pltpu.VMEM((D_MODEL, tn), F32)],
        compiler_params=_cparams(("parallel", "arbitrary")),
        name="peer_experts",
    )(h2b, u_b, vt_b, *dense, x, mod_l)


def _final_norm_kernel(x_ref, g_ref, o_ref):
    x = x_ref[...]
    o_ref[...] = x * lax.rsqrt(jnp.mean(x * x, axis=-1, keepdims=True) + EPS) * g_ref[...]


def _final_norm(x, g):
    n = x.shape[0]
    tn = 512
    return pl.pallas_call(
        _final_norm_kernel,
        grid=(n // tn,),
        in_specs=[pl.BlockSpec((tn, D_MODEL), lambda i: (i, 0)),
                  pl.BlockSpec((1, D_MODEL), lambda i: (0, 0))],
        out_specs=pl.BlockSpec((tn, D_MODEL), lambda i: (i, 0)),
        out_shape=jax.ShapeDtypeStruct((n, D_MODEL), F32),
        compiler_params=_cparams(("parallel",)),
        name="final_norm",
    )(x, g)


def _pack_columns(a, src, width):
    parts = [a[..., lo:hi] for lo, hi in src]
    used = sum(hi - lo for lo, hi in src)
    if used < width:
        parts.append(jnp.zeros(a.shape[:-1] + (width - used,), a.dtype))
    return jnp.concatenate(parts, axis=-1)


def _position_code(rows):
    quarter = D_MODEL // 4
    omega = 1.0 / (POS_BASE ** (jnp.arange(quarter, dtype=F32) / quarter))
    r, col = jnp.meshgrid(jnp.arange(rows, dtype=F32), jnp.arange(GRID_W, dtype=F32), indexing='ij')

    def enc(pos):
        ang = pos.reshape(-1, 1) * omega
        return jnp.concatenate([jnp.sin(ang), jnp.cos(ang)], axis=-1)
    return jnp.concatenate([enc(r), enc(col)], axis=-1)


def _layer(x, bsz, seq, mod_l, lp, state, emit_state):
    layer = lp['layer']
    pw = _inproj(x, mod_l, lp['norm1_g'], lp['w_wide'], lp['b_wide'], seq, layer, 2432, BF16,
                 "in_projection_wide")
    pf = _inproj(x, mod_l, lp['norm1_g'], lp['w_fine'], lp['b_fine'], seq, layer, N_FINE, F32,
                 "in_projection_fine")
    if state is None:
        st_a = st_b = st_c = st_d = None
    else:
        c0, n0, m0, sb0, sc0, hd0 = state
        st_a = (c0, n0, m0)
        st_b = jnp.swapaxes(sb0, -1, -2)
        st_c = jnp.swapaxes(sc0, -1, -2)
        st_d = hd0
    a_f, a_b, new_a = _mlstm(pw, pf, bsz, seq, st_a, emit_state)
    b_f, b_b, new_b = _hgrn((pw, pf), bsz, seq, lp['hgrn_lb'], st_b, emit_state)
    c_f, c_b, new_c = _gla((pw, pf), bsz, seq, lp['gla_up_pad'], lp['gla_up_b'], st_c, emit_state)
    yd, new_d = _lru(pw, pf, bsz, seq, lp['conv_w'], lp['conv_b'], lp['lru_w_a'], lp['lru_b_a'],
                     lp['lru_w_x'], lp['lru_b_x'], lp['lru_lambda'], st_d, emit_state)
    x1, h2, h2b = _merge((a_f, a_b, b_f, b_b, c_f, c_b), yd, pw, x, mod_l, lp['norm2_g'],
                         lp['w_branch'], lp['w_out'], seq, layer)
    dense = _route(h2, lp['wq_hi'], lp['wq_lo'], lp['keys_hi'], lp['keys_lo'], layer)
    x2 = _experts(h2b, lp['peer_u'], lp['peer_vt'], dense, x1, mod_l, seq, layer)
    new_state = None
    if emit_state:
        new_state = (*new_a, jnp.swapaxes(new_b, -1, -2), jnp.swapaxes(new_c, -1, -2), new_d)
    return x2, new_state


def kernel(x_prompt, x_sample, state_mlstm_C, state_mlstm_n, state_mlstm_m, state_hgrn_S,
           state_gla_S, state_lru_h, c, c_ctx, norm1_g, norm2_g, final_norm_g, w_mod, b_mod,
           w_in, b_in, w_gla_up, b_gla_up, hgrn_lower_bounds, conv_w, conv_b, lru_w_a, lru_b_a,
           lru_w_x, lru_b_x, lru_lambda, w_branch, w_out, peer_w_q, peer_sub_keys, peer_u, peer_v):
    lb_soft = jax.nn.softmax(hgrn_lower_bounds.astype(F32), axis=0)
    lb_all = jnp.cumsum(lb_soft, axis=0) - lb_soft[0:1]
    w_wide = _pack_columns(w_in, _WIDE_SRC, N_WIDE).astype(BF16)
    w_fine = _pack_columns(w_in, _FINE_SRC, N_FINE).astype(BF16)
    b_wide = _pack_columns(b_in, _WIDE_SRC, N_WIDE).reshape(DEPTH, 1, N_WIDE)
    b_fine = _pack_columns(b_in, _FINE_SRC, N_FINE).reshape(DEPTH, 1, N_FINE)
    kw = N_HEAD * DK_C
    up_pad = jnp.zeros((DEPTH, 2, SMALL_W, kw), F32)
    for d in range(2):
        lo = 2 * N_HEAD * 2 + d * R_C
        up_pad = up_pad.at[:, d, lo:lo + R_C, :].set(w_gla_up[:, d].astype(F32))
    wq_hi, wq_lo = _split(peer_w_q)
    keys_hi, keys_lo = _split(peer_sub_keys)
    u_b = peer_u.astype(BF16)
    vt_b = jnp.swapaxes(peer_v, 1, 2).astype(BF16)
    wbr_b = w_branch.astype(BF16)
    wout_b = w_out.astype(BF16)

    def layer_params(l):
        return {
            'norm1_g': norm1_g[l].reshape(1, D_MODEL), 'norm2_g': norm2_g[l].reshape(1, D_MODEL),
            'layer': l, 'w_wide': w_wide, 'b_wide': b_wide[l], 'w_fine': w_fine, 'b_fine': b_fine[l],
            'hgrn_lb': lb_all[l],
            'gla_up_pad': up_pad[l], 'gla_up_b': b_gla_up[l],
            'conv_w': conv_w[l], 'conv_b': conv_b[l], 'lru_w_a': lru_w_a[l], 'lru_b_a': lru_b_a[l],
            'lru_w_x': lru_w_x[l], 'lru_b_x': lru_b_x[l], 'lru_lambda': lru_lambda[l],
            'w_branch': wbr_b, 'w_out': wout_b, 'wq_hi': wq_hi, 'wq_lo': wq_lo,
            'keys_hi': keys_hi, 'keys_lo': keys_lo, 'peer_u': u_b, 'peer_vt': vt_b,
        }

    cond = jnp.concatenate([c, c_ctx[None, :]], axis=0).astype(F32)
    n_dec = c.shape[0]
    mod = _modulation(cond, w_mod, b_mod)
    final_g = final_norm_g.reshape(1, D_MODEL)

    bp, tp, _ = x_prompt.shape
    x = x_prompt.reshape(bp * tp, D_MODEL)
    ctx_states = []
    for l in range(DEPTH):
        mod_l = mod[l, n_dec:n_dec + 1].reshape(1, 1, 6 * D_MODEL)
        x, st = _layer(x, bp, tp, mod_l, layer_params(l), None, True)
        ctx_states.append(st)
    y_prompt = _final_norm(x, final_g).reshape(bp, tp, D_MODEL)
    new_states = tuple(jnp.stack([s[i] for s in ctx_states], axis=1) for i in range(6))

    bd, td, _ = x_sample.shape
    x = _add_position(x_sample, _position_code(td // GRID_W)).reshape(bd * td, D_MODEL)
    for l in range(DEPTH):
        cached = (state_mlstm_C[:, l].astype(F32), state_mlstm_n[:, l].astype(F32),
                  state_mlstm_m[:, l].astype(F32), state_hgrn_S[:, l].astype(F32),
                  state_gla_S[:, l].astype(F32), state_lru_h[:, l].astype(F32))
        mod_l = mod[l, 0:n_dec].reshape(n_dec, 1, 6 * D_MODEL)
        x, _ = _layer(x, bd, td, mod_l, layer_params(l), cached, False)
    y_sample = _final_norm(x, final_g).reshape(bd, td, D_MODEL)
    return (y_prompt, y_sample) + new_states
```

```python
import functools

import jax
import jax.numpy as jnp
from jax import lax
from jax.experimental import pallas as pl
from jax.experimental.pallas import tpu as pltpu

F32 = jnp.float32
BF16 = jnp.bfloat16
HIGHEST = lax.Precision.HIGHEST

D_MODEL = 1024
DEPTH = 4
GRID_W = 64
EPS = 1e-6
NEG_BIG = -1e30
MAX_EXP_ARG = 80.0
MIX_W = 512
N_HEAD = 4
DH = 128
DK_C = 64
R_C = 16
GLA_TAU = 16.0
CONV_W = 4
LRU_C = 8.0
N_BRANCH = 4
N_KEYS = 128
N_EXPERTS = N_KEYS * N_KEYS
PEER_HEADS = 8
PEER_TOPK = 16
POS_BASE = 10000.0

LANE = 128
SUBLANES = 8
VMEM_LIMIT = 56 * 1024 * 1024

COL_GATES = 0
COL_AQ, COL_AK, COL_AV, COL_AO = 4096, 4608, 5120, 5632
COL_BQ, COL_BI, COL_BG = 6144, 6656, 7168
COL_CQ, COL_CK, COL_CV, COL_CG = 7680, 7936, 8192, 8704
COL_DG = 9216
N_WIDE = 9728
COL_BF, COL_DX = 0, 1024
COL_SMALL = 1536
SMALL_W = 256
N_FINE = COL_SMALL + SMALL_W

_WIDE_SRC = ((7216, 11312), (0, 2048), (2064, 2576), (3600, 4624), (4624, 6160), (6704, 7216))
_FINE_SRC = ((2576, 3600), (6192, 6704), (2048, 2064), (6160, 6192))

CHUNK_A = 128
CHUNK_G = 64
SUB_G = 16
CHUNK_D = 128
LRU_HEADS = 2


def _cparams(sem):
    return pltpu.CompilerParams(dimension_semantics=sem, vmem_limit_bytes=VMEM_LIMIT)


def _bdot(a, b):
    return jnp.dot(a.astype(BF16), b.astype(BF16), preferred_element_type=F32)


def _bdot_nt(a, b):
    return lax.dot_general(a.astype(BF16), b.astype(BF16), (((1,), (1,)), ((), ())),
                           preferred_element_type=F32)


def _bdot_tn(a, b):
    return lax.dot_general(a.astype(BF16), b.astype(BF16), (((0,), (0,)), ((), ())),
                           preferred_element_type=F32)


def _split(a):
    hi = a.astype(BF16)
    lo = (a - hi.astype(F32)).astype(BF16)
    return hi, lo


def _dot3(a, b_hi, b_lo):
    a_hi, a_lo = _split(a)
    return (jnp.dot(a_hi, b_hi, preferred_element_type=F32)
            + jnp.dot(a_lo, b_hi, preferred_element_type=F32)
            + jnp.dot(a_hi, b_lo, preferred_element_type=F32))


def _log_sigmoid(z):
    return jnp.minimum(z, 0.0) - jnp.log1p(jnp.exp(-jnp.abs(z)))


def _sigmoid(z):
    return 0.5 * jnp.tanh(0.5 * z) + 0.5


def _gelu(x):
    return 0.5 * x * (1.0 + jnp.tanh(0.7978845608028654 * (x + 0.044715 * x * x * x)))


def _silu(x):
    return x * _sigmoid(x)


def _expm1(y):
    u = jnp.exp(y)
    near = (u - 1.0) * y / jnp.log(u)
    return jnp.where(u == 1.0, y, jnp.where(u < 0.5, u - 1.0, near))


def _mod_kernel(c_ref, w_ref, b_ref, o_ref):
    a = _silu(c_ref[...])
    o_ref[0] = jnp.dot(a, w_ref[0], precision=HIGHEST, preferred_element_type=F32) + b_ref[0]


def _modulation(cond, w_mod, b_mod):
    r = cond.shape[0]
    tc = 1536
    return pl.pallas_call(
        _mod_kernel,
        grid=(DEPTH, 6 * D_MODEL // tc),
        in_specs=[pl.BlockSpec((r, D_MODEL), lambda l, j: (0, 0)),
                  pl.BlockSpec((1, D_MODEL, tc), lambda l, j: (l, 0, j)),
                  pl.BlockSpec((1, 1, tc), lambda l, j: (l, 0, j))],
        out_specs=pl.BlockSpec((1, r, tc), lambda l, j: (l, 0, j)),
        out_shape=jax.ShapeDtypeStruct((DEPTH, r, 6 * D_MODEL), F32),
        compiler_params=_cparams(("parallel", "parallel")),
        name="modulation",
    )(cond, w_mod, b_mod.reshape(DEPTH, 1, 6 * D_MODEL))


def _addpos_kernel(x_ref, p_ref, o_ref):
    o_ref[0] = x_ref[0] + p_ref[...]


def _add_position(x, pos):
    b, t, d = x.shape
    tt = 512
    return pl.pallas_call(
        _addpos_kernel,
        grid=(b, t // tt),
        in_specs=[pl.BlockSpec((1, tt, d), lambda i, j: (i, j, 0)),
                  pl.BlockSpec((tt, d), lambda i, j: (j, 0))],
        out_specs=pl.BlockSpec((1, tt, d), lambda i, j: (i, j, 0)),
        out_shape=jax.ShapeDtypeStruct(x.shape, F32),
        compiler_params=_cparams(("parallel", "parallel")),
        name="add_position",
    )(x, pos)


def _inproj_kernel(x_ref, mod_ref, g_ref, w_ref, b_ref, o_ref, h_ref):
    @pl.when(pl.program_id(1) == 0)
    def _():
        x = x_ref[...]
        inv = lax.rsqrt(jnp.mean(x * x, axis=-1, keepdims=True) + EPS)
        sh = mod_ref[0, :, 0:D_MODEL]
        sc = mod_ref[0, :, D_MODEL:2 * D_MODEL]
        h_ref[...] = (x * inv * g_ref[...] * (1.0 + sc) + sh).astype(BF16)

    acc = jnp.dot(h_ref[...], w_ref[...], preferred_element_type=F32) + b_ref[...]
    o_ref[...] = acc.astype(o_ref.dtype)


def _inproj(x, mod_l, norm_g, w_pack, b_pack, seq_len, layer, tc, out_dtype, name):
    n = x.shape[0]
    width = w_pack.shape[-1]
    tn = 1024
    rows = mod_l.shape[0]
    if rows == 1:
        mod_map = lambda i, j: (0, 0, 0)
    else:
        mod_map = lambda i, j: ((i * tn) // seq_len, 0, 0)
    return pl.pallas_call(
        _inproj_kernel,
        grid=(n // tn, width // tc),
        in_specs=[pl.BlockSpec((tn, D_MODEL), lambda i, j: (i, 0)),
                  pl.BlockSpec((1, 1, 6 * D_MODEL), mod_map),
                  pl.BlockSpec((1, D_MODEL), lambda i, j: (0, 0)),
                  pl.BlockSpec((None, D_MODEL, tc), lambda i, j: (layer, 0, j)),
                  pl.BlockSpec((1, tc), lambda i, j: (0, j))],
        out_specs=pl.BlockSpec((tn, tc), lambda i, j: (i, j)),
        out_shape=jax.ShapeDtypeStruct((n, width), out_dtype),
        scratch_shapes=[pltpu.VMEM((tn, D_MODEL), BF16)],
        compiler_params=_cparams(("parallel", "arbitrary")),
        name=name,
    )(x, mod_l, norm_g, w_pack, b_pack)


SCAN_GROUP = 4


def _lockstep(gens):
    alive = list(gens)
    while alive:
        still = []
        for gen in alive:
            try:
                next(gen)
                still.append(gen)
            except StopIteration:
                pass
        alive = still
        if alive:
            yield


def _per_sequence(body, n_group, shared):
    def kern(*refs):
        def run(phase):
            stages = [body(*[r if i in shared else r.at[g] for i, r in enumerate(refs)], phase=phase)
                      for g in range(n_group)]
            for _ in _lockstep([s for s in stages if s is not None]):
                pass

        pl.when(pl.program_id(1) == 0)(lambda: run('init'))
        run('main')
        pl.when(pl.program_id(1) == pl.num_programs(1) - 1)(lambda: run('emit'))
    return kern


def _cumsum_rows(x, rev):
    k = 1
    while k < x.shape[0]:
        x = x + _shift_rows(x, k, rev, 0.0)
        k *= 2
    return x


def _causal_mask(length, rev):
    row = lax.broadcasted_iota(jnp.int32, (length, length), 0)
    col = lax.broadcasted_iota(jnp.int32, (length, length), 1)
    return (col >= row) if rev else (col <= row)


def _mlstm_kernel(*refs, has_init, emit_state, phase):
    L = CHUNK_A
    ins = list(refs)
    dirs = [ins[0:4], ins[4:8]]
    pos = 8
    if has_init:
        c0_ref, n0_ref, m0_ref = ins[pos:pos + 3]
        pos += 3
    outs = ins[pos:pos + 2]
    pos += 2
    if emit_state:
        cout_ref, nout_ref, mout_ref = ins[pos:pos + 3]
        pos += 3
    cs_ref, ns_ref, ms_ref = ins[pos:pos + 3]

    if phase == 'init':
        if has_init:
            cs_ref[...] = c0_ref[...]
            ns_ref[...] = n0_ref[...]
            ms_ref[...] = m0_ref[...]
        else:
            cs_ref[...] = jnp.zeros_like(cs_ref)
            ns_ref[...] = jnp.zeros_like(ns_ref)
            ms_ref[...] = jnp.zeros_like(ms_ref)
        return
    if phase == 'emit':
        if emit_state:
            cout_ref[...] = cs_ref[...]
            nout_ref[...] = ns_ref[...]
            mout_ref[...] = ms_ref[...]
        return

    def direction(d):
        rev = d == 1
        q_ref, k_ref, v_ref, s_ref = dirs[d]
        o_ref = outs[d]
        mask = _causal_mask(L, rev)
        sm = s_ref[:, 0:LANE]
        lf = _log_sigmoid(sm)
        bc = jnp.dot(mask.astype(F32), lf, precision=HIGHEST,
                     preferred_element_type=F32)
        sm_t = sm.T
        bc_t = bc.T
        last = 0 if rev else L - 1
        heads = range(N_HEAD)
        hsl = [slice(h * DH, (h + 1) * DH) for h in heads]
        yield
        li_c, b_c, logw, m_prev, m_t, w_state = [], [], [], [], [], []
        for h in heads:
            ci, cf = d * N_HEAD + h, 2 * N_HEAD + d * N_HEAD + h
            li_c.append(sm[:, ci:ci + 1])
            b_c.append(bc[:, cf:cf + 1])
            li_r, b_r = sm_t[ci:ci + 1, :], bc_t[cf:cf + 1, :]
            m_prev.append(ms_ref[d, h][:, 0:1])
            logw.append(jnp.where(mask, b_c[h] - b_r + li_r, NEG_BIG))
            from_state = b_c[h] + m_prev[h]
            m_t.append(jnp.maximum(from_state, jnp.max(logw[h], axis=-1, keepdims=True)))
            w_state.append(jnp.exp(from_state - m_t[h]))
        q = [q_ref[:, hsl[h]].astype(F32) for h in heads]
        k = [k_ref[:, hsl[h]].astype(F32) * (DH ** -0.5) for h in heads]
        v = [v_ref[:, hsl[h]] for h in heads]
        c_st = [cs_ref[d, h] for h in heads]
        n_st = [ns_ref[d, h] for h in heads]
        qk = [_bdot_nt(q[h], k[h]) for h in heads]
        qc = [_bdot(q[h], c_st[h]) for h in heads]
        yield
        for h in heads:
            scores = qk[h] * jnp.exp(logw[h] - m_t[h])
            num = w_state[h] * qc[h] + _bdot(scores, v[h])
            den = (w_state[h] * jnp.sum(q[h] * n_st[h], axis=-1, keepdims=True)
                   + jnp.sum(scores, axis=-1, keepdims=True))
            floor = jnp.exp(jnp.minimum(-m_t[h], MAX_EXP_ARG))
            o_ref[:, hsl[h]] = num / jnp.maximum(jnp.abs(den), floor)
        yield
        for h in heads:
            m_new = m_t[h][last:last + 1, :]
            b_last = b_c[h][last:last + 1, :]
            kw = k[h] * jnp.exp(b_last - b_c[h] + li_c[h] - m_new)
            decay = jnp.exp(b_last + m_prev[h] - m_new)
            cs_ref[d, h] = decay * c_st[h] + _bdot_tn(kw, v[h])
            ns_ref[d, h] = decay * n_st[h] + jnp.sum(kw, axis=0, keepdims=True)
            ms_ref[d, h] = jnp.broadcast_to(m_new, (1, LANE))

    yield from _lockstep([direction(0), direction(1)])


def _scan_maps(nc):
    fwd = lambda cb: (lambda b, c: (b, c, cb))
    bwd = lambda cb: (lambda b, c: (b, nc - 1 - c, cb))
    return fwd, bwd


def _mlstm(pw, pf, bsz, seq, state, emit_state):
    L = CHUNK_A
    nc = seq // L
    n = bsz * seq
    g = min(bsz, SCAN_GROUP)
    has_init = state is not None
    pw3 = pw.reshape(bsz, seq, N_WIDE)
    pf3 = pf.reshape(bsz, seq, N_FINE)
    fwd, bwd = _scan_maps(nc)

    in_specs, args = [], []
    for mk in (fwd, bwd):
        for col in (COL_AQ, COL_AK, COL_AV):
            in_specs.append(pl.BlockSpec((g, L, MIX_W), mk(col // MIX_W)))
            args.append(pw3)
        in_specs.append(pl.BlockSpec((g, L, SMALL_W), mk(COL_SMALL // SMALL_W)))
        args.append(pf3)
    st_specs = [pl.BlockSpec((g, 2, N_HEAD, DH, DH), lambda b, c: (b, 0, 0, 0, 0)),
                pl.BlockSpec((g, 2, N_HEAD, 1, DH), lambda b, c: (b, 0, 0, 0, 0)),
                pl.BlockSpec((g, 2, N_HEAD, 1, LANE), lambda b, c: (b, 0, 0, 0, 0))]
    st_shapes = [jax.ShapeDtypeStruct((bsz, 2, N_HEAD, DH, DH), F32),
                 jax.ShapeDtypeStruct((bsz, 2, N_HEAD, 1, DH), F32),
                 jax.ShapeDtypeStruct((bsz, 2, N_HEAD, 1, LANE), F32)]
    if has_init:
        c0, n0, m0 = state
        in_specs += st_specs
        args += [c0, n0.reshape(bsz, 2, N_HEAD, 1, DH),
                 jnp.broadcast_to(m0[..., None, None], (bsz, 2, N_HEAD, 1, LANE))]
    out_specs = [pl.BlockSpec((g, L, MIX_W), fwd(0)), pl.BlockSpec((g, L, MIX_W), bwd(0))]
    out_shape = [jax.ShapeDtypeStruct((bsz, seq, MIX_W), F32)] * 2
    if emit_state:
        out_specs += st_specs
        out_shape += st_shapes
    body = functools.partial(_mlstm_kernel, has_init=has_init, emit_state=emit_state)
    res = pl.pallas_call(
        _per_sequence(body, g, ()),
        grid=(bsz // g, nc),
        in_specs=in_specs,
        out_specs=out_specs,
        out_shape=out_shape,
        scratch_shapes=[pltpu.VMEM((g, 2, N_HEAD, DH, DH), F32),
                        pltpu.VMEM((g, 2, N_HEAD, 1, DH), F32),
                        pltpu.VMEM((g, 2, N_HEAD, 1, LANE), F32)],
        compiler_params=_cparams(("parallel", "arbitrary")),
        name="mlstm_scan",
    )(*args)
    o_f, o_b = res[0].reshape(n, MIX_W), res[1].reshape(n, MIX_W)
    new_state = None
    if emit_state:
        new_state = (res[2], res[3].reshape(bsz, 2, N_HEAD, DH), res[4][:, :, :, 0, 0])
    return o_f, o_b, new_state


def _gla_direction(q, k, la, v, s_ref, d, o_ref, *, rev, dk):
    L, S = CHUNK_G, SUB_G
    nsub = L // S
    mask = _causal_mask(L, rev)
    b = _cumsum_rows(la, rev)
    bx = b - la
    ref_rows = [bx[i * S + (S - 1 if rev else 0):i * S + (S - 1 if rev else 0) + 1, :]
                for i in range(nsub)]
    bref = jnp.concatenate([jnp.broadcast_to(r, (S, r.shape[1])) for r in ref_rows], axis=0)
    q_hat = q * jnp.exp(b - bref)
    rowid = lax.broadcasted_iota(jnp.int32, (L, 1), 0)
    k_hat = []
    for i in range(nsub):
        seen = (rowid >= i * S) if rev else (rowid < (i + 1) * S)
        k_hat.append(k * jnp.exp(jnp.where(seen, ref_rows[i] - b, NEG_BIG)))
    last = 0 if rev else L - 1
    b_last = b[last:last + 1, :]
    q_state = q * jnp.exp(b)
    k_state = k * jnp.exp(b_last - b)
    e_last = jnp.exp(b_last)
    yield
    ksl = [slice(h * dk, (h + 1) * dk) for h in range(N_HEAD)]
    vsl = [slice(h * DH, (h + 1) * DH) for h in range(N_HEAD)]
    blocks = [[_bdot_nt(q_hat[i * S:(i + 1) * S, ksl[h]], k_hat[i][:, ksl[h]]) for i in range(nsub)]
              for h in range(N_HEAD)]
    states = [s_ref[d, h] for h in range(N_HEAD)]
    inter = [_bdot_nt(q_state[:, ksl[h]], states[h]) for h in range(N_HEAD)]
    grown = [_bdot_tn(v[:, vsl[h]], k_state[:, ksl[h]]) for h in range(N_HEAD)]
    yield
    for h in range(N_HEAD):
        scores = jnp.where(mask, jnp.concatenate(blocks[h], axis=0), 0.0)
        o_ref[:, vsl[h]] = inter[h] + _bdot(scores, v[:, vsl[h]])
    yield
    for h in range(N_HEAD):
        s_ref[d, h] = states[h] * e_last[:, ksl[h]] + grown[h]


def _scan_state_io(ins, pos, has_init, emit_state, n_out):
    s0_ref = sout_ref = None
    if has_init:
        s0_ref = ins[pos]
        pos += 1
    outs = ins[pos:pos + n_out]
    pos += n_out
    if emit_state:
        sout_ref = ins[pos]
        pos += 1
    return s0_ref, outs, sout_ref, ins[pos]


def _scan_state_phase(phase, s_ref, s0_ref, sout_ref):
    if phase == 'init':
        if s0_ref is not None:
            s_ref[...] = s0_ref[...]
        else:
            s_ref[...] = jnp.zeros_like(s_ref)
    elif phase == 'emit' and sout_ref is not None:
        sout_ref[...] = s_ref[...]
    return phase != 'main'


def _hgrn_kernel(*refs, has_init, emit_state, phase):
    ins = list(refs)
    lb_ref = ins[6]
    s0_ref, outs, sout_ref, s_ref = _scan_state_io(ins, 7, has_init, emit_state, 2)
    if _scan_state_phase(phase, s_ref, s0_ref, sout_ref):
        return
    chains = []
    for d in range(2):
        q_ref, z_ref, v_ref = ins[3 * d:3 * d + 3]
        z = z_ref[...]
        lb = lb_ref[d:d + 1, :]
        la = _log_sigmoid(z) + jnp.log1p(lb * jnp.exp(jnp.minimum(-z, MAX_EXP_ARG)))
        k = (1.0 - lb) * _sigmoid(-z)
        q = q_ref[...].astype(F32) * (DH ** -0.5)
        chains.append(_gla_direction(q, k, la, v_ref[...], s_ref, d, outs[d], rev=d == 1, dk=DH))
    yield from _lockstep(chains)


def _gla_kernel(*refs, has_init, emit_state, phase):
    ins = list(refs)
    up_ref, upb_ref = ins[8], ins[9]
    s0_ref, outs, sout_ref, s_ref = _scan_state_io(ins, 10, has_init, emit_state, 2)
    if _scan_state_phase(phase, s_ref, s0_ref, sout_ref):
        return
    chains = []
    for d in range(2):
        q_ref, k_ref, v_ref, sm_ref = ins[4 * d:4 * d + 4]
        zg = jnp.dot(sm_ref[...], up_ref[d], precision=HIGHEST,
                     preferred_element_type=F32) + upb_ref[d:d + 1, :]
        la = _log_sigmoid(zg) * (1.0 / GLA_TAU)
        q = q_ref[...].astype(F32) * (DK_C ** -0.5)
        k = k_ref[...].astype(F32)
        chains.append(_gla_direction(q, k, la, v_ref[...], s_ref, d, outs[d], rev=d == 1,
                                     dk=DK_C))
    yield from _lockstep(chains)


def _gated_scan(kernel, p, bsz, seq, cols, extra, extra_specs, dk, state_t, emit_state, name):
    L = CHUNK_G
    nc = seq // L
    n = bsz * seq
    g = min(bsz, SCAN_GROUP)
    has_init = state_t is not None
    src = {'wide': p[0].reshape(bsz, seq, N_WIDE), 'fine': p[1].reshape(bsz, seq, N_FINE)}
    fwd, bwd = _scan_maps(nc)

    in_specs, args = [], []
    for d, mk in enumerate((fwd, bwd)):
        for which, off, width in cols[d]:
            in_specs.append(pl.BlockSpec((g, L, width), mk(off // width)))
            args.append(src[which])
    shared = tuple(range(len(args), len(args) + len(extra)))
    in_specs += extra_specs
    args += extra
    st_spec = pl.BlockSpec((g, 2, N_HEAD, DH, dk), lambda b, c: (b, 0, 0, 0, 0))
    if has_init:
        in_specs.append(st_spec)
        args.append(state_t)
    out_specs = [pl.BlockSpec((g, L, MIX_W), fwd(0)), pl.BlockSpec((g, L, MIX_W), bwd(0))]
    out_shape = [jax.ShapeDtypeStruct((bsz, seq, MIX_W), F32)] * 2
    if emit_state:
        out_specs.append(st_spec)
        out_shape.append(jax.ShapeDtypeStruct((bsz, 2, N_HEAD, DH, dk), F32))
    body = functools.partial(kernel, has_init=has_init, emit_state=emit_state)
    res = pl.pallas_call(
        _per_sequence(body, g, shared),
        grid=(bsz // g, nc),
        in_specs=in_specs,
        out_specs=out_specs,
        out_shape=out_shape,
        scratch_shapes=[pltpu.VMEM((g, 2, N_HEAD, DH, dk), F32)],
        compiler_params=_cparams(("parallel", "arbitrary")),
        name=name,
    )(*args)
    return res[0].reshape(n, MIX_W), res[1].reshape(n, MIX_W), (res[2] if emit_state else None)


def _hgrn(p, bsz, seq, lb, state_t, emit_state):
    cols = [[('wide', COL_BQ, MIX_W), ('fine', COL_BF + d * MIX_W, MIX_W), ('wide', COL_BI, MIX_W)]
            for d in range(2)]
    return _gated_scan(_hgrn_kernel, p, bsz, seq, cols, [lb],
                       [pl.BlockSpec((2, MIX_W), lambda b, c: (0, 0))], DH, state_t, emit_state,
                       "hgrn2_scan")


def _gla(p, bsz, seq, up_pad, up_b, state_t, emit_state):
    kw = N_HEAD * DK_C
    cols = [[('wide', COL_CQ, kw), ('wide', COL_CK, kw), ('wide', COL_CV, MIX_W),
             ('fine', COL_SMALL, SMALL_W)] for _ in range(2)]
    return _gated_scan(_gla_kernel, p, bsz, seq, cols, [up_pad, up_b],
                       [pl.BlockSpec((2, SMALL_W, kw), lambda b, c: (0, 0, 0)),
                        pl.BlockSpec((2, kw), lambda b, c: (0, 0))], DK_C, state_t, emit_state,
                       "gla_scan")


def _shift_rows(x, k, rev, fill):
    n = x.shape[0]
    rowid = lax.broadcasted_iota(jnp.int32, x.shape, 0)
    if rev:
        return jnp.where(rowid >= n - k, fill, pltpu.roll(x, n - k, 0))
    return jnp.where(rowid < k, fill, pltpu.roll(x, k, 0))


def _lru_kernel(*refs, seq, has_init, emit_state):
    L = CHUNK_D
    nc = seq // L
    ins = list(refs)
    (dx_ref, dg_ref, cw_ref, cb_ref, wah_ref, wal_ref, ba_ref, wxh_ref, wxl_ref, bx_ref,
     lam_ref) = ins[0:11]
    pos = 11
    h0_ref = hout_ref = None
    if has_init:
        h0_ref = ins[pos]
        pos += 1
    y_ref = ins[pos]
    pos += 1
    if emit_state:
        hout_ref = ins[pos]
        pos += 1
    pad_ref, hb_ref, xd_ref, xh_ref, xl_ref = ins[pos:pos + 5]
    width = LRU_HEADS * LANE

    zeros8 = jnp.zeros((8, width), F32)
    pad_ref[0:8, :] = zeros8
    pad_ref[8:8 + seq, :] = dx_ref[...]
    pad_ref[8 + seq:16 + seq, :] = zeros8
    lam = lam_ref[...]
    sp = jnp.maximum(-lam, 0.0) + jnp.log1p(jnp.exp(-jnp.abs(lam)))
    cw = cw_ref[...]
    cb = cb_ref[...]

    def conv_body(c, carry):
        start = pl.multiple_of(c * L, L)
        win = pad_ref[pl.ds(start, L + 16), :]
        xd = cb + sum(cw[j:j + 1, :] * win[7 + j:7 + j + L, :] for j in range(CONV_W))
        x_hi, x_lo = _split(xd)
        xd_ref[pl.ds(start, L), :] = xd
        xh_ref[pl.ds(start, L), :] = x_hi
        xl_ref[pl.ds(start, L), :] = x_lo
        return carry

    lax.fori_loop(0, nc, conv_body, 0)

    def gate_dot(x_hi, x_lo, w_hi, w_lo):
        return (jnp.dot(x_hi, w_hi, preferred_element_type=F32)
                + jnp.dot(x_lo, w_hi, preferred_element_type=F32)
                + jnp.dot(x_hi, w_lo, preferred_element_type=F32))

    def chunk_scan(c, carry, k, d, out):
        rev = d == 1
        cols = slice(k * LANE, (k + 1) * LANE)
        start = pl.multiple_of(c * L, L)
        rows = pl.ds(start, L)
        xd, x_hi, x_lo = xd_ref[rows, cols], xh_ref[rows, cols], xl_ref[rows, cols]
        yield
        za = gate_dot(x_hi, x_lo, wah_ref[d, k], wal_ref[d, k])
        zx = gate_dot(x_hi, x_lo, wxh_ref[d, k], wxl_ref[d, k])
        yield
        r = _sigmoid(za + ba_ref[d:d + 1, cols])
        ig = _sigmoid(zx + bx_ref[d:d + 1, cols])
        log_a = -LRU_C * r * sp[d:d + 1, cols]
        a = jnp.exp(log_a)
        u = jnp.sqrt(jnp.maximum(-_expm1(2.0 * log_a), 0.0)) * (ig * xd)
        yield
        step = 1
        while step < L:
            u = a * _shift_rows(u, step, rev, 0.0) + u
            a = a * _shift_rows(a, step, rev, 1.0)
            step *= 2
            yield
        h = a * carry + u
        last = 0 if rev else L - 1
        out.extend((start, h, h[last:last + 1, :]))

    chains = [(k, d) for k in range(LRU_HEADS) for d in range(2)]

    def scan_body(i, carry):
        res = [[] for _ in chains]
        gens = [chunk_scan(i if d == 0 else nc - 1 - i, carry[n], k, d, res[n])
                for n, (k, d) in enumerate(chains)]
        for _ in _lockstep(gens):
            pass
        for n, (k, d) in enumerate(chains):
            dst = y_ref if d == 0 else hb_ref
            dst[pl.ds(res[n][0], L), k * LANE:(k + 1) * LANE] = res[n][1]
        return tuple(r[2] for r in res)

    def gate_body(c, carry):
        rows = pl.ds(pl.multiple_of(c * L, L), L)
        y_ref[rows, :] = (y_ref[rows, :] + hb_ref[rows, :]) * _gelu(dg_ref[rows, :].astype(F32))
        return carry

    if has_init:
        init = tuple(h0_ref[0, d:d + 1, k * LANE:(k + 1) * LANE] for k, d in chains)
    else:
        init = tuple(jnp.zeros((1, LANE), F32) for _ in chains)
    fin = lax.fori_loop(0, nc, scan_body, init)
    lax.fori_loop(0, nc, gate_body, 0)
    if emit_state:
        for n, (k, d) in enumerate(chains):
            hout_ref[0, d:d + 1, k * LANE:(k + 1) * LANE] = fin[n]


def _lru(pw, pf, bsz, seq, conv_w, conv_b, wa, ba, wx, bx, lam, h0, emit_state):
    n = bsz * seq
    has_init = h0 is not None
    wah, wal = _split(wa)
    wxh, wxl = _split(wx)
    width = LRU_HEADS * LANE
    gate_w = pl.BlockSpec((2, LRU_HEADS, DH, DH), lambda b, h: (0, h, 0, 0))
    vec2 = pl.BlockSpec((2, width), lambda b, h: (0, h))
    in_specs = [pl.BlockSpec((seq, width), lambda b, h: (b, COL_DX // width + h)),
                pl.BlockSpec((seq, width), lambda b, h: (b, COL_DG // width + h)),
                pl.BlockSpec((CONV_W, width), lambda b, h: (0, h)),
                pl.BlockSpec((1, width), lambda b, h: (0, h)),
                gate_w, gate_w, vec2, gate_w, gate_w, vec2, vec2]
    args = [pf, pw, conv_w, conv_b.reshape(1, MIX_W), wah, wal, ba, wxh, wxl, bx, lam]
    st_spec = pl.BlockSpec((1, 2, width), lambda b, h: (b, 0, h))
    if has_init:
        in_specs.append(st_spec)
        args.append(h0)
    out_specs = [pl.BlockSpec((seq, width), lambda b, h: (b, h))]
    out_shape = [jax.ShapeDtypeStruct((n, MIX_W), F32)]
    if emit_state:
        out_specs.append(st_spec)
        out_shape.append(jax.ShapeDtypeStruct((bsz, 2, MIX_W), F32))
    res = pl.pallas_call(
        functools.partial(_lru_kernel, seq=seq, has_init=has_init, emit_state=emit_state),
        grid=(bsz, N_HEAD // LRU_HEADS),
        in_specs=in_specs,
        out_specs=out_specs,
        out_shape=out_shape,
        scratch_shapes=[pltpu.VMEM((seq + 16, width), F32), pltpu.VMEM((seq, width), F32),
                        pltpu.VMEM((seq, width), F32), pltpu.VMEM((seq, width), BF16),
                        pltpu.VMEM((seq, width), BF16)],
        compiler_params=_cparams(("parallel", "parallel")),
        name="conv_rglru",
    )(*args)
    return res[0], (res[1] if emit_state else None)


def _head_rms(x):
    parts = []
    for h in range(N_HEAD):
        xh = x[:, h * DH:(h + 1) * DH]
        parts.append(xh * lax.rsqrt(jnp.mean(xh * xh, axis=-1, keepdims=True) + EPS))
    return jnp.concatenate(parts, axis=-1)


def _merge_kernel(af_ref, ab_ref, bf_ref, bb_ref, cf_ref, cb_ref, yd_ref, ao_ref, bg_ref, cg_ref,
                  gt0_ref, gt1_ref, gt2_ref, gt3_ref, x_ref, mod_ref, g2_ref, wbr_ref, wout_ref,
                  xo_ref, h2_ref, h2b_ref):
    ya = _sigmoid(ao_ref[...].astype(F32)) * _head_rms(af_ref[...] + ab_ref[...])
    yb = _silu(bg_ref[...].astype(F32)) * _head_rms(bf_ref[...] + bb_ref[...])
    yc = _silu(cg_ref[...].astype(F32)) * _head_rms(cf_ref[...] + cb_ref[...])
    merged = None
    gate_refs = (gt0_ref, gt1_ref, gt2_ref, gt3_ref)
    for i, y in enumerate((ya, yb, yc, yd_ref[...])):
        proj = jnp.dot(y.astype(BF16), wbr_ref[i], preferred_element_type=F32)
        term = _sigmoid(gate_refs[i][...]) * proj.astype(BF16)
        merged = term if merged is None else merged + term
    out = jnp.dot(merged, wout_ref[...], preferred_element_type=F32)
    g1 = mod_ref[0, :, 2 * D_MODEL:3 * D_MODEL]
    sh2 = mod_ref[0, :, 3 * D_MODEL:4 * D_MODEL]
    sc2 = mod_ref[0, :, 4 * D_MODEL:5 * D_MODEL]
    x = x_ref[...] + g1 * out
    xo_ref[...] = x
    inv = lax.rsqrt(jnp.mean(x * x, axis=-1, keepdims=True) + EPS)
    h2 = x * inv * g2_ref[...] * (1.0 + sc2) + sh2
    h2_ref[...] = h2
    h2b_ref[...] = h2.astype(BF16)


def _merge(mix_outs, yd, p, x, mod_l, norm2_g, w_branch, w_out, seq_len, layer):
    n = x.shape[0]
    tn = 256
    rows = mod_l.shape[0]
    mod_map = (lambda i: (0, 0, 0)) if rows == 1 else (lambda i: ((i * tn) // seq_len, 0, 0))
    tok = lambda cb: (lambda i: (i, cb))
    in_specs = [pl.BlockSpec((tn, MIX_W), tok(0))] * 7
    in_specs += [pl.BlockSpec((tn, MIX_W), tok(COL_AO // MIX_W)),
                 pl.BlockSpec((tn, MIX_W), tok(COL_BG // MIX_W)),
                 pl.BlockSpec((tn, MIX_W), tok(COL_CG // MIX_W)),
                 *[pl.BlockSpec((tn, D_MODEL), tok(COL_GATES // D_MODEL + i)) for i in range(N_BRANCH)],
                 pl.BlockSpec((tn, D_MODEL), tok(0)),
                 pl.BlockSpec((1, 1, 6 * D_MODEL), mod_map),
                 pl.BlockSpec((1, D_MODEL), lambda i: (0, 0)),
                 pl.BlockSpec((None, N_BRANCH, MIX_W, D_MODEL), lambda i: (layer, 0, 0, 0)),
                 pl.BlockSpec((None, D_MODEL, D_MODEL), lambda i: (layer, 0, 0))]
    out_spec = pl.BlockSpec((tn, D_MODEL), tok(0))
    return pl.pallas_call(
        _merge_kernel,
        grid=(n // tn,),
        in_specs=in_specs,
        out_specs=[out_spec, out_spec, out_spec],
        out_shape=[jax.ShapeDtypeStruct((n, D_MODEL), F32), jax.ShapeDtypeStruct((n, D_MODEL), F32),
                   jax.ShapeDtypeStruct((n, D_MODEL), BF16)],
        compiler_params=_cparams(("parallel",)),
        name="branch_merge",
    )(*mix_outs, yd, p, p, p, p, p, p, p, x, mod_l, norm2_g, w_branch, w_out)


ROUTE_TN = 512
ROUTE_LG = LANE


def _top16_exact(vals):
    n_rows = vals.shape[0]
    rowid = lax.broadcasted_iota(jnp.int32, vals.shape, 0).astype(F32)
    rank = jnp.full(vals.shape, float(PEER_TOPK), F32)
    tops = []
    for r in range(PEER_TOPK):
        m = jnp.max(vals, axis=0, keepdims=True)
        idx = jnp.min(jnp.where(vals == m, rowid, float(n_rows)), axis=0, keepdims=True)
        sel = rowid == idx
        rank = jnp.where(sel, float(r), rank)
        vals = jnp.where(sel, -jnp.inf, vals)
        tops.append(m)
    return jnp.concatenate(tops, axis=0), rank


_TAKEN_BASE = -3.0e38
_TAKEN_STEP = 2.0e36
_TAKEN_BELOW = _TAKEN_BASE + 0.5 * _TAKEN_STEP


def _top16_quick_many(vals_list):
    work = list(vals_list)
    tops = [[] for _ in work]
    for r in range(PEER_TOPK):
        for i in range(len(work)):
            m = jnp.max(work[i], axis=0, keepdims=True)
            work[i] = jnp.where(work[i] == m, _TAKEN_BASE - r * _TAKEN_STEP, work[i])
            tops[i].append(m)
    out = []
    for i in range(len(work)):
        taken = work[i] <= _TAKEN_BELOW
        rank = jnp.where(taken, jnp.round((_TAKEN_BASE - work[i]) * (1.0 / _TAKEN_STEP)),
                         float(PEER_TOPK))
        n_taken = jnp.sum(jnp.where(taken, 1.0, 0.0), axis=0, keepdims=True)
        out.append((jnp.concatenate(tops[i], axis=0), rank, n_taken))
    return out


_CAND_PIECES = (('a', 0, 0, 16), ('a', 0, 8, 16), ('a', 1, 0, 8), ('a', 2, 0, 5), ('a', 3, 0, 4),
                ('b', 0, 0, (4, 8)), ('b', 0, 8, (8, 16)), ('b', 1, 0, (4, 8)), ('b', 2, 0, (4, 5)))


def _pair_pieces(x1, x2, combine):
    out = []
    for kind, fixed, off, _ in _CAND_PIECES:
        if kind == 'a':
            out.append(combine(x1[fixed:fixed + 1, :], x2[off:off + 8, :]))
        else:
            out.append(combine(x1[off:off + 8, :], x2[fixed:fixed + 1, :]))
    return out


def _candidate_sums(s1, s2):
    K = PEER_TOPK
    n = s1.shape[1]
    iota8 = lax.broadcasted_iota(jnp.int32, (8, n), 0)
    sums = _pair_pieces(s1, s2, lambda x, y: x + y)
    vals, poss = [], []
    for (kind, fixed, off, lim), sm in zip(_CAND_PIECES, sums):
        idx = iota8 + off
        if kind == 'a':
            valid = idx < lim
            pos = fixed * K + idx
        else:
            valid = (idx >= lim[0]) & (idx < lim[1])
            pos = idx * K + fixed
        vals.append(jnp.where(valid, sm, -jnp.inf))
        poss.append(jnp.where(valid, pos, K * K).astype(F32))
    return jnp.concatenate(vals, axis=0), jnp.concatenate(poss, axis=0)


def _choose_exact(vals, posid):
    K = PEER_TOPK
    chosen = jnp.zeros(vals.shape, F32)
    for _ in range(K):
        m = jnp.max(vals, axis=0, keepdims=True)
        idx = jnp.min(jnp.where(vals == m, posid, float(K * K)), axis=0, keepdims=True)
        sel = posid == idx
        chosen = jnp.where(sel, 1.0, chosen)
        vals = jnp.where(sel, -jnp.inf, vals)
    return chosen


def _choose_quick_many(vals_list):
    work = list(vals_list)
    for _ in range(PEER_TOPK):
        for i in range(len(work)):
            m = jnp.max(work[i], axis=0, keepdims=True)
            work[i] = jnp.where(work[i] == m, _TAKEN_BASE, work[i])
    out = []
    for w in work:
        chosen = jnp.where(w == _TAKEN_BASE, 1.0, 0.0)
        out.append((chosen, jnp.sum(chosen, axis=0, keepdims=True)))
    return out


def _route_kernel(h_ref, wqh_ref, wql_ref, kh_ref, kl_ref, a_ref, b_ref, r2_ref, c_ref,
                  qh_ref, ql_ref, s_ref, top_ref, rank_ref, ch_ref):
    K = PEER_TOPK
    n_lg = ROUTE_TN // ROUTE_LG
    q = _dot3(h_ref[...], wqh_ref[...], wql_ref[...])
    q_hi, q_lo = _split(q)
    for i in range(2 * PEER_HEADS):
        qh_ref[i] = q_hi[:, i * N_KEYS:(i + 1) * N_KEYS]
        ql_ref[i] = q_lo[:, i * N_KEYS:(i + 1) * N_KEYS]
    nt = (((1,), (1,)), ((), ()))
    lanes = [slice(lg * ROUTE_LG, (lg + 1) * ROUTE_LG) for lg in range(n_lg)]

    def miscount(counts):
        worst = None
        for cnt in counts:
            dev = jnp.abs(cnt - float(K))
            worst = dev if worst is None else jnp.maximum(worst, dev)
        return jnp.max(worst) > 0.0

    def head_body(h, carry):
        for half in range(2):
            kh, kl = kh_ref[h, half], kl_ref[h, half]
            qh, ql = qh_ref[2 * h + half], ql_ref[2 * h + half]
            s_ref[half] = (lax.dot_general(kh, qh, nt, preferred_element_type=F32)
                           + lax.dot_general(kl, qh, nt, preferred_element_type=F32)
                           + lax.dot_general(kh, ql, nt, preferred_element_type=F32))

        problems = [(half, lg) for half in range(2) for lg in range(n_lg)]
        quick = _top16_quick_many([s_ref[half, :, lanes[lg]] for half, lg in problems])
        counts = []
        for (half, lg), (top, rank, n_taken) in zip(problems, quick):
            top_ref[half, lg] = top
            rank_ref[half, lg] = rank
            counts.append(n_taken)

        @pl.when(miscount(counts))
        def _():
            for half in range(2):
                for lg in range(n_lg):
                    top, rank = _top16_exact(s_ref[half, :, lanes[lg]])
                    top_ref[half, lg] = top
                    rank_ref[half, lg] = rank

        cands = [_candidate_sums(top_ref[0, lg], top_ref[1, lg])[0] for lg in range(n_lg)]
        counts = []
        for lg, (chosen, n_taken) in enumerate(_choose_quick_many(cands)):
            ch_ref[lg] = chosen
            counts.append(n_taken)

        @pl.when(miscount(counts))
        def _():
            for lg in range(n_lg):
                vals, posid = _candidate_sums(top_ref[0, lg], top_ref[1, lg])
                ch_ref[lg] = _choose_exact(vals, posid)

        for lg in range(n_lg):
            ls = lanes[lg]
            st1, st2 = s_ref[0, :, ls], s_ref[1, :, ls]
            s1, s2 = top_ref[0, lg], top_ref[1, lg]
            rank1, rank2 = rank_ref[0, lg], rank_ref[1, lg]
            chosen = ch_ref[lg]
            ch = [chosen[8 * i:8 * i + 8, :] for i in range(len(_CAND_PIECES))]
            e1 = jnp.exp(s1 - s1[0:1, :])
            e2 = jnp.exp(s2 - s2[0:1, :])
            pair = _pair_pieces(e1, e2, lambda x, y: x * y)
            z = sum(jnp.sum(c * p, axis=0, keepdims=True) for c, p in zip(ch, pair))
            iota8 = lax.broadcasted_iota(jnp.int32, (8, ROUTE_LG), 0)
            low = ch[5] + ch[7] + ch[8]
            for a, cnt in ((3, ch[4]), (2, ch[3]), (1, ch[2]), (0, ch[0] + ch[1])):
                low = jnp.where(iota8 == a, jnp.sum(cnt, axis=0, keepdims=True), low)
            counts = jnp.concatenate([low, ch[6]], axis=0)
            c_dense = jnp.zeros((N_KEYS, ROUTE_LG), F32)
            for a in range(K):
                c_dense = jnp.where(rank1 == float(a), counts[a:a + 1, :], c_dense)
            a_ref[h, :, ls] = jnp.where(rank1 < float(K), jnp.exp(st1 - s1[0:1, :]), 0.0) / z
            b_ref[h, :, ls] = jnp.where(rank2 < float(K), jnp.exp(st2 - s2[0:1, :]), 0.0).astype(BF16)
            r2_ref[h, :, ls] = rank2.astype(BF16)
            c_ref[h, :, ls] = c_dense
        return carry

    lax.fori_loop(0, PEER_HEADS, head_body, 0)


def _route(h2, wq_hi, wq_lo, keys_hi, keys_lo, layer):
    n = h2.shape[0]
    tn = ROUTE_TN
    dense = pl.BlockSpec((PEER_HEADS, N_KEYS, tn), lambda i: (0, 0, i))
    wspec = pl.BlockSpec((None, D_MODEL, PEER_HEADS * 2 * N_KEYS), lambda i: (layer, 0, 0))
    kspec = pl.BlockSpec((None, PEER_HEADS, 2, N_KEYS, N_KEYS), lambda i: (layer, 0, 0, 0, 0))
    return pl.pallas_call(
        _route_kernel,
        grid=(n // tn,),
        in_specs=[pl.BlockSpec((tn, D_MODEL), lambda i: (i, 0)), wspec, wspec, kspec, kspec],
        out_specs=[dense] * 4,
        out_shape=[jax.ShapeDtypeStruct((PEER_HEADS, N_KEYS, n), dt) for dt in (F32, BF16, BF16, F32)],
        scratch_shapes=[pltpu.VMEM((2 * PEER_HEADS, tn, N_KEYS), BF16),
                        pltpu.VMEM((2 * PEER_HEADS, tn, N_KEYS), BF16),
                        pltpu.VMEM((2, N_KEYS, tn), F32),
                        pltpu.VMEM((2, tn // ROUTE_LG, PEER_TOPK, ROUTE_LG), F32),
                        pltpu.VMEM((2, tn // ROUTE_LG, N_KEYS, ROUTE_LG), F32),
                        pltpu.VMEM((tn // ROUTE_LG, 8 * len(_CAND_PIECES), ROUTE_LG), F32)],
        compiler_params=_cparams(("parallel",)),
        name="peer_route",
    )(h2, wq_hi, wq_lo, keys_hi, keys_lo)


PEER_TN = 512
PEER_TE = 1024


def _gelu_bf16(t):
    x = t.astype(BF16)
    y2 = x * (1.0 + 0.044715 * (x * x)) * (-2.0 * 0.7978845608028654)
    return x / (1.0 + jnp.exp(y2))


BF16_ROWS = 16


def _row_to_packed(row):
    tile = jnp.broadcast_to(row, (BF16_ROWS, row.shape[1])).astype(BF16)
    return jnp.concatenate([tile] * (N_KEYS // BF16_ROWS), axis=0)


def _expert_kernel(hb_ref, u_ref, vt_ref, a_ref, b_ref, r2_ref, c_ref, x_ref, mod_ref, o_ref,
                   acc_ref):
    e = pl.program_id(1)
    rows_per_tile = PEER_TE // N_KEYS

    @pl.when(e == 0)
    def _():
        acc_ref[...] = jnp.zeros_like(acc_ref)

    t_t = lax.dot_general(u_ref[...], hb_ref[...], (((1,), (1,)), ((), ())),
                          preferred_element_type=F32)
    w_parts = []
    for j in range(rows_per_tile):
        e1 = e * rows_per_tile + j
        g = None
        for h in range(PEER_HEADS):
            a_row = _row_to_packed(a_ref[h, pl.ds(e1, 1), :])
            c_row = _row_to_packed(c_ref[h, pl.ds(e1, 1), :])
            term = a_row * jnp.where(r2_ref[h] < c_row, b_ref[h], 0.0)
            g = term if g is None else g + term
        w_parts.append(g * _gelu_bf16(t_t[j * N_KEYS:(j + 1) * N_KEYS, :]))
    w_t = jnp.concatenate(w_parts, axis=0)
    acc_ref[...] += jnp.dot(vt_ref[...], w_t, preferred_element_type=F32)

    @pl.when(e == pl.num_programs(1) - 1)
    def _():
        g2 = mod_ref[0, :, 5 * D_MODEL:6 * D_MODEL]
        o_ref[...] = x_ref[...] + g2 * acc_ref[...].T


def _experts(h2b, u_b, vt_b, dense, x, mod_l, seq_len, layer):
    n = x.shape[0]
    tn, te = PEER_TN, PEER_TE
    rows = mod_l.shape[0]
    mod_map = (lambda i, e: (0, 0, 0)) if rows == 1 else (lambda i, e: ((i * tn) // seq_len, 0, 0))
    dspec = pl.BlockSpec((PEER_HEADS, N_KEYS, tn), lambda i, e: (0, 0, i))
    return pl.pallas_call(
        _expert_kernel,
        grid=(n // tn, N_EXPERTS // te),
        in_specs=[pl.BlockSpec((tn, D_MODEL), lambda i, e: (i, 0)),
                  pl.BlockSpec((None, te, D_MODEL), lambda i, e: (layer, e, 0)),
                  pl.BlockSpec((None, D_MODEL, te), lambda i, e: (layer, 0, e)),
                  dspec, dspec, dspec, dspec,
                  pl.BlockSpec((tn, D_MODEL), lambda i, e: (i, 0)),
                  pl.BlockSpec((1, 1, 6 * D_MODEL), mod_map)],
        out_specs=pl.BlockSpec((tn, D_MODEL), lambda i, e: (i, 0)),
        out_shape=jax.ShapeDtypeStruct((n, D_MODEL), F32),
        scratch_shapes=[pltpu.VMEM((D_MODEL, tn), F32)],
        compiler_params=_cparams(("parallel", "arbitrary")),
        name="peer_experts",
    )(h2b, u_b, vt_b, *dense, x, mod_l)


def _final_norm_kernel(x_ref, g_ref, o_ref):
    x = x_ref[...]
    o_ref[...] = x * lax.rsqrt(jnp.mean(x * x, axis=-1, keepdims=True) + EPS) * g_ref[...]


def _final_norm(x, g):
    n = x.shape[0]
    tn = 512
    return pl.pallas_call(
        _final_norm_kernel,
        grid=(n // tn,),
        in_specs=[pl.BlockSpec((tn, D_MODEL), lambda i: (i, 0)),
                  pl.BlockSpec((1, D_MODEL), lambda i: (0, 0))],
        out_specs=pl.BlockSpec((tn, D_MODEL), lambda i: (i, 0)),
        out_shape=jax.ShapeDtypeStruct((n, D_MODEL), F32),
        compiler_params=_cparams(("parallel",)),
        name="final_norm",
    )(x, g)


def _pack_columns(a, src, width):
    parts = [a[..., lo:hi] for lo, hi in src]
    used = sum(hi - lo for lo, hi in src)
    if used < width:
        parts.append(jnp.zeros(a.shape[:-1] + (width - used,), a.dtype))
    return jnp.concatenate(parts, axis=-1)


def _position_code(rows):
    quarter = D_MODEL // 4
    omega = 1.0 / (POS_BASE ** (jnp.arange(quarter, dtype=F32) / quarter))
    r, col = jnp.meshgrid(jnp.arange(rows, dtype=F32), jnp.arange(GRID_W, dtype=F32), indexing='ij')

    def enc(pos):
        ang = pos.reshape(-1, 1) * omega
        return jnp.concatenate([jnp.sin(ang), jnp.cos(ang)], axis=-1)
    return jnp.concatenate([enc(r), enc(col)], axis=-1)


def _layer(x, bsz, seq, mod_l, lp, state, emit_state):
    layer = lp['layer']
    pw = _inproj(x, mod_l, lp['norm1_g'], lp['w_wide'], lp['b_wide'], seq, layer, 2432, BF16,
                 "in_projection_wide")
    pf = _inproj(x, mod_l, lp['norm1_g'], lp['w_fine'], lp['b_fine'], seq, layer, N_FINE, F32,
                 "in_projection_fine")
    if state is None:
        st_a = st_b = st_c = st_d = None
    else:
        c0, n0, m0, sb0, sc0, hd0 = state
        st_a = (c0, n0, m0)
        st_b = jnp.swapaxes(sb0, -1, -2)
        st_c = jnp.swapaxes(sc0, -1, -2)
        st_d = hd0
    a_f, a_b, new_a = _mlstm(pw, pf, bsz, seq, st_a, emit_state)
    b_f, b_b, new_b = _hgrn((pw, pf), bsz, seq, lp['hgrn_lb'], st_b, emit_state)
    c_f, c_b, new_c = _gla((pw, pf), bsz, seq, lp['gla_up_pad'], lp['gla_up_b'], st_c, emit_state)
    yd, new_d = _lru(pw, pf, bsz, seq, lp['conv_w'], lp['conv_b'], lp['lru_w_a'], lp['lru_b_a'],
                     lp['lru_w_x'], lp['lru_b_x'], lp['lru_lambda'], st_d, emit_state)
    x1, h2, h2b = _merge((a_f, a_b, b_f, b_b, c_f, c_b), yd, pw, x, mod_l, lp['norm2_g'],
                         lp['w_branch'], lp['w_out'], seq, layer)
    dense = _route(h2, lp['wq_hi'], lp['wq_lo'], lp['keys_hi'], lp['keys_lo'], layer)
    x2 = _experts(h2b, lp['peer_u'], lp['peer_vt'], dense, x1, mod_l, seq, layer)
    new_state = None
    if emit_state:
        new_state = (*new_a, jnp.swapaxes(new_b, -1, -2), jnp.swapaxes(new_c, -1, -2), new_d)
    return x2, new_state


def kernel(x_prompt, x_sample, state_mlstm_C, state_mlstm_n, state_mlstm_m, state_hgrn_S,
           state_gla_S, state_lru_h, c, c_ctx, norm1_g, norm2_g, final_norm_g, w_mod, b_mod,
           w_in, b_in, w_gla_up, b_gla_up, hgrn_lower_bounds, conv_w, conv_b, lru_w_a, lru_b_a,
           lru_w_x, lru_b_x, lru_lambda, w_branch, w_out, peer_w_q, peer_sub_keys, peer_u, peer_v):
    lb_soft = jax.nn.softmax(hgrn_lower_bounds.astype(F32), axis=0)
    lb_all = jnp.cumsum(lb_soft, axis=0) - lb_soft[0:1]
    w_wide = _pack_columns(w_in, _WIDE_SRC, N_WIDE).astype(BF16)
    w_fine = _pack_columns(w_in, _FINE_SRC, N_FINE).astype(BF16)
    b_wide = _pack_columns(b_in, _WIDE_SRC, N_WIDE).reshape(DEPTH, 1, N_WIDE)
    b_fine = _pack_columns(b_in, _FINE_SRC, N_FINE).reshape(DEPTH, 1, N_FINE)
    kw = N_HEAD * DK_C
    up_pad = jnp.zeros((DEPTH, 2, SMALL_W, kw), F32)
    for d in range(2):
        lo = 2 * N_HEAD * 2 + d * R_C
        up_pad = up_pad.at[:, d, lo:lo + R_C, :].set(w_gla_up[:, d].astype(F32))
    wq_hi, wq_lo = _split(peer_w_q)
    keys_hi, keys_lo = _split(peer_sub_keys)
    u_b = peer_u.astype(BF16)
    vt_b = jnp.swapaxes(peer_v, 1, 2).astype(BF16)
    wbr_b = w_branch.astype(BF16)
    wout_b = w_out.astype(BF16)

    def layer_params(l):
        return {
            'norm1_g': norm1_g[l].reshape(1, D_MODEL), 'norm2_g': norm2_g[l].reshape(1, D_MODEL),
            'layer': l, 'w_wide': w_wide, 'b_wide': b_wide[l], 'w_fine': w_fine, 'b_fine': b_fine[l],
            'hgrn_lb': lb_all[l],
            'gla_up_pad': up_pad[l], 'gla_up_b': b_gla_up[l],
            'conv_w': conv_w[l], 'conv_b': conv_b[l], 'lru_w_a': lru_w_a[l], 'lru_b_a': lru_b_a[l],
            'lru_w_x': lru_w_x[l], 'lru_b_x': lru_b_x[l], 'lru_lambda': lru_lambda[l],
            'w_branch': wbr_b, 'w_out': wout_b, 'wq_hi': wq_hi, 'wq_lo': wq_lo,
            'keys_hi': keys_hi, 'keys_lo': keys_lo, 'peer_u': u_b, 'peer_vt': vt_b,
        }

    cond = jnp.concatenate([c, c_ctx[None, :]], axis=0).astype(F32)
    n_dec = c.shape[0]
    mod = _modulation(cond, w_mod, b_mod)
    final_g = final_norm_g.reshape(1, D_MODEL)

    bp, tp, _ = x_prompt.shape
    x = x_prompt.reshape(bp * tp, D_MODEL)
    ctx_states = []
    for l in range(DEPTH):
        mod_l = mod[l, n_dec:n_dec + 1].reshape(1, 1, 6 * D_MODEL)
        x, st = _layer(x, bp, tp, mod_l, layer_params(l), None, True)
        ctx_states.append(st)
    y_prompt = _final_norm(x, final_g).reshape(bp, tp, D_MODEL)
    new_states = tuple(jnp.stack([s[i] for s in ctx_states], axis=1) for i in range(6))

    bd, td, _ = x_sample.shape
    x = _add_position(x_sample, _position_code(td // GRID_W)).reshape(bd * td, D_MODEL)
    for l in range(DEPTH):
        cached = (state_mlstm_C[:, l].astype(F32), state_mlstm_n[:, l].astype(F32),
                  state_mlstm_m[:, l].astype(F32), state_hgrn_S[:, l].astype(F32),
                  state_gla_S[:, l].astype(F32), state_lru_h[:, l].astype(F32))
        mod_l = mod[l, 0:n_dec].reshape(n_dec, 1, 6 * D_MODEL)
        x, _ = _layer(x, bd, td, mod_l, layer_params(l), cached, False)
    y_sample = _final_norm(x, final_g).reshape(bd, td, D_MODEL)
    return (y_prompt, y_sample) + new_states
```

```python
import functools

import jax
import jax.numpy as jnp
from jax import lax
from jax.experimental import pallas as pl
from jax.experimental.pallas import tpu as pltpu

F32 = jnp.float32
BF16 = jnp.bfloat16
HIGHEST = lax.Precision.HIGHEST

D_MODEL = 1024
DEPTH = 4
GRID_W = 64
EPS = 1e-6
NEG_BIG = -1e30
MAX_EXP_ARG = 80.0
MIX_W = 512
N_HEAD = 4
DH = 128
DK_C = 64
R_C = 16
GLA_TAU = 16.0
CONV_W = 4
LRU_C = 8.0
N_BRANCH = 4
N_KEYS = 128
N_EXPERTS = N_KEYS * N_KEYS
PEER_HEADS = 8
PEER_TOPK = 16
POS_BASE = 10000.0

LANE = 128
SUBLANES = 8
VMEM_LIMIT = 56 * 1024 * 1024

COL_GATES = 0
COL_AQ, COL_AK, COL_AV, COL_AO = 4096, 4608, 5120, 5632
COL_BQ, COL_BI, COL_BG = 6144, 6656, 7168
COL_CQ, COL_CK, COL_CV, COL_CG = 7680, 7936, 8192, 8704
COL_DG = 9216
N_WIDE = 9728
COL_BF, COL_DX = 0, 1024
COL_SMALL = 1536
SMALL_W = 256
N_FINE = COL_SMALL + SMALL_W

_WIDE_SRC = ((7216, 11312), (0, 2048), (2064, 2576), (3600, 4624), (4624, 6160), (6704, 7216))
_FINE_SRC = ((2576, 3600), (6192, 6704), (2048, 2064), (6160, 6192))

CHUNK_A = 128
CHUNK_G = 64
SUB_G = 16
CHUNK_D = 128
LRU_HEADS = 2


def _cparams(sem):
    return pltpu.CompilerParams(dimension_semantics=sem, vmem_limit_bytes=VMEM_LIMIT)


def _bdot(a, b):
    return jnp.dot(a.astype(BF16), b.astype(BF16), preferred_element_type=F32)


def _bdot_nt(a, b):
    return lax.dot_general(a.astype(BF16), b.astype(BF16), (((1,), (1,)), ((), ())),
                           preferred_element_type=F32)


def _bdot_tn(a, b):
    return lax.dot_general(a.astype(BF16), b.astype(BF16), (((0,), (0,)), ((), ())),
                           preferred_element_type=F32)


def _split(a):
    hi = a.astype(BF16)
    lo = (a - hi.astype(F32)).astype(BF16)
    return hi, lo


def _dot3(a, b_hi, b_lo):
    a_hi, a_lo = _split(a)
    return (jnp.dot(a_hi, b_hi, preferred_element_type=F32)
            + jnp.dot(a_lo, b_hi, preferred_element_type=F32)
            + jnp.dot(a_hi, b_lo, preferred_element_type=F32))


def _log_sigmoid(z):
    return jnp.minimum(z, 0.0) - jnp.log1p(jnp.exp(-jnp.abs(z)))


def _sigmoid(z):
    return 0.5 * jnp.tanh(0.5 * z) + 0.5


def _gelu(x):
    return 0.5 * x * (1.0 + jnp.tanh(0.7978845608028654 * (x + 0.044715 * x * x * x)))


def _silu(x):
    return x * _sigmoid(x)


def _expm1(y):
    u = jnp.exp(y)
    near = (u - 1.0) * y / jnp.log(u)
    return jnp.where(u == 1.0, y, jnp.where(u < 0.5, u - 1.0, near))


def _mod_kernel(c_ref, w_ref, b_ref, o_ref):
    a = _silu(c_ref[...])
    o_ref[0] = jnp.dot(a, w_ref[0], precision=HIGHEST, preferred_element_type=F32) + b_ref[0]


def _modulation(cond, w_mod, b_mod):
    r = cond.shape[0]
    tc = 1536
    return pl.pallas_call(
        _mod_kernel,
        grid=(DEPTH, 6 * D_MODEL // tc),
        in_specs=[pl.BlockSpec((r, D_MODEL), lambda l, j: (0, 0)),
                  pl.BlockSpec((1, D_MODEL, tc), lambda l, j: (l, 0, j)),
                  pl.BlockSpec((1, 1, tc), lambda l, j: (l, 0, j))],
        out_specs=pl.BlockSpec((1, r, tc), lambda l, j: (l, 0, j)),
        out_shape=jax.ShapeDtypeStruct((DEPTH, r, 6 * D_MODEL), F32),
        compiler_params=_cparams(("parallel", "parallel")),
        name="modulation",
    )(cond, w_mod, b_mod.reshape(DEPTH, 1, 6 * D_MODEL))


def _addpos_kernel(x_ref, p_ref, o_ref):
    o_ref[0] = x_ref[0] + p_ref[...]


def _add_position(x, pos):
    b, t, d = x.shape
    tt = 512
    return pl.pallas_call(
        _addpos_kernel,
        grid=(b, t // tt),
        in_specs=[pl.BlockSpec((1, tt, d), lambda i, j: (i, j, 0)),
                  pl.BlockSpec((tt, d), lambda i, j: (j, 0))],
        out_specs=pl.BlockSpec((1, tt, d), lambda i, j: (i, j, 0)),
        out_shape=jax.ShapeDtypeStruct(x.shape, F32),
        compiler_params=_cparams(("parallel", "parallel")),
        name="add_position",
    )(x, pos)


def _inproj_kernel(x_ref, mod_ref, g_ref, w_ref, b_ref, o_ref, h_ref):
    @pl.when(pl.program_id(1) == 0)
    def _():
        x = x_ref[...]
        inv = lax.rsqrt(jnp.mean(x * x, axis=-1, keepdims=True) + EPS)
        sh = mod_ref[0, :, 0:D_MODEL]
        sc = mod_ref[0, :, D_MODEL:2 * D_MODEL]
        h_ref[...] = (x * inv * g_ref[...] * (1.0 + sc) + sh).astype(BF16)

    acc = jnp.dot(h_ref[...], w_ref[...], preferred_element_type=F32) + b_ref[...]
    o_ref[...] = acc.astype(o_ref.dtype)


def _inproj(x, mod_l, norm_g, w_pack, b_pack, seq_len, layer, tc, out_dtype, name):
    n = x.shape[0]
    width = w_pack.shape[-1]
    tn = 1024
    rows = mod_l.shape[0]
    if rows == 1:
        mod_map = lambda i, j: (0, 0, 0)
    else:
        mod_map = lambda i, j: ((i * tn) // seq_len, 0, 0)
    return pl.pallas_call(
        _inproj_kernel,
        grid=(n // tn, width // tc),
        in_specs=[pl.BlockSpec((tn, D_MODEL), lambda i, j: (i, 0)),
                  pl.BlockSpec((1, 1, 6 * D_MODEL), mod_map),
                  pl.BlockSpec((1, D_MODEL), lambda i, j: (0, 0)),
                  pl.BlockSpec((None, D_MODEL, tc), lambda i, j: (layer, 0, j)),
                  pl.BlockSpec((1, tc), lambda i, j: (0, j))],
        out_specs=pl.BlockSpec((tn, tc), lambda i, j: (i, j)),
        out_shape=jax.ShapeDtypeStruct((n, width), out_dtype),
        scratch_shapes=[pltpu.VMEM((tn, D_MODEL), BF16)],
        compiler_params=_cparams(("parallel", "arbitrary")),
        name=name,
    )(x, mod_l, norm_g, w_pack, b_pack)


SCAN_GROUP = 4


def _lockstep(gens):
    alive = list(gens)
    while alive:
        still = []
        for gen in alive:
            try:
                next(gen)
                still.append(gen)
            except StopIteration:
                pass
        alive = still
        if alive:
            yield


def _per_sequence(body, n_group, shared):
    def kern(*refs):
        def run(phase):
            stages = [body(*[r if i in shared else r.at[g] for i, r in enumerate(refs)], phase=phase)
                      for g in range(n_group)]
            for _ in _lockstep([s for s in stages if s is not None]):
                pass

        pl.when(pl.program_id(1) == 0)(lambda: run('init'))
        run('main')
        pl.when(pl.program_id(1) == pl.num_programs(1) - 1)(lambda: run('emit'))
    return kern


def _cumsum_rows(x, rev):
    k = 1
    while k < x.shape[0]:
        x = x + _shift_rows(x, k, rev, 0.0)
        k *= 2
    return x


def _causal_mask(length, rev):
    row = lax.broadcasted_iota(jnp.int32, (length, length), 0)
    col = lax.broadcasted_iota(jnp.int32, (length, length), 1)
    return (col >= row) if rev else (col <= row)


def _mlstm_kernel(*refs, has_init, emit_state, phase):
    L = CHUNK_A
    ins = list(refs)
    dirs = [ins[0:4], ins[4:8]]
    pos = 8
    if has_init:
        c0_ref, n0_ref, m0_ref = ins[pos:pos + 3]
        pos += 3
    outs = ins[pos:pos + 2]
    pos += 2
    if emit_state:
        cout_ref, nout_ref, mout_ref = ins[pos:pos + 3]
        pos += 3
    cs_ref, ns_ref, ms_ref = ins[pos:pos + 3]

    if phase == 'init':
        if has_init:
            cs_ref[...] = c0_ref[...]
            ns_ref[...] = n0_ref[...]
            ms_ref[...] = m0_ref[...]
        else:
            cs_ref[...] = jnp.zeros_like(cs_ref)
            ns_ref[...] = jnp.zeros_like(ns_ref)
            ms_ref[...] = jnp.zeros_like(ms_ref)
        return
    if phase == 'emit':
        if emit_state:
            cout_ref[...] = cs_ref[...]
            nout_ref[...] = ns_ref[...]
            mout_ref[...] = ms_ref[...]
        return

    def direction(d):
        rev = d == 1
        q_ref, k_ref, v_ref, s_ref = dirs[d]
        o_ref = outs[d]
        mask = _causal_mask(L, rev)
        sm = s_ref[:, 0:LANE]
        lf = _log_sigmoid(sm)
        bc = jnp.dot(mask.astype(F32), lf, precision=HIGHEST,
                     preferred_element_type=F32)
        sm_t = sm.T
        bc_t = bc.T
        last = 0 if rev else L - 1
        heads = range(N_HEAD)
        hsl = [slice(h * DH, (h + 1) * DH) for h in heads]
        yield
        li_c, b_c, logw, m_prev, m_t, w_state = [], [], [], [], [], []
        for h in heads:
            ci, cf = d * N_HEAD + h, 2 * N_HEAD + d * N_HEAD + h
            li_c.append(sm[:, ci:ci + 1])
            b_c.append(bc[:, cf:cf + 1])
            li_r, b_r = sm_t[ci:ci + 1, :], bc_t[cf:cf + 1, :]
            m_prev.append(ms_ref[d, h][:, 0:1])
            logw.append(jnp.where(mask, b_c[h] - b_r + li_r, NEG_BIG))
            from_state = b_c[h] + m_prev[h]
            m_t.append(jnp.maximum(from_state, jnp.max(logw[h], axis=-1, keepdims=True)))
            w_state.append(jnp.exp(from_state - m_t[h]))
        q = [q_ref[:, hsl[h]].astype(F32) for h in heads]
        k = [k_ref[:, hsl[h]].astype(F32) * (DH ** -0.5) for h in heads]
        v = [v_ref[:, hsl[h]] for h in heads]
        c_st = [cs_ref[d, h] for h in heads]
        n_st = [ns_ref[d, h] for h in heads]
        qk = [_bdot_nt(q[h], k[h]) for h in heads]
        qc = [_bdot(q[h], c_st[h]) for h in heads]
        yield
        for h in heads:
            scores = qk[h] * jnp.exp(logw[h] - m_t[h])
            num = w_state[h] * qc[h] + _bdot(scores, v[h])
            den = (w_state[h] * jnp.sum(q[h] * n_st[h], axis=-1, keepdims=True)
                   + jnp.sum(scores, axis=-1, keepdims=True))
            floor = jnp.exp(jnp.minimum(-m_t[h], MAX_EXP_ARG))
            o_ref[:, hsl[h]] = num / jnp.maximum(jnp.abs(den), floor)
        yield
        for h in heads:
            m_new = m_t[h][last:last + 1, :]
            b_last = b_c[h][last:last + 1, :]
            kw = k[h] * jnp.exp(b_last - b_c[h] + li_c[h] - m_new)
            decay = jnp.exp(b_last + m_prev[h] - m_new)
            cs_ref[d, h] = decay * c_st[h] + _bdot_tn(kw, v[h])
            ns_ref[d, h] = decay * n_st[h] + jnp.sum(kw, axis=0, keepdims=True)
            ms_ref[d, h] = jnp.broadcast_to(m_new, (1, LANE))

    yield from _lockstep([direction(0), direction(1)])


def _scan_maps(nc):
    fwd = lambda cb: (lambda b, c: (b, c, cb))
    bwd = lambda cb: (lambda b, c: (b, nc - 1 - c, cb))
    return fwd, bwd


def _mlstm(pw, pf, bsz, seq, state, emit_state):
    L = CHUNK_A
    nc = seq // L
    n = bsz * seq
    g = min(bsz, SCAN_GROUP)
    has_init = state is not None
    pw3 = pw.reshape(bsz, seq, N_WIDE)
    pf3 = pf.reshape(bsz, seq, N_FINE)
    fwd, bwd = _scan_maps(nc)

    in_specs, args = [], []
    for mk in (fwd, bwd):
        for col in (COL_AQ, COL_AK, COL_AV):
            in_specs.append(pl.BlockSpec((g, L, MIX_W), mk(col // MIX_W)))
            args.append(pw3)
        in_specs.append(pl.BlockSpec((g, L, SMALL_W), mk(COL_SMALL // SMALL_W)))
        args.append(pf3)
    st_specs = [pl.BlockSpec((g, 2, N_HEAD, DH, DH), lambda b, c: (b, 0, 0, 0, 0)),
                pl.BlockSpec((g, 2, N_HEAD, 1, DH), lambda b, c: (b, 0, 0, 0, 0)),
                pl.BlockSpec((g, 2, N_HEAD, 1, LANE), lambda b, c: (b, 0, 0, 0, 0))]
    st_shapes = [jax.ShapeDtypeStruct((bsz, 2, N_HEAD, DH, DH), F32),
                 jax.ShapeDtypeStruct((bsz, 2, N_HEAD, 1, DH), F32),
                 jax.ShapeDtypeStruct((bsz, 2, N_HEAD, 1, LANE), F32)]
    if has_init:
        c0, n0, m0 = state
        in_specs += st_specs
        args += [c0, n0.reshape(bsz, 2, N_HEAD, 1, DH),
                 jnp.broadcast_to(m0[..., None, None], (bsz, 2, N_HEAD, 1, LANE))]
    out_specs = [pl.BlockSpec((g, L, MIX_W), fwd(0)), pl.BlockSpec((g, L, MIX_W), bwd(0))]
    out_shape = [jax.ShapeDtypeStruct((bsz, seq, MIX_W), F32)] * 2
    if emit_state:
        out_specs += st_specs
        out_shape += st_shapes
    body = functools.partial(_mlstm_kernel, has_init=has_init, emit_state=emit_state)
    res = pl.pallas_call(
        _per_sequence(body, g, ()),
        grid=(bsz // g, nc),
        in_specs=in_specs,
        out_specs=out_specs,
        out_shape=out_shape,
        scratch_shapes=[pltpu.VMEM((g, 2, N_HEAD, DH, DH), F32),
                        pltpu.VMEM((g, 2, N_HEAD, 1, DH), F32),
                        pltpu.VMEM((g, 2, N_HEAD, 1, LANE), F32)],
        compiler_params=_cparams(("parallel", "arbitrary")),
        name="mlstm_scan",
    )(*args)
    o_f, o_b = res[0].reshape(n, MIX_W), res[1].reshape(n, MIX_W)
    new_state = None
    if emit_state:
        new_state = (res[2], res[3].reshape(bsz, 2, N_HEAD, DH), res[4][:, :, :, 0, 0])
    return o_f, o_b, new_state


def _gla_direction(q, k, la, v, s_ref, d, o_ref, *, rev, dk):
    L, S = CHUNK_G, SUB_G
    nsub = L // S
    mask = _causal_mask(L, rev)
    b = _cumsum_rows(la, rev)
    bx = b - la
    ref_rows = [bx[i * S + (S - 1 if rev else 0):i * S + (S - 1 if rev else 0) + 1, :]
                for i in range(nsub)]
    bref = jnp.concatenate([jnp.broadcast_to(r, (S, r.shape[1])) for r in ref_rows], axis=0)
    q_hat = q * jnp.exp(b - bref)
    rowid = lax.broadcasted_iota(jnp.int32, (L, 1), 0)
    k_hat = []
    for i in range(nsub):
        seen = (rowid >= i * S) if rev else (rowid < (i + 1) * S)
        k_hat.append(k * jnp.exp(jnp.where(seen, ref_rows[i] - b, NEG_BIG)))
    last = 0 if rev else L - 1
    b_last = b[last:last + 1, :]
    q_state = q * jnp.exp(b)
    k_state = k * jnp.exp(b_last - b)
    e_last = jnp.exp(b_last)
    yield
    ksl = [slice(h * dk, (h + 1) * dk) for h in range(N_HEAD)]
    vsl = [slice(h * DH, (h + 1) * DH) for h in range(N_HEAD)]
    blocks = [[_bdot_nt(q_hat[i * S:(i + 1) * S, ksl[h]], k_hat[i][:, ksl[h]]) for i in range(nsub)]
              for h in range(N_HEAD)]
    states = [s_ref[d, h] for h in range(N_HEAD)]
    inter = [_bdot_nt(q_state[:, ksl[h]], states[h]) for h in range(N_HEAD)]
    grown = [_bdot_tn(v[:, vsl[h]], k_state[:, ksl[h]]) for h in range(N_HEAD)]
    yield
    for h in range(N_HEAD):
        scores = jnp.where(mask, jnp.concatenate(blocks[h], axis=0), 0.0)
        o_ref[:, vsl[h]] = inter[h] + _bdot(scores, v[:, vsl[h]])
    yield
    for h in range(N_HEAD):
        s_ref[d, h] = states[h] * e_last[:, ksl[h]] + grown[h]


def _scan_state_io(ins, pos, has_init, emit_state, n_out):
    s0_ref = sout_ref = None
    if has_init:
        s0_ref = ins[pos]
        pos += 1
    outs = ins[pos:pos + n_out]
    pos += n_out
    if emit_state:
        sout_ref = ins[pos]
        pos += 1
    return s0_ref, outs, sout_ref, ins[pos]


def _scan_state_phase(phase, s_ref, s0_ref, sout_ref):
    if phase == 'init':
        if s0_ref is not None:
            s_ref[...] = s0_ref[...]
        else:
            s_ref[...] = jnp.zeros_like(s_ref)
    elif phase == 'emit' and sout_ref is not None:
        sout_ref[...] = s_ref[...]
    return phase != 'main'


def _hgrn_kernel(*refs, has_init, emit_state, phase):
    ins = list(refs)
    lb_ref = ins[6]
    s0_ref, outs, sout_ref, s_ref = _scan_state_io(ins, 7, has_init, emit_state, 2)
    if _scan_state_phase(phase, s_ref, s0_ref, sout_ref):
        return
    chains = []
    for d in range(2):
        q_ref, z_ref, v_ref = ins[3 * d:3 * d + 3]
        z = z_ref[...]
        lb = lb_ref[d:d + 1, :]
        la = _log_sigmoid(z) + jnp.log1p(lb * jnp.exp(jnp.minimum(-z, MAX_EXP_ARG)))
        k = (1.0 - lb) * _sigmoid(-z)
        q = q_ref[...].astype(F32) * (DH ** -0.5)
        chains.append(_gla_direction(q, k, la, v_ref[...], s_ref, d, outs[d], rev=d == 1, dk=DH))
    yield from _lockstep(chains)


def _gla_kernel(*refs, has_init, emit_state, phase):
    ins = list(refs)
    up_ref, upb_ref = ins[8], ins[9]
    s0_ref, outs, sout_ref, s_ref = _scan_state_io(ins, 10, has_init, emit_state, 2)
    if _scan_state_phase(phase, s_ref, s0_ref, sout_ref):
        return
    chains = []
    for d in range(2):
        q_ref, k_ref, v_ref, sm_ref = ins[4 * d:4 * d + 4]
        zg = jnp.dot(sm_ref[...], up_ref[d], precision=HIGHEST,
                     preferred_element_type=F32) + upb_ref[d:d + 1, :]
        la = _log_sigmoid(zg) * (1.0 / GLA_TAU)
        q = q_ref[...].astype(F32) * (DK_C ** -0.5)
        k = k_ref[...].astype(F32)
        chains.append(_gla_direction(q, k, la, v_ref[...], s_ref, d, outs[d], rev=d == 1,
                                     dk=DK_C))
    yield from _lockstep(chains)


def _gated_scan(kernel, p, bsz, seq, cols, extra, extra_specs, dk, state_t, emit_state, name):
    L = CHUNK_G
    nc = seq // L
    n = bsz * seq
    g = min(bsz, SCAN_GROUP)
    has_init = state_t is not None
    src = {'wide': p[0].reshape(bsz, seq, N_WIDE), 'fine': p[1].reshape(bsz, seq, N_FINE)}
    fwd, bwd = _scan_maps(nc)

    in_specs, args = [], []
    for d, mk in enumerate((fwd, bwd)):
        for which, off, width in cols[d]:
            in_specs.append(pl.BlockSpec((g, L, width), mk(off // width)))
            args.append(src[which])
    shared = tuple(range(len(args), len(args) + len(extra)))
    in_specs += extra_specs
    args += extra
    st_spec = pl.BlockSpec((g, 2, N_HEAD, DH, dk), lambda b, c: (b, 0, 0, 0, 0))
    if has_init:
        in_specs.append(st_spec)
        args.append(state_t)
    out_specs = [pl.BlockSpec((g, L, MIX_W), fwd(0)), pl.BlockSpec((g, L, MIX_W), bwd(0))]
    out_shape = [jax.ShapeDtypeStruct((bsz, seq, MIX_W), F32)] * 2
    if emit_state:
        out_specs.append(st_spec)
        out_shape.append(jax.ShapeDtypeStruct((bsz, 2, N_HEAD, DH, dk), F32))
    body = functools.partial(kernel, has_init=has_init, emit_state=emit_state)
    res = pl.pallas_call(
        _per_sequence(body, g, shared),
        grid=(bsz // g, nc),
        in_specs=in_specs,
        out_specs=out_specs,
        out_shape=out_shape,
        scratch_shapes=[pltpu.VMEM((g, 2, N_HEAD, DH, dk), F32)],
        compiler_params=_cparams(("parallel", "arbitrary")),
        name=name,
    )(*args)
    return res[0].reshape(n, MIX_W), res[1].reshape(n, MIX_W), (res[2] if emit_state else None)


def _hgrn(p, bsz, seq, lb, state_t, emit_state):
    cols = [[('wide', COL_BQ, MIX_W), ('fine', COL_BF + d * MIX_W, MIX_W), ('wide', COL_BI, MIX_W)]
            for d in range(2)]
    return _gated_scan(_hgrn_kernel, p, bsz, seq, cols, [lb],
                       [pl.BlockSpec((2, MIX_W), lambda b, c: (0, 0))], DH, state_t, emit_state,
                       "hgrn2_scan")


def _gla(p, bsz, seq, up_pad, up_b, state_t, emit_state):
    kw = N_HEAD * DK_C
    cols = [[('wide', COL_CQ, kw), ('wide', COL_CK, kw), ('wide', COL_CV, MIX_W),
             ('fine', COL_SMALL, SMALL_W)] for _ in range(2)]
    return _gated_scan(_gla_kernel, p, bsz, seq, cols, [up_pad, up_b],
                       [pl.BlockSpec((2, SMALL_W, kw), lambda b, c: (0, 0, 0)),
                        pl.BlockSpec((2, kw), lambda b, c: (0, 0))], DK_C, state_t, emit_state,
                       "gla_scan")


def _shift_rows(x, k, rev, fill):
    n = x.shape[0]
    rowid = lax.broadcasted_iota(jnp.int32, x.shape, 0)
    if rev:
        return jnp.where(rowid >= n - k, fill, pltpu.roll(x, n - k, 0))
    return jnp.where(rowid < k, fill, pltpu.roll(x, k, 0))


def _lru_kernel(*refs, seq, has_init, emit_state):
    L = CHUNK_D
    nc = seq // L
    ins = list(refs)
    (dx_ref, dg_ref, cw_ref, cb_ref, wah_ref, wal_ref, ba_ref, wxh_ref, wxl_ref, bx_ref,
     lam_ref) = ins[0:11]
    pos = 11
    h0_ref = hout_ref = None
    if has_init:
        h0_ref = ins[pos]
        pos += 1
    y_ref = ins[pos]
    pos += 1
    if emit_state:
        hout_ref = ins[pos]
        pos += 1
    pad_ref, hb_ref, xd_ref, xh_ref, xl_ref = ins[pos:pos + 5]
    width = LRU_HEADS * LANE

    zeros8 = jnp.zeros((8, width), F32)
    pad_ref[0:8, :] = zeros8
    pad_ref[8:8 + seq, :] = dx_ref[...]
    pad_ref[8 + seq:16 + seq, :] = zeros8
    lam = lam_ref[...]
    sp = jnp.maximum(-lam, 0.0) + jnp.log1p(jnp.exp(-jnp.abs(lam)))
    cw = cw_ref[...]
    cb = cb_ref[...]

    def conv_body(c, carry):
        start = pl.multiple_of(c * L, L)
        win = pad_ref[pl.ds(start, L + 16), :]
        xd = cb + sum(cw[j:j + 1, :] * win[7 + j:7 + j + L, :] for j in range(CONV_W))
        x_hi, x_lo = _split(xd)
        xd_ref[pl.ds(start, L), :] = xd
        xh_ref[pl.ds(start, L), :] = x_hi
        xl_ref[pl.ds(start, L), :] = x_lo
        return carry

    lax.fori_loop(0, nc, conv_body, 0)

    def gate_dot(x_hi, x_lo, w_hi, w_lo):
        return (jnp.dot(x_hi, w_hi, preferred_element_type=F32)
                + jnp.dot(x_lo, w_hi, preferred_element_type=F32)
                + jnp.dot(x_hi, w_lo, preferred_element_type=F32))

    def chunk_scan(c, carry, k, d, out):
        rev = d == 1
        cols = slice(k * LANE, (k + 1) * LANE)
        start = pl.multiple_of(c * L, L)
        rows = pl.ds(start, L)
        xd, x_hi, x_lo = xd_ref[rows, cols], xh_ref[rows, cols], xl_ref[rows, cols]
        yield
        za = gate_dot(x_hi, x_lo, wah_ref[d, k], wal_ref[d, k])
        zx = gate_dot(x_hi, x_lo, wxh_ref[d, k], wxl_ref[d, k])
        yield
        r = _sigmoid(za + ba_ref[d:d + 1, cols])
        ig = _sigmoid(zx + bx_ref[d:d + 1, cols])
        log_a = -LRU_C * r * sp[d:d + 1, cols]
        a = jnp.exp(log_a)
        u = jnp.sqrt(jnp.maximum(-_expm1(2.0 * log_a), 0.0)) * (ig * xd)
        yield
        step = 1
        while step < L:
            u = a * _shift_rows(u, step, rev, 0.0) + u
            a = a * _shift_rows(a, step, rev, 1.0)
            step *= 2
            yield
        h = a * carry + u
        last = 0 if rev else L - 1
        out.extend((start, h, h[last:last + 1, :]))

    chains = [(k, d) for k in range(LRU_HEADS) for d in range(2)]

    def scan_body(i, carry):
        res = [[] for _ in chains]
        gens = [chunk_scan(i if d == 0 else nc - 1 - i, carry[n], k, d, res[n])
                for n, (k, d) in enumerate(chains)]
        for _ in _lockstep(gens):
            pass
        for n, (k, d) in enumerate(chains):
            dst = y_ref if d == 0 else hb_ref
            dst[pl.ds(res[n][0], L), k * LANE:(k + 1) * LANE] = res[n][1]
        return tuple(r[2] for r in res)

    def gate_body(c, carry):
        rows = pl.ds(pl.multiple_of(c * L, L), L)
        y_ref[rows, :] = (y_ref[rows, :] + hb_ref[rows, :]) * _gelu(dg_ref[rows, :].astype(F32))
        return carry

    if has_init:
        init = tuple(h0_ref[0, d:d + 1, k * LANE:(k + 1) * LANE] for k, d in chains)
    else:
        init = tuple(jnp.zeros((1, LANE), F32) for _ in chains)
    fin = lax.fori_loop(0, nc, scan_body, init)
    lax.fori_loop(0, nc, gate_body, 0)
    if emit_state:
        for n, (k, d) in enumerate(chains):
            hout_ref[0, d:d + 1, k * LANE:(k + 1) * LANE] = fin[n]


def _lru(pw, pf, bsz, seq, conv_w, conv_b, wa, ba, wx, bx, lam, h0, emit_state):
    n = bsz * seq
    has_init = h0 is not None
    wah, wal = _split(wa)
    wxh, wxl = _split(wx)
    width = LRU_HEADS * LANE
    gate_w = pl.BlockSpec((2, LRU_HEADS, DH, DH), lambda b, h: (0, h, 0, 0))
    vec2 = pl.BlockSpec((2, width), lambda b, h: (0, h))
    in_specs = [pl.BlockSpec((seq, width), lambda b, h: (b, COL_DX // width + h)),
                pl.BlockSpec((seq, width), lambda b, h: (b, COL_DG // width + h)),
                pl.BlockSpec((CONV_W, width), lambda b, h: (0, h)),
                pl.BlockSpec((1, width), lambda b, h: (0, h)),
                gate_w, gate_w, vec2, gate_w, gate_w, vec2, vec2]
    args = [pf, pw, conv_w, conv_b.reshape(1, MIX_W), wah, wal, ba, wxh, wxl, bx, lam]
    st_spec = pl.BlockSpec((1, 2, width), lambda b, h: (b, 0, h))
    if has_init:
        in_specs.append(st_spec)
        args.append(h0)
    out_specs = [pl.BlockSpec((seq, width), lambda b, h: (b, h))]
    out_shape = [jax.ShapeDtypeStruct((n, MIX_W), F32)]
    if emit_state:
        out_specs.append(st_spec)
        out_shape.append(jax.ShapeDtypeStruct((bsz, 2, MIX_W), F32))
    res = pl.pallas_call(
        functools.partial(_lru_kernel, seq=seq, has_init=has_init, emit_state=emit_state),
        grid=(bsz, N_HEAD // LRU_HEADS),
        in_specs=in_specs,
        out_specs=out_specs,
        out_shape=out_shape,
        scratch_shapes=[pltpu.VMEM((seq + 16, width), F32), pltpu.VMEM((seq, width), F32),
                        pltpu.VMEM((seq, width), F32), pltpu.VMEM((seq, width), BF16),
                        pltpu.VMEM((seq, width), BF16)],
        compiler_params=_cparams(("parallel", "parallel")),
        name="conv_rglru",
    )(*args)
    return res[0], (res[1] if emit_state else None)


def _head_rms(x):
    parts = []
    for h in range(N_HEAD):
        xh = x[:, h * DH:(h + 1) * DH]
        parts.append(xh * lax.rsqrt(jnp.mean(xh * xh, axis=-1, keepdims=True) + EPS))
    return jnp.concatenate(parts, axis=-1)


def _merge_kernel(af_ref, ab_ref, bf_ref, bb_ref, cf_ref, cb_ref, yd_ref, ao_ref, bg_ref, cg_ref,
                  gt0_ref, gt1_ref, gt2_ref, gt3_ref, x_ref, mod_ref, g2_ref, wbr_ref, wout_ref,
                  xo_ref, h2_ref, h2b_ref):
    ya = _sigmoid(ao_ref[...]).astype(F32) * _head_rms(af_ref[...] + ab_ref[...])
    yb = _silu(bg_ref[...]).astype(F32) * _head_rms(bf_ref[...] + bb_ref[...])
    yc = _silu(cg_ref[...]).astype(F32) * _head_rms(cf_ref[...] + cb_ref[...])
    merged = None
    gate_refs = (gt0_ref, gt1_ref, gt2_ref, gt3_ref)
    for i, y in enumerate((ya, yb, yc, yd_ref[...])):
        proj = jnp.dot(y.astype(BF16), wbr_ref[i], preferred_element_type=F32)
        term = _sigmoid(gate_refs[i][...]) * proj.astype(BF16)
        merged = term if merged is None else merged + term
    out = jnp.dot(merged, wout_ref[...], preferred_element_type=F32)
    g1 = mod_ref[0, :, 2 * D_MODEL:3 * D_MODEL]
    sh2 = mod_ref[0, :, 3 * D_MODEL:4 * D_MODEL]
    sc2 = mod_ref[0, :, 4 * D_MODEL:5 * D_MODEL]
    x = x_ref[...] + g1 * out
    xo_ref[...] = x
    inv = lax.rsqrt(jnp.mean(x * x, axis=-1, keepdims=True) + EPS)
    h2 = x * inv * g2_ref[...] * (1.0 + sc2) + sh2
    h2_ref[...] = h2
    h2b_ref[...] = h2.astype(BF16)


def _merge(mix_outs, yd, p, x, mod_l, norm2_g, w_branch, w_out, seq_len, layer):
    n = x.shape[0]
    tn = 256
    rows = mod_l.shape[0]
    mod_map = (lambda i: (0, 0, 0)) if rows == 1 else (lambda i: ((i * tn) // seq_len, 0, 0))
    tok = lambda cb: (lambda i: (i, cb))
    in_specs = [pl.BlockSpec((tn, MIX_W), tok(0))] * 7
    in_specs += [pl.BlockSpec((tn, MIX_W), tok(COL_AO // MIX_W)),
                 pl.BlockSpec((tn, MIX_W), tok(COL_BG // MIX_W)),
                 pl.BlockSpec((tn, MIX_W), tok(COL_CG // MIX_W)),
                 *[pl.BlockSpec((tn, D_MODEL), tok(COL_GATES // D_MODEL + i)) for i in range(N_BRANCH)],
                 pl.BlockSpec((tn, D_MODEL), tok(0)),
                 pl.BlockSpec((1, 1, 6 * D_MODEL), mod_map),
                 pl.BlockSpec((1, D_MODEL), lambda i: (0, 0)),
                 pl.BlockSpec((None, N_BRANCH, MIX_W, D_MODEL), lambda i: (layer, 0, 0, 0)),
                 pl.BlockSpec((None, D_MODEL, D_MODEL), lambda i: (layer, 0, 0))]
    out_spec = pl.BlockSpec((tn, D_MODEL), tok(0))
    return pl.pallas_call(
        _merge_kernel,
        grid=(n // tn,),
        in_specs=in_specs,
        out_specs=[out_spec, out_spec, out_spec],
        out_shape=[jax.ShapeDtypeStruct((n, D_MODEL), F32), jax.ShapeDtypeStruct((n, D_MODEL), F32),
                   jax.ShapeDtypeStruct((n, D_MODEL), BF16)],
        compiler_params=_cparams(("parallel",)),
        name="branch_merge",
    )(*mix_outs, yd, p, p, p, p, p, p, p, x, mod_l, norm2_g, w_branch, w_out)


ROUTE_TN = 512
ROUTE_LG = LANE


def _top16_exact(vals):
    n_rows = vals.shape[0]
    rowid = lax.broadcasted_iota(jnp.int32, vals.shape, 0).astype(F32)
    rank = jnp.full(vals.shape, float(PEER_TOPK), F32)
    tops = []
    for r in range(PEER_TOPK):
        m = jnp.max(vals, axis=0, keepdims=True)
        idx = jnp.min(jnp.where(vals == m, rowid, float(n_rows)), axis=0, keepdims=True)
        sel = rowid == idx
        rank = jnp.where(sel, float(r), rank)
        vals = jnp.where(sel, -jnp.inf, vals)
        tops.append(m)
    return jnp.concatenate(tops, axis=0), rank


_TAKEN_BASE = -3.0e38
_TAKEN_STEP = 2.0e36
_TAKEN_BELOW = _TAKEN_BASE + 0.5 * _TAKEN_STEP


def _top16_quick_many(vals_list):
    work = list(vals_list)
    tops = [[] for _ in work]
    for r in range(PEER_TOPK):
        for i in range(len(work)):
            m = jnp.max(work[i], axis=0, keepdims=True)
            work[i] = jnp.where(work[i] == m, _TAKEN_BASE - r * _TAKEN_STEP, work[i])
            tops[i].append(m)
    out = []
    for i in range(len(work)):
        taken = work[i] <= _TAKEN_BELOW
        rank = jnp.where(taken, jnp.round((_TAKEN_BASE - work[i]) * (1.0 / _TAKEN_STEP)),
                         float(PEER_TOPK))
        n_taken = jnp.sum(jnp.where(taken, 1.0, 0.0), axis=0, keepdims=True)
        out.append((jnp.concatenate(tops[i], axis=0), rank, n_taken))
    return out


_CAND_PIECES = (('a', 0, 0, 16), ('a', 0, 8, 16), ('a', 1, 0, 8), ('a', 2, 0, 5), ('a', 3, 0, 4),
                ('b', 0, 0, (4, 8)), ('b', 0, 8, (8, 16)), ('b', 1, 0, (4, 8)), ('b', 2, 0, (4, 5)))


def _pair_pieces(x1, x2, combine):
    out = []
    for kind, fixed, off, _ in _CAND_PIECES:
        if kind == 'a':
            out.append(combine(x1[fixed:fixed + 1, :], x2[off:off + 8, :]))
        else:
            out.append(combine(x1[off:off + 8, :], x2[fixed:fixed + 1, :]))
    return out


def _candidate_sums(s1, s2):
    K = PEER_TOPK
    n = s1.shape[1]
    iota8 = lax.broadcasted_iota(jnp.int32, (8, n), 0)
    sums = _pair_pieces(s1, s2, lambda x, y: x + y)
    vals, poss = [], []
    for (kind, fixed, off, lim), sm in zip(_CAND_PIECES, sums):
        idx = iota8 + off
        if kind == 'a':
            valid = idx < lim
            pos = fixed * K + idx
        else:
            valid = (idx >= lim[0]) & (idx < lim[1])
            pos = idx * K + fixed
        vals.append(jnp.where(valid, sm, -jnp.inf))
        poss.append(jnp.where(valid, pos, K * K).astype(F32))
    return jnp.concatenate(vals, axis=0), jnp.concatenate(poss, axis=0)


def _choose_exact(vals, posid):
    K = PEER_TOPK
    chosen = jnp.zeros(vals.shape, F32)
    for _ in range(K):
        m = jnp.max(vals, axis=0, keepdims=True)
        idx = jnp.min(jnp.where(vals == m, posid, float(K * K)), axis=0, keepdims=True)
        sel = posid == idx
        chosen = jnp.where(sel, 1.0, chosen)
        vals = jnp.where(sel, -jnp.inf, vals)
    return chosen


def _choose_quick_many(vals_list):
    work = list(vals_list)
    for _ in range(PEER_TOPK):
        for i in range(len(work)):
            m = jnp.max(work[i], axis=0, keepdims=True)
            work[i] = jnp.where(work[i] == m, _TAKEN_BASE, work[i])
    out = []
    for w in work:
        chosen = jnp.where(w == _TAKEN_BASE, 1.0, 0.0)
        out.append((chosen, jnp.sum(chosen, axis=0, keepdims=True)))
    return out


def _route_kernel(h_ref, wqh_ref, wql_ref, kh_ref, kl_ref, a_ref, b_ref, r2_ref, c_ref,
                  qh_ref, ql_ref, s_ref, top_ref, rank_ref, ch_ref):
    K = PEER_TOPK
    n_lg = ROUTE_TN // ROUTE_LG
    q = _dot3(h_ref[...], wqh_ref[...], wql_ref[...])
    q_hi, q_lo = _split(q)
    for i in range(2 * PEER_HEADS):
        qh_ref[i] = q_hi[:, i * N_KEYS:(i + 1) * N_KEYS]
        ql_ref[i] = q_lo[:, i * N_KEYS:(i + 1) * N_KEYS]
    nt = (((1,), (1,)), ((), ()))
    lanes = [slice(lg * ROUTE_LG, (lg + 1) * ROUTE_LG) for lg in range(n_lg)]

    def miscount(counts):
        worst = None
        for cnt in counts:
            dev = jnp.abs(cnt - float(K))
            worst = dev if worst is None else jnp.maximum(worst, dev)
        return jnp.max(worst) > 0.0

    def head_body(h, carry):
        for half in range(2):
            kh, kl = kh_ref[h, half], kl_ref[h, half]
            qh, ql = qh_ref[2 * h + half], ql_ref[2 * h + half]
            s_ref[half] = (lax.dot_general(kh, qh, nt, preferred_element_type=F32)
                           + lax.dot_general(kl, qh, nt, preferred_element_type=F32)
                           + lax.dot_general(kh, ql, nt, preferred_element_type=F32))

        problems = [(half, lg) for half in range(2) for lg in range(n_lg)]
        quick = _top16_quick_many([s_ref[half, :, lanes[lg]] for half, lg in problems])
        counts = []
        for (half, lg), (top, rank, n_taken) in zip(problems, quick):
            top_ref[half, lg] = top
            rank_ref[half, lg] = rank
            counts.append(n_taken)

        @pl.when(miscount(counts))
        def _():
            for half in range(2):
                for lg in range(n_lg):
                    top, rank = _top16_exact(s_ref[half, :, lanes[lg]])
                    top_ref[half, lg] = top
                    rank_ref[half, lg] = rank

        cands = [_candidate_sums(top_ref[0, lg], top_ref[1, lg])[0] for lg in range(n_lg)]
        counts = []
        for lg, (chosen, n_taken) in enumerate(_choose_quick_many(cands)):
            ch_ref[lg] = chosen
            counts.append(n_taken)

        @pl.when(miscount(counts))
        def _():
            for lg in range(n_lg):
                vals, posid = _candidate_sums(top_ref[0, lg], top_ref[1, lg])
                ch_ref[lg] = _choose_exact(vals, posid)

        for lg in range(n_lg):
            ls = lanes[lg]
            st1, st2 = s_ref[0, :, ls], s_ref[1, :, ls]
            s1, s2 = top_ref[0, lg], top_ref[1, lg]
            rank1, rank2 = rank_ref[0, lg], rank_ref[1, lg]
            chosen = ch_ref[lg]
            ch = [chosen[8 * i:8 * i + 8, :] for i in range(len(_CAND_PIECES))]
            e1 = jnp.exp(s1 - s1[0:1, :])
            e2 = jnp.exp(s2 - s2[0:1, :])
            pair = _pair_pieces(e1, e2, lambda x, y: x * y)
            z = sum(jnp.sum(c * p, axis=0, keepdims=True) for c, p in zip(ch, pair))
            iota8 = lax.broadcasted_iota(jnp.int32, (8, ROUTE_LG), 0)
            low = ch[5] + ch[7] + ch[8]
            for a, cnt in ((3, ch[4]), (2, ch[3]), (1, ch[2]), (0, ch[0] + ch[1])):
                low = jnp.where(iota8 == a, jnp.sum(cnt, axis=0, keepdims=True), low)
            counts = jnp.concatenate([low, ch[6]], axis=0)
            c_dense = jnp.zeros((N_KEYS, ROUTE_LG), F32)
            for a in range(K):
                c_dense = jnp.where(rank1 == float(a), counts[a:a + 1, :], c_dense)
            a_ref[h, :, ls] = jnp.where(rank1 < float(K), jnp.exp(st1 - s1[0:1, :]), 0.0) / z
            b_ref[h, :, ls] = jnp.where(rank2 < float(K), jnp.exp(st2 - s2[0:1, :]), 0.0).astype(BF16)
            r2_ref[h, :, ls] = rank2.astype(BF16)
            c_ref[h, :, ls] = c_dense
        return carry

    lax.fori_loop(0, PEER_HEADS, head_body, 0)


def _route(h2, wq_hi, wq_lo, keys_hi, keys_lo, layer):
    n = h2.shape[0]
    tn = ROUTE_TN
    dense = pl.BlockSpec((PEER_HEADS, N_KEYS, tn), lambda i: (0, 0, i))
    wspec = pl.BlockSpec((None, D_MODEL, PEER_HEADS * 2 * N_KEYS), lambda i: (layer, 0, 0))
    kspec = pl.BlockSpec((None, PEER_HEADS, 2, N_KEYS, N_KEYS), lambda i: (layer, 0, 0, 0, 0))
    return pl.pallas_call(
        _route_kernel,
        grid=(n // tn,),
        in_specs=[pl.BlockSpec((tn, D_MODEL), lambda i: (i, 0)), wspec, wspec, kspec, kspec],
        out_specs=[dense] * 4,
        out_shape=[jax.ShapeDtypeStruct((PEER_HEADS, N_KEYS, n), dt) for dt in (F32, BF16, BF16, F32)],
        scratch_shapes=[pltpu.VMEM((2 * PEER_HEADS, tn, N_KEYS), BF16),
                        pltpu.VMEM((2 * PEER_HEADS, tn, N_KEYS), BF16),
                        pltpu.VMEM((2, N_KEYS, tn), F32),
                        pltpu.VMEM((2, tn // ROUTE_LG, PEER_TOPK, ROUTE_LG), F32),
                        pltpu.VMEM((2, tn // ROUTE_LG, N_KEYS, ROUTE_LG), F32),
                        pltpu.VMEM((tn // ROUTE_LG, 8 * len(_CAND_PIECES), ROUTE_LG), F32)],
        compiler_params=_cparams(("parallel",)),
        name="peer_route",
    )(h2, wq_hi, wq_lo, keys_hi, keys_lo)


PEER_TN = 512
PEER_TE = 1024


def _gelu_bf16(t):
    x = t.astype(BF16)
    c1 = -2.0 * 0.7978845608028654
    y2 = x * (c1 + (c1 * 0.044715) * (x * x))
    return x / (1.0 + jnp.exp(y2))


BF16_ROWS = 16


def _row_to_packed(row):
    tile = jnp.broadcast_to(row, (BF16_ROWS, row.shape[1])).astype(BF16)
    return jnp.concatenate([tile] * (N_KEYS // BF16_ROWS), axis=0)


def _expert_kernel(hb_ref, u_ref, vt_ref, a_ref, b_ref, r2_ref, c_ref, x_ref, mod_ref, o_ref,
                   acc_ref):
    e = pl.program_id(1)
    rows_per_tile = PEER_TE // N_KEYS

    @pl.when(e == 0)
    def _():
        acc_ref[...] = jnp.zeros_like(acc_ref)

    t_t = lax.dot_general(u_ref[...], hb_ref[...], (((1,), (1,)), ((), ())),
                          preferred_element_type=F32)
    w_parts = []
    for j in range(rows_per_tile):
        e1 = e * rows_per_tile + j
        g = None
        for h in range(PEER_HEADS):
            a_row = _row_to_packed(a_ref[h, pl.ds(e1, 1), :])
            c_row = _row_to_packed(c_ref[h, pl.ds(e1, 1), :])
            term = a_row * jnp.where(r2_ref[h] < c_row, b_ref[h], 0.0)
            g = term if g is None else g + term
        w_parts.append(g * _gelu_bf16(t_t[j * N_KEYS:(j + 1) * N_KEYS, :]))
    w_t = jnp.concatenate(w_parts, axis=0)
    acc_ref[...] += jnp.dot(vt_ref[...], w_t, preferred_element_type=F32)

    @pl.when(e == pl.num_programs(1) - 1)
    def _():
        g2 = mod_ref[0, :, 5 * D_MODEL:6 * D_MODEL]
        o_ref[...] = x_ref[...] + g2 * acc_ref[...].T


def _experts(h2b, u_b, vt_b, dense, x, mod_l, seq_len, layer):
    n = x.shape[0]
    tn, te = PEER_TN, PEER_TE
    rows = mod_l.shape[0]
    mod_map = (lambda i, e: (0, 0, 0)) if rows == 1 else (lambda i, e: ((i * tn) // seq_len, 0, 0))
    dspec = pl.BlockSpec((PEER_HEADS, N_KEYS, tn), lambda i, e: (0, 0, i))
    return pl.pallas_call(
        _expert_kernel,
        grid=(n // tn, N_EXPERTS // te),
        in_specs=[pl.BlockSpec((tn, D_MODEL), lambda i, e: (i, 0)),
                  pl.BlockSpec((None, te, D_MODEL), lambda i, e: (layer, e, 0)),
                  pl.BlockSpec((None, D_MODEL, te), lambda i, e: (layer, 0, e)),
                  dspec, dspec, dspec, dspec,
                  pl.BlockSpec((tn, D_MODEL), lambda i, e: (i, 0)),
                  pl.BlockSpec((1, 1, 6 * D_MODEL), mod_map)],
        out_specs=pl.BlockSpec((tn, D_MODEL), lambda i, e: (i, 0)),
        out_shape=jax.ShapeDtypeStruct((n, D_MODEL), F32),
        scratch_shapes=[pltpu.VMEM((D_MODEL, tn), F32)],
        compiler_params=_cparams(("parallel", "arbitrary")),
        name="peer_experts",
    )(h2b, u_b, vt_b, *dense, x, mod_l)


def _final_norm_kernel(x_ref, g_ref, o_ref):
    x = x_ref[...]
    o_ref[...] = x * lax.rsqrt(jnp.mean(x * x, axis=-1, keepdims=True) + EPS) * g_ref[...]


def _final_norm(x, g):
    n = x.shape[0]
    tn = 512
    return pl.pallas_call(
        _final_norm_kernel,
        grid=(n // tn,),
        in_specs=[pl.BlockSpec((tn, D_MODEL), lambda i: (i, 0)),
                  pl.BlockSpec((1, D_MODEL), lambda i: (0, 0))],
        out_specs=pl.BlockSpec((tn, D_MODEL), lambda i: (i, 0)),
        out_shape=jax.ShapeDtypeStruct((n, D_MODEL), F32),
        compiler_params=_cparams(("parallel",)),
        name="final_norm",
    )(x, g)


def _pack_columns(a, src, width):
    parts = [a[..., lo:hi] for lo, hi in src]
    used = sum(hi - lo for lo, hi in src)
    if used < width:
        parts.append(jnp.zeros(a.shape[:-1] + (width - used,), a.dtype))
    return jnp.concatenate(parts, axis=-1)


def _position_code(rows):
    quarter = D_MODEL // 4
    omega = 1.0 / (POS_BASE ** (jnp.arange(quarter, dtype=F32) / quarter))
    r, col = jnp.meshgrid(jnp.arange(rows, dtype=F32), jnp.arange(GRID_W, dtype=F32), indexing='ij')

    def enc(pos):
        ang = pos.reshape(-1, 1) * omega
        return jnp.concatenate([jnp.sin(ang), jnp.cos(ang)], axis=-1)
    return jnp.concatenate([enc(r), enc(col)], axis=-1)


def _layer(x, bsz, seq, mod_l, lp, state, emit_state):
    layer = lp['layer']
    pw = _inproj(x, mod_l, lp['norm1_g'], lp['w_wide'], lp['b_wide'], seq, layer, 2432, BF16,
                 "in_projection_wide")
    pf = _inproj(x, mod_l, lp['norm1_g'], lp['w_fine'], lp['b_fine'], seq, layer, N_FINE, F32,
                 "in_projection_fine")
    if state is None:
        st_a = st_b = st_c = st_d = None
    else:
        c0, n0, m0, sb0, sc0, hd0 = state
        st_a = (c0, n0, m0)
        st_b = jnp.swapaxes(sb0, -1, -2)
        st_c = jnp.swapaxes(sc0, -1, -2)
        st_d = hd0
    a_f, a_b, new_a = _mlstm(pw, pf, bsz, seq, st_a, emit_state)
    b_f, b_b, new_b = _hgrn((pw, pf), bsz, seq, lp['hgrn_lb'], st_b, emit_state)
    c_f, c_b, new_c = _gla((pw, pf), bsz, seq, lp['gla_up_pad'], lp['gla_up_b'], st_c, emit_state)
    yd, new_d = _lru(pw, pf, bsz, seq, lp['conv_w'], lp['conv_b'], lp['lru_w_a'], lp['lru_b_a'],
                     lp['lru_w_x'], lp['lru_b_x'], lp['lru_lambda'], st_d, emit_state)
    x1, h2, h2b = _merge((a_f, a_b, b_f, b_b, c_f, c_b), yd, pw, x, mod_l, lp['norm2_g'],
                         lp['w_branch'], lp['w_out'], seq, layer)
    dense = _route(h2, lp['wq_hi'], lp['wq_lo'], lp['keys_hi'], lp['keys_lo'], layer)
    x2 = _experts(h2b, lp['peer_u'], lp['peer_vt'], dense, x1, mod_l, seq, layer)
    new_state = None
    if emit_state:
        new_state = (*new_a, jnp.swapaxes(new_b, -1, -2), jnp.swapaxes(new_c, -1, -2), new_d)
    return x2, new_state


def kernel(x_prompt, x_sample, state_mlstm_C, state_mlstm_n, state_mlstm_m, state_hgrn_S,
           state_gla_S, state_lru_h, c, c_ctx, norm1_g, norm2_g, final_norm_g, w_mod, b_mod,
           w_in, b_in, w_gla_up, b_gla_up, hgrn_lower_bounds, conv_w, conv_b, lru_w_a, lru_b_a,
           lru_w_x, lru_b_x, lru_lambda, w_branch, w_out, peer_w_q, peer_sub_keys, peer_u, peer_v):
    lb_soft = jax.nn.softmax(hgrn_lower_bounds.astype(F32), axis=0)
    lb_all = jnp.cumsum(lb_soft, axis=0) - lb_soft[0:1]
    w_wide = _pack_columns(w_in, _WIDE_SRC, N_WIDE).astype(BF16)
    w_fine = _pack_columns(w_in, _FINE_SRC, N_FINE).astype(BF16)
    b_wide = _pack_columns(b_in, _WIDE_SRC, N_WIDE).reshape(DEPTH, 1, N_WIDE)
    b_fine = _pack_columns(b_in, _FINE_SRC, N_FINE).reshape(DEPTH, 1, N_FINE)
    kw = N_HEAD * DK_C
    up_pad = jnp.zeros((DEPTH, 2, SMALL_W, kw), F32)
    for d in range(2):
        lo = 2 * N_HEAD * 2 + d * R_C
        up_pad = up_pad.at[:, d, lo:lo + R_C, :].set(w_gla_up[:, d].astype(F32))
    wq_hi, wq_lo = _split(peer_w_q)
    keys_hi, keys_lo = _split(peer_sub_keys)
    u_b = peer_u.astype(BF16)
    vt_b = jnp.swapaxes(peer_v, 1, 2).astype(BF16)
    wbr_b = w_branch.astype(BF16)
    wout_b = w_out.astype(BF16)

    def layer_params(l):
        return {
            'norm1_g': norm1_g[l].reshape(1, D_MODEL), 'norm2_g': norm2_g[l].reshape(1, D_MODEL),
            'layer': l, 'w_wide': w_wide, 'b_wide': b_wide[l], 'w_fine': w_fine, 'b_fine': b_fine[l],
            'hgrn_lb': lb_all[l],
            'gla_up_pad': up_pad[l], 'gla_up_b': b_gla_up[l],
            'conv_w': conv_w[l], 'conv_b': conv_b[l], 'lru_w_a': lru_w_a[l], 'lru_b_a': lru_b_a[l],
            'lru_w_x': lru_w_x[l], 'lru_b_x': lru_b_x[l], 'lru_lambda': lru_lambda[l],
            'w_branch': wbr_b, 'w_out': wout_b, 'wq_hi': wq_hi, 'wq_lo': wq_lo,
            'keys_hi': keys_hi, 'keys_lo': keys_lo, 'peer_u': u_b, 'peer_vt': vt_b,
        }

    cond = jnp.concatenate([c, c_ctx[None, :]], axis=0).astype(F32)
    n_dec = c.shape[0]
    mod = _modulation(cond, w_mod, b_mod)
    final_g = final_norm_g.reshape(1, D_MODEL)

    bp, tp, _ = x_prompt.shape
    x = x_prompt.reshape(bp * tp, D_MODEL)
    ctx_states = []
    for l in range(DEPTH):
        mod_l = mod[l, n_dec:n_dec + 1].reshape(1, 1, 6 * D_MODEL)
        x, st = _layer(x, bp, tp, mod_l, layer_params(l), None, True)
        ctx_states.append(st)
    y_prompt = _final_norm(x, final_g).reshape(bp, tp, D_MODEL)
    new_states = tuple(jnp.stack([s[i] for s in ctx_states], axis=1) for i in range(6))

    bd, td, _ = x_sample.shape
    x = _add_position(x_sample, _position_code(td // GRID_W)).reshape(bd * td, D_MODEL)
    for l in range(DEPTH):
        cached = (state_mlstm_C[:, l].astype(F32), state_mlstm_n[:, l].astype(F32),
                  state_mlstm_m[:, l].astype(F32), state_hgrn_S[:, l].astype(F32),
                  state_gla_S[:, l].astype(F32), state_lru_h[:, l].astype(F32))
        mod_l = mod[l, 0:n_dec].reshape(n_dec, 1, 6 * D_MODEL)
        x, _ = _layer(x, bd, td, mod_l, layer_params(l), cached, False)
    y_sample = _final_norm(x, final_g).reshape(bd, td, D_MODEL)
    return (y_prompt, y_sample) + new_states
```
